```python
import math
import jax, jax.numpy as jnp
from jax import lax
import numpy as np

D_MODEL = 1024
BATCH = 8
SEQ = 2048
DEPTH = 2
DEC_BATCH = 128
DEC_SEQ = 8
PAST_LEN = 16384
PAGE_SIZE = 128

N_BRANCH = 4
CHUNK = 64
NORM_EPS = 1e-6
MIN_FORGET = 1e-30
HG_H = 4
HG_DK = 64
HG_DV = 64
HG_W = HG_H * HG_DV
GLA_H = 4
GLA_DK = 32
GLA_DV = 64
GLA_RANK = 16
GLA_TAU = 16.0
GLA_W = GLA_H * GLA_DV
RW_H = 4
RW_N = 64
RW_W = RW_H * RW_N
RW_DECAY_RANK = 64
RW_A_RANK = 64
RW_G_RANK = 128
RW_COLS = 3 * RW_W + RW_DECAY_RANK + RW_A_RANK + RW_G_RANK
RW_LN_EPS = 64e-5
RET_H = 4
RET_DK = 64
RET_DV = 64
RET_W = RET_H * RET_DV
ROPE_BASE = 10000.0
BRANCH_W = HG_W
D_FF = 256 * (-(-(8 * D_MODEL) // (3 * 256)))

IN_WIDTHS = (HG_H * HG_DK, HG_H * HG_DK, HG_W, HG_W,
             GLA_H * GLA_DK, GLA_H * GLA_DK, GLA_W, GLA_RANK, GLA_W,
             RW_COLS,
             RET_H * RET_DK, RET_H * RET_DK, RET_W, RET_W,
             N_BRANCH * D_MODEL)
N_IN = sum(IN_WIDTHS)
RW_WIDTHS = (RW_W, RW_DECAY_RANK, RW_W, RW_W, RW_A_RANK, RW_G_RANK)

kernel_name = 'hybrid_hgrn2_gla_rwkv7_retnet_step'


def _split(t, widths):
    return jnp.split(t, [int(i) for i in np.cumsum(widths)[:-1]], axis=-1)


def _heads(t, n):
    return t.reshape(t.shape[:-1] + (n, t.shape[-1] // n))


def _rmsnorm(x, w):
    xf = x.astype(jnp.float32)
    y = xf * lax.rsqrt(jnp.mean(xf * xf, axis=-1, keepdims=True) + NORM_EPS)
    return (y * w.astype(jnp.float32)).astype(x.dtype)


def _head_rms(o):
    return o * lax.rsqrt(jnp.mean(o * o, axis=-1, keepdims=True) + NORM_EPS)


def _head_layernorm(o, w, b):
    mu = jnp.mean(o, axis=-1, keepdims=True)
    var = jnp.mean(jnp.square(o - mu), axis=-1, keepdims=True)
    return (o - mu) * lax.rsqrt(var + RW_LN_EPS) * w.reshape(o.shape[-2:]) + b.reshape(o.shape[-2:])


def _rotary(x, pos):
    half = x.shape[-1] // 2
    inv = ROPE_BASE ** (-jnp.arange(half, dtype=jnp.float32) / half)
    ang = pos[:, None] * inv[None, :]
    cos = jnp.cos(ang)[None, :, None, :]
    sin = jnp.sin(ang)[None, :, None, :]
    x1, x2 = x[..., :half], x[..., half:]
    return jnp.concatenate([x1 * cos - x2 * sin, x1 * sin + x2 * cos], axis=-1)


def gated_linear_attention_chunked(q, k, v, log_g, s0):
    B, L, H, _ = q.shape
    dv = v.shape[-1]
    C = math.gcd(L, CHUNK)
    n = L // C

    def to_chunks(t):
        return t.reshape(B, n, C, H, t.shape[-1]).transpose(1, 0, 3, 2, 4)

    causal = jnp.tril(jnp.ones((C, C), dtype=jnp.float32))[None, None, :, :, None]

    def step(S, inp):
        qb, kb, vb, gb = inp
        b = jnp.cumsum(gb, axis=2)
        diff = b[:, :, :, None, :] - b[:, :, None, :, :]
        decay = jnp.exp(jnp.where(causal > 0, diff, 0.0)) * causal
        scores = jnp.sum(qb[:, :, :, None, :] * kb[:, :, None, :, :] * decay, axis=-1)
        o = (jnp.einsum('bhts,bhsv->bhtv', scores, vb)
             + jnp.einsum('bhtk,bhkv->bhtv', qb * jnp.exp(b), S))
        b_last = b[:, :, -1:, :]
        S_new = (jnp.exp(b_last[:, :, 0, :])[..., None] * S
                 + jnp.einsum('bhsk,bhsv->bhkv', kb * jnp.exp(b_last - b), vb))
        return S_new, o

    S, o = lax.scan(step, s0.astype(jnp.float32),
                    (to_chunks(q), to_chunks(k), to_chunks(v), to_chunks(log_g)))
    o = o.transpose(1, 0, 3, 2, 4).reshape(B, L, H, dv)
    return o, S


def rwkv7_recurrence(r, log_w, k, v, kk, a, s0):
    def step(S, inp):
        r_t, lw_t, k_t, v_t, kk_t, a_t = inp
        kS = jnp.einsum('bhk,bhkv->bhv', kk_t, S)
        S = (jnp.exp(lw_t)[..., None] * S
             - (a_t * kk_t)[..., None] * kS[..., None, :]
             + k_t[..., None] * v_t[..., None, :])
        return S, jnp.einsum('bhk,bhkv->bhv', r_t, S)

    xs = (r.transpose(1, 0, 2, 3), log_w.transpose(1, 0, 2, 3), k.transpose(1, 0, 2, 3),
          v.transpose(1, 0, 2, 3), kk.transpose(1, 0, 2, 3), a.transpose(1, 0, 2, 3))
    S, o = lax.scan(step, s0.astype(jnp.float32), xs)
    return o.transpose(1, 0, 2, 3), S


def token_mixing(h, pos, states, lw):
    f32 = jnp.float32
    B, L, _ = h.shape
    s_hg, s_gla, s_rw, s_shift, s_ret = states
    proj = (h @ lw['w_in']).astype(f32)
    (hg_q, hg_f, hg_i, hg_g, gla_q, gla_k, gla_v, gla_a, gla_g, rw_cols,
     ret_q, ret_k, ret_v, ret_g, gate_logits) = _split(proj, IN_WIDTHS)

    lb = lw['hg_lb']
    forget = lb + (1.0 - lb) * jax.nn.sigmoid(hg_f)
    log_f = jnp.log(jnp.maximum(forget, MIN_FORGET))
    o_hg, n_hg = gated_linear_attention_chunked(
        _heads(jax.nn.silu(hg_q), HG_H), _heads(1.0 - forget, HG_H),
        _heads(hg_i, HG_H), _heads(log_f, HG_H), s_hg)
    o_hg = (_head_rms(o_hg) * lw['hg_norm_w']).reshape(B, L, HG_W) * jax.nn.silu(hg_g)

    log_a = jax.nn.log_sigmoid(gla_a @ lw['gla_wa2'] + lw['gla_ba']) / GLA_TAU
    o_gla, n_gla = gated_linear_attention_chunked(
        _heads(gla_q * GLA_DK ** -0.5, GLA_H), _heads(gla_k, GLA_H),
        _heads(gla_v, GLA_H), _heads(log_a, GLA_H), s_gla)
    o_gla = (_head_rms(o_gla) * lw['gla_norm_w']).reshape(B, L, GLA_W) * jax.nn.silu(gla_g)

    prev = jnp.concatenate([s_shift.astype(f32)[:, None, :], rw_cols[:, :-1]], axis=1)
    rw_mix = rw_cols + (prev - rw_cols) * lw['rw_mu']
    r, w_d, k, v, a_d, g_d = _split(rw_mix, RW_WIDTHS)
    log_w = -math.exp(-0.5) * jax.nn.sigmoid(lw['rw_w0'] + jnp.tanh(w_d) @ lw['rw_w2'])
    a = jax.nn.sigmoid(lw['rw_a0'] + a_d @ lw['rw_a2'])
    g = jax.nn.sigmoid(g_d) @ lw['rw_g2']
    kk = _heads(k * lw['rw_kk'], RW_H)
    kk = kk / jnp.maximum(jnp.sqrt(jnp.sum(kk * kk, axis=-1, keepdims=True)), 1e-12)
    k = k * (1.0 + (a - 1.0) * lw['rw_ka'])
    r_h, k_h, v_h = _heads(r, RW_H), _heads(k, RW_H), _heads(v, RW_H)
    o_rw, n_rw = rwkv7_recurrence(r_h, _heads(log_w, RW_H), k_h, v_h, kk, _heads(a, RW_H), s_rw)
    bonus = jnp.sum(r_h * k_h * lw['rw_rk'].reshape(RW_H, RW_N), axis=-1, keepdims=True) * v_h
    o_rw = (_head_layernorm(o_rw, lw['rw_ln_w'], lw['rw_ln_b']) + bonus).reshape(B, L, RW_W) * g

    q_r = _rotary(_heads(ret_q, RET_H), pos)
    k_r = _rotary(_heads(ret_k, RET_H), pos) * RET_DK ** -0.5
    log_gamma = jnp.log1p(-jnp.exp2(-5.0 - jnp.arange(RET_H, dtype=f32)))
    log_gamma = jnp.broadcast_to(log_gamma[:, None], (B, L, RET_H, 1))
    o_ret, n_ret = gated_linear_attention_chunked(q_r, k_r, _heads(ret_v, RET_H), log_gamma, s_ret)
    o_ret = _head_rms(o_ret).reshape(B, L, RET_W) * jax.nn.silu(ret_g)

    branches = jnp.stack([o_hg, o_gla, o_rw, o_ret], axis=2).astype(h.dtype)
    up = jnp.einsum('blnc,ncd->blnd', branches, lw['w_branch'])
    gates = jax.nn.sigmoid(gate_logits.reshape(B, L, N_BRANCH, D_MODEL))
    merged = jnp.sum(gates * up, axis=2).astype(h.dtype)
    y = merged @ lw['w_out']
    dt = h.dtype
    new_states = (n_hg.astype(dt), n_gla.astype(dt), n_rw.astype(dt),
                  rw_cols[:, -1].astype(dt), n_ret.astype(dt))
    return y, new_states


def decoder_layer(x, pos, states, lw):
    y, new_states = token_mixing(_rmsnorm(x, lw['attn_norm']), pos, states, lw)
    x = x + y.astype(x.dtype)
    h = _rmsnorm(x, lw['ffn_norm'])
    g, u = jnp.split(h @ lw['w_ffn_in'], 2, axis=-1)
    x = x + ((jax.nn.silu(g) * u) @ lw['w_ffn_out']).astype(x.dtype)
    return x, new_states


def setup_inputs(seed: int = 0) -> dict:
    key = jax.random.key(seed)
    keys = iter(jax.random.split(key, 40))

    def nrm(shape, scale):
        return scale * jax.random.normal(next(keys), shape, jnp.float32)

    return {
        'x_prompt': nrm((BATCH, SEQ, D_MODEL), 1.0),
        'x_sample': nrm((DEC_BATCH, DEC_SEQ, D_MODEL), 1.0),
        'state_hgrn': nrm((DEPTH, DEC_BATCH, HG_H, HG_DK, HG_DV), 0.5),
        'state_gla': nrm((DEPTH, DEC_BATCH, GLA_H, GLA_DK, GLA_DV), 0.5),
        'state_rwkv': nrm((DEPTH, DEC_BATCH, RW_H, RW_N, RW_N), 0.5),
        'state_rwkv_shift': nrm((DEPTH, DEC_BATCH, RW_COLS), 1.0),
        'state_ret': nrm((DEPTH, DEC_BATCH, RET_H, RET_DK, RET_DV), 0.5),
        'attn_norm_w': 1.0 + nrm((DEPTH, D_MODEL), 0.1),
        'w_in': nrm((DEPTH, D_MODEL, N_IN), D_MODEL ** -0.5),
        'hg_lb_logits': nrm((DEPTH, HG_H * HG_DK), 0.5),
        'hg_norm_w': 1.0 + nrm((DEPTH, HG_DV), 0.1),
        'gla_wa2': nrm((DEPTH, GLA_RANK, GLA_H * GLA_DK), GLA_RANK ** -0.5),
        'gla_ba': nrm((DEPTH, GLA_H * GLA_DK), 0.1),
        'gla_norm_w': 1.0 + nrm((DEPTH, GLA_DV), 0.1),
        'rw_mu': jax.random.uniform(next(keys), (DEPTH, RW_COLS), jnp.float32),
        'rw_w0': nrm((DEPTH, RW_W), 0.5),
        'rw_w2': nrm((DEPTH, RW_DECAY_RANK, RW_W), 0.1 * RW_DECAY_RANK ** -0.5),
        'rw_a0': nrm((DEPTH, RW_W), 0.1),
        'rw_a2': nrm((DEPTH, RW_A_RANK, RW_W), 0.1 * RW_A_RANK ** -0.5),
        'rw_g2': nrm((DEPTH, RW_G_RANK, RW_W), RW_G_RANK ** -0.5),
        'rw_kk': 0.85 + nrm((DEPTH, RW_W), 0.1),
        'rw_ka': 1.0 + nrm((DEPTH, RW_W), 0.1),
        'rw_rk': nrm((DEPTH, RW_W), 0.1),
        'rw_ln_w': 1.0 + nrm((DEPTH, RW_W), 0.1),
        'rw_ln_b': nrm((DEPTH, RW_W), 0.01),
        'w_branch': nrm((DEPTH, N_BRANCH, BRANCH_W, D_MODEL), BRANCH_W ** -0.5),
        'w_out': nrm((DEPTH, D_MODEL, D_MODEL), D_MODEL ** -0.5),
        'ffn_norm_w': 1.0 + nrm((DEPTH, D_MODEL), 0.1),
        'w_ffn_in': nrm((DEPTH, D_MODEL, 2 * D_FF), D_MODEL ** -0.5),
        'w_ffn_out': nrm((DEPTH, D_FF, D_MODEL), D_FF ** -0.5),
        'final_norm_w': 1.0 + nrm((D_MODEL,), 0.1),
    }


def reference(x_prompt, x_sample, state_hgrn, state_gla, state_rwkv, state_rwkv_shift, state_ret,
              attn_norm_w, w_in, hg_lb_logits, hg_norm_w, gla_wa2, gla_ba, gla_norm_w,
              rw_mu, rw_w0, rw_w2, rw_a0, rw_a2, rw_g2, rw_kk, rw_ka, rw_rk, rw_ln_w, rw_ln_b,
              w_branch, w_out, ffn_norm_w, w_ffn_in, w_ffn_out, final_norm_w):
    f32 = jnp.float32
    lb_p = jax.nn.softmax(hg_lb_logits.astype(f32), axis=0)
    lower_bounds = jnp.cumsum(lb_p, axis=0) - lb_p[0:1]

    Bp, Lp, _ = x_prompt.shape
    Ls = x_sample.shape[1]
    pos_p = jnp.arange(Lp, dtype=f32)
    pos_s = float(PAST_LEN) + jnp.arange(Ls, dtype=f32)
    dt = x_prompt.dtype
    zero_states = (jnp.zeros((Bp, HG_H, HG_DK, HG_DV), dt),
                   jnp.zeros((Bp, GLA_H, GLA_DK, GLA_DV), dt),
                   jnp.zeros((Bp, RW_H, RW_N, RW_N), dt),
                   jnp.zeros((Bp, RW_COLS), dt),
                   jnp.zeros((Bp, RET_H, RET_DK, RET_DV), dt))

    xp, xs = x_prompt, x_sample
    p_states, s_states = [], []
    for l in range(DEPTH):
        lw = {
            'attn_norm': attn_norm_w[l], 'w_in': w_in[l],
            'hg_lb': lower_bounds[l], 'hg_norm_w': hg_norm_w[l],
            'gla_wa2': gla_wa2[l], 'gla_ba': gla_ba[l], 'gla_norm_w': gla_norm_w[l],
            'rw_mu': rw_mu[l], 'rw_w0': rw_w0[l], 'rw_w2': rw_w2[l], 'rw_a0': rw_a0[l],
            'rw_a2': rw_a2[l], 'rw_g2': rw_g2[l], 'rw_kk': rw_kk[l], 'rw_ka': rw_ka[l],
            'rw_rk': rw_rk[l], 'rw_ln_w': rw_ln_w[l], 'rw_ln_b': rw_ln_b[l],
            'w_branch': w_branch[l], 'w_out': w_out[l],
            'ffn_norm': ffn_norm_w[l], 'w_ffn_in': w_ffn_in[l], 'w_ffn_out': w_ffn_out[l],
        }
        xp, sp = decoder_layer(xp, pos_p, zero_states, lw)
        xs, ss = decoder_layer(xs, pos_s, (state_hgrn[l], state_gla[l], state_rwkv[l],
                                           state_rwkv_shift[l], state_ret[l]), lw)
        p_states.append(sp)
        s_states.append(ss)

    y_prompt = _rmsnorm(xp, final_norm_w)
    y_sample = _rmsnorm(xs, final_norm_w)
    prompt_hgrn = jnp.stack([s[0] for s in p_states])
    prompt_gla = jnp.stack([s[1] for s in p_states])
    prompt_rwkv = jnp.stack([s[2] for s in p_states])
    prompt_rwkv_shift = jnp.stack([s[3] for s in p_states])
    prompt_ret = jnp.stack([s[4] for s in p_states])
    sample_hgrn = jnp.stack([s[0] for s in s_states])
    sample_gla = jnp.stack([s[1] for s in s_states])
    sample_rwkv = jnp.stack([s[2] for s in s_states])
    sample_rwkv_shift = jnp.stack([s[3] for s in s_states])
    sample_ret = jnp.stack([s[4] for s in s_states])
    return (y_prompt, y_sample,
            prompt_hgrn, prompt_gla, prompt_rwkv, prompt_rwkv_shift, prompt_ret,
            sample_hgrn, sample_gla, sample_rwkv, sample_rwkv_shift, sample_ret)
```

```python
import functools
import math

import jax
import jax.numpy as jnp
import numpy as np
from jax import lax
from jax.experimental import pallas as pl
from jax.experimental.pallas import tpu as pltpu

F32 = jnp.float32
BF16 = jnp.bfloat16

D_MODEL = 1024
N_HEAD = 4
HEAD_V = 64
BRANCH_W = N_HEAD * HEAD_V
N_BRANCH = 4
GLA_DK = 32
GLA_KW = N_HEAD * GLA_DK
GLA_RANK = 16
GLA_TAU = 16.0
RW_COLS = 1024
NORM_EPS = 1e-6
RW_LN_EPS = 64e-5
MIN_FORGET = 1e-30
ROPE_BASE = 10000.0
D_FF = 2816
N_MIX = 4480
N_GATE = N_BRANCH * D_MODEL

VMEM_LIMIT_BYTES = 56 * 1024 * 1024
LANES = 128

C_HG_Q, C_HG_F, C_HG_I, C_HG_G = 0, 256, 512, 768
C_GL_Q, C_GL_K, C_GL_V, C_GL_A, C_GL_G = 1024, 1152, 1280, 1536, 1664
C_RW = 1920
C_RT_Q, C_RT_K, C_RT_V, C_RT_G, C_RT_QS, C_RT_KS = 2944, 3200, 3456, 3712, 3968, 4224

P_LB, P_HG_NW, P_GL_NW, P_W0, P_A0, P_KK, P_KA, P_RK, P_LNW, P_LNB, P_RT_LG = range(11)

_NN = (((1,), (0,)), ((), ()))
_NT = (((1,), (1,)), ((), ()))
_TN = (((0,), (0,)), ((), ()))


def _dg(a, b, dn):
    return lax.dot_general(a, b, dn, preferred_element_type=F32)


def _split(x):
    hi = x.astype(BF16)
    lo = (x - hi.astype(F32)).astype(BF16)
    return hi, lo


def _dot1(a, b, dn=_NN):
    return _dg(a.astype(BF16), b.astype(BF16), dn)


def _dot3(a, b, dn=_NN):
    ah, al = _split(a)
    bh, bl = _split(b)
    return _dg(ah, bh, dn) + (_dg(ah, bl, dn) + _dg(al, bh, dn))


def _dotl2(a, b_exact, dn=_NN):
    ah, al = _split(a)
    return _dg(ah, b_exact, dn) + _dg(al, b_exact, dn)


def _dotr3(a_exact, b, dn=_NN):
    b1 = b.astype(BF16)
    r = b - b1.astype(F32)
    b2 = r.astype(BF16)
    b3 = (r - b2.astype(F32)).astype(BF16)
    return _dg(a_exact, b1, dn) + (_dg(a_exact, b2, dn) + _dg(a_exact, b3, dn))


def _sigmoid(x):
    return jax.nn.sigmoid(x)


def _silu(x):
    return x * jax.nn.sigmoid(x)


def _rms(x, w):
    return x * lax.rsqrt(jnp.mean(x * x, axis=-1, keepdims=True) + NORM_EPS) * w


def _inproj_kernel(x_ref, nw_ref, w_ref, o_ref):
    h = _rms(x_ref[...], nw_ref[...])
    o_ref[...] = jnp.dot(h.astype(BF16), w_ref[...], preferred_element_type=F32)


def _merge_kernel(x_ref, o_ref, nw_ref, wg_ref, wb_ref, wo_ref, out_ref):
    x = x_ref[...]
    h = _rms(x, nw_ref[...]).astype(BF16)
    merged = None
    for b in range(N_BRANCH):
        gl = jnp.dot(h, wg_ref[:, b * D_MODEL:(b + 1) * D_MODEL], preferred_element_type=F32)
        ob = o_ref[:, b * BRANCH_W:(b + 1) * BRANCH_W].astype(BF16)
        up = jnp.dot(ob, wb_ref[b], preferred_element_type=F32)
        t = _sigmoid(gl) * up
        merged = t if merged is None else merged + t
    y = jnp.dot(merged.astype(BF16), wo_ref[...], preferred_element_type=F32)
    out_ref[...] = x + y


FFN_CHUNK = 1408


def _ffn_kernel(x_ref, nw_ref, wi_ref, wo_ref, fw_ref, out_ref, *, final):
    x = x_ref[...]
    h = _rms(x, nw_ref[...]).astype(BF16)
    acc = None
    for j in range(D_FF // FFN_CHUNK):
        lo = j * FFN_CHUNK
        g = jnp.dot(h, wi_ref[:, lo:lo + FFN_CHUNK], preferred_element_type=F32)
        u = jnp.dot(h, wi_ref[:, D_FF + lo:D_FF + lo + FFN_CHUNK], preferred_element_type=F32)
        a = (_silu(g) * u).astype(BF16)
        t = jnp.dot(a, wo_ref[lo:lo + FFN_CHUNK, :], preferred_element_type=F32)
        acc = t if acc is None else acc + t
    x2 = x + acc
    if final:
        x2 = _rms(x2, fw_ref[...])
    out_ref[...] = x2


def _dense_params():
    return pltpu.CompilerParams(dimension_semantics=("arbitrary",), vmem_limit_bytes=VMEM_LIMIT_BYTES)


def _const_spec(shape):
    nd = len(shape)
    return pl.BlockSpec(shape, lambda i: (0,) * nd)


def _token_tile(t):
    return 256 if t % 256 == 0 else t


def _inproj(x, nw, wmix):
    t = x.shape[0]
    tm = _token_tile(t)
    return pl.pallas_call(
        _inproj_kernel,
        grid=(t // tm,),
        in_specs=[pl.BlockSpec((tm, D_MODEL), lambda i: (i, 0)),
                  _const_spec((1, D_MODEL)), _const_spec((D_MODEL, N_MIX))],
        out_specs=pl.BlockSpec((tm, N_MIX), lambda i: (i, 0)),
        out_shape=jax.ShapeDtypeStruct((t, N_MIX), F32),
        compiler_params=_dense_params(),
        name="inproj",
    )(x, nw, wmix)


def _merge(x, o, nw, wg, wb, wo):
    t = x.shape[0]
    tm = _token_tile(t)
    return pl.pallas_call(
        _merge_kernel,
        grid=(t // tm,),
        in_specs=[pl.BlockSpec((tm, D_MODEL), lambda i: (i, 0)),
                  pl.BlockSpec((tm, N_BRANCH * BRANCH_W), lambda i: (i, 0)),
                  _const_spec((1, D_MODEL)), _const_spec((D_MODEL, N_GATE)),
                  _const_spec((N_BRANCH, BRANCH_W, D_MODEL)), _const_spec((D_MODEL, D_MODEL))],
        out_specs=pl.BlockSpec((tm, D_MODEL), lambda i: (i, 0)),
        out_shape=jax.ShapeDtypeStruct((t, D_MODEL), F32),
        compiler_params=_dense_params(),
        name="merge",
    )(x, o, nw, wg, wb, wo)


def _ffn(x, nw, wi, wo, fw, final):
    t = x.shape[0]
    tm = _token_tile(t)
    return pl.pallas_call(
        functools.partial(_ffn_kernel, final=final),
        grid=(t // tm,),
        in_specs=[pl.BlockSpec((tm, D_MODEL), lambda i: (i, 0)),
                  _const_spec((1, D_MODEL)), _const_spec((D_MODEL, 2 * D_FF)),
                  _const_spec((D_FF, D_MODEL)), _const_spec((1, D_MODEL))],
        out_specs=pl.BlockSpec((tm, D_MODEL), lambda i: (i, 0)),
        out_shape=jax.ShapeDtypeStruct((t, D_MODEL), F32),
        compiler_params=_dense_params(),
        name="ffn",
    )(x, nw, wi, wo, fw)


def _gla_prep(q, k, v, g, tri, onesb, ind, c, qd_ref, kd_ref, acc_ref, dec_ref):
    tl = q.shape[0]
    b = _dotr3(tri, g)
    bl = _dotr3(onesb, g)
    qd_ref[...] = q * jnp.exp(b)
    kd_ref[...] = k * jnp.exp(bl - b)
    dec_ref[...] = jnp.exp(bl)
    rowmod = lax.broadcasted_iota(jnp.int32, q.shape, 0) & (c - 1)
    acc = _dotl2(q * k, ind) * v
    for d in range(1, c):
        ks = pltpu.roll(k, d, 0)
        bs = pltpu.roll(b, d, 0)
        vs = pltpu.roll(v, d, 0)
        p = jnp.where(rowmod >= d, q * ks * jnp.exp(b - bs), 0.0)
        acc = acc + _dotl2(p, ind) * vs
    acc_ref[...] = acc


def _gla_step(st_ref, qd, kd, v, acc, dec_row, mask):
    st = st_ref[...]
    o = acc + _dot1(qd, st, _NT)
    st_ref[...] = st * dec_row + _dot1(v, kd, _TN) * mask
    return o


def _head_rms_gate(o, ind64, nw, gate):
    ms = _dotl2(o * o, ind64) * (1.0 / HEAD_V)
    return o * lax.rsqrt(ms + NORM_EPS) * nw * _silu(gate)


def _rwkv_block(kt, ah, kh, rt, ap, kp, v, dec_row, st_ref, hmasks, strict, incl, bd4, eye4, m256, c):
    def stack(x):
        return jnp.concatenate([x * hmasks[h] for h in range(N_HEAD)], axis=0)

    ahs = stack(ah)
    khs = stack(kh)
    kr = jnp.concatenate([kt, rt], axis=0)
    la = _dot3(kr, ahs, _NT)
    lk = _dot3(kr, khs, _NT)
    l_a = jnp.where(strict, la[:c], 0.0)
    m_a = jnp.where(incl, la[c:], 0.0)
    l_k = jnp.where(strict, lk[:c], 0.0)
    m_k = jnp.where(incl, lk[c:], 0.0)
    n = -(jnp.concatenate([l_a] * N_HEAD, axis=0) * bd4)
    pm = eye4 + n
    nk = n
    for _ in range(int(math.log2(c)) - 1):
        nk = _dot3(nk, nk)
        pm = pm + _dot3(pm, nk)
    ts = pm[0:c]
    for h in range(1, N_HEAD):
        ts = ts + pm[h * c:(h + 1) * c]
    vs = stack(v)
    lkv = _dot3(l_k, vs)
    mkv = _dot3(m_k, vs)
    kbar = _dot3(ts, stack(kt))
    u0 = -_dot3(ts, stack(lkv))
    st = st_ref[...]
    xr = _dot3(jnp.concatenate([kbar, rt], axis=0), st, _NT)
    u = u0 - xr[:c]
    o = xr[c:] + _dot3(m_a, stack(u)) + mkv
    upd = _dot3(jnp.concatenate([u, v], axis=0), jnp.concatenate([ap, kp], axis=0), _TN)
    st_ref[...] = st * dec_row + upd * m256
    return o


def _mixer_kernel(x_ref, shg_ref, sgl_ref, srw_ref, srt_ref, shift_ref, cos_ref, sin_ref,
                  p256_ref, glba_ref, mu_ref, wa2_ref, w2_ref, a2_ref, g2_ref,
                  tri_ref, onesb_ref, ind64_ref, indgl_ref, m256_ref, mgl_ref,
                  o_ref, nhg_ref, ngl_ref, nrw_ref, nrt_ref,
                  carry, hg_qd, hg_kd, hg_acc, hg_dec, gl_qd, gl_kd, gl_acc, gl_dec,
                  rt_qd, rt_kd, rt_acc, rt_dec,
                  rw_kt, rw_ah, rw_kh, rw_rt, rw_ap, rw_kp, rw_v, rw_dec, osc, *, c, tl):
    j = pl.program_id(1)

    @pl.when(j == 0)
    def _():
        nhg_ref[...] = shg_ref[...]
        ngl_ref[...] = sgl_ref[...]
        nrw_ref[...] = srw_ref[...]
        nrt_ref[...] = srt_ref[...]
        carry[...] = shift_ref[...]

    def prm(i):
        return p256_ref[i:i + 1, :]

    tri = tri_ref[...]
    onesb = onesb_ref[...]
    ind64 = ind64_ref[...]
    indgl = indgl_ref[...]

    lb = prm(P_LB)
    forget = lb + (1.0 - lb) * _sigmoid(x_ref[:, C_HG_F:C_HG_F + 256])
    _gla_prep(_silu(x_ref[:, C_HG_Q:C_HG_Q + 256]), 1.0 - forget, x_ref[:, C_HG_I:C_HG_I + 256],
              jnp.log(jnp.maximum(forget, MIN_FORGET)), tri, onesb, ind64, c,
              hg_qd, hg_kd, hg_acc, hg_dec)

    za = _dot1(x_ref[:, C_GL_A:C_GL_A + LANES], wa2_ref[...]) + glba_ref[0:1, :]
    log_a = (jnp.minimum(za, 0.0) - jnp.log1p(jnp.exp(-jnp.abs(za)))) / GLA_TAU
    _gla_prep(x_ref[:, C_GL_Q:C_GL_Q + GLA_KW] * GLA_DK ** -0.5, x_ref[:, C_GL_K:C_GL_K + GLA_KW],
              x_ref[:, C_GL_V:C_GL_V + 256], log_a, tri, onesb, indgl, c,
              gl_qd, gl_kd, gl_acc, gl_dec)

    cosf = cos_ref[...]
    sins = sin_ref[...]
    q_r = x_ref[:, C_RT_Q:C_RT_Q + 256] * cosf + x_ref[:, C_RT_QS:C_RT_QS + 256] * sins
    k_r = (x_ref[:, C_RT_K:C_RT_K + 256] * cosf + x_ref[:, C_RT_KS:C_RT_KS + 256] * sins) * HEAD_V ** -0.5
    _gla_prep(q_r, k_r, x_ref[:, C_RT_V:C_RT_V + 256],
              jnp.broadcast_to(prm(P_RT_LG), (tl, 256)), tri, onesb, ind64, c,
              rt_qd, rt_kd, rt_acc, rt_dec)

    rw = x_ref[:, C_RW:C_RW + RW_COLS]
    row = lax.broadcasted_iota(jnp.int32, (tl, RW_COLS), 0)
    prev = jnp.where(row == 0, jnp.broadcast_to(carry[...], (tl, RW_COLS)), pltpu.roll(rw, 1, 0))
    carry[...] = rw[tl - 1:tl, :]
    mix = rw + (prev - rw) * mu_ref[...]
    r = mix[:, 0:256]
    k0 = mix[:, 256:512]
    v = mix[:, 512:768]
    wa = mix[:, 768:896]
    gd = mix[:, 896:1024]
    lw = -math.exp(-0.5) * _sigmoid(prm(P_W0) + _dot1(jnp.tanh(wa), w2_ref[...]))
    a = _sigmoid(prm(P_A0) + _dot1(wa, a2_ref[...]))
    g = _dot1(_sigmoid(gd), g2_ref[...])
    kk = k0 * prm(P_KK)
    kk = kk / jnp.maximum(jnp.sqrt(_dotl2(kk * kk, ind64)), 1e-12)
    k = k0 * (1.0 + (a - 1.0) * prm(P_KA))
    bw = _dotr3(tri, lw)
    bc = _dotr3(onesb, lw)
    alpha = a * kk
    e_neg = jnp.exp(-bw)
    e_rem = jnp.exp(bc - bw)
    rw_kt[...] = kk * jnp.exp(bw - lw)
    rw_ah[...] = alpha * e_neg
    rw_kh[...] = k * e_neg
    rw_rt[...] = r * jnp.exp(bw)
    rw_ap[...] = alpha * e_rem
    rw_kp[...] = k * e_rem
    rw_v[...] = v
    rw_dec[...] = jnp.exp(bc)
    bonus = _dotl2(r * k * prm(P_RK), ind64) * v

    lane = lax.broadcasted_iota(jnp.int32, (1, 256), 1)
    hmasks = [((lane >> 6) == h).astype(F32) for h in range(N_HEAD)]
    r4 = lax.broadcasted_iota(jnp.int32, (c, N_HEAD * c), 0)
    s4 = lax.broadcasted_iota(jnp.int32, (c, N_HEAD * c), 1) & (c - 1)
    strict = s4 < r4
    incl = s4 <= r4
    rr = lax.broadcasted_iota(jnp.int32, (N_HEAD * c, N_HEAD * c), 0)
    cc = lax.broadcasted_iota(jnp.int32, (N_HEAD * c, N_HEAD * c), 1)
    sh = int(math.log2(c))
    bd4 = ((rr >> sh) == (cc >> sh)).astype(F32)
    eye4 = (rr == cc).astype(F32)
    m256 = m256_ref[...]
    mgl = mgl_ref[...]

    def body(i, carry_):
        r0 = pl.multiple_of(i * c, c)
        rows = pl.ds(r0, c)
        o_hg = _gla_step(nhg_ref, hg_qd[rows, :], hg_kd[rows, :], x_ref[rows, C_HG_I:C_HG_I + 256],
                         hg_acc[rows, :], hg_dec[pl.ds(r0, 1), :], m256)
        o_gl = _gla_step(ngl_ref, gl_qd[rows, :], gl_kd[rows, :], x_ref[rows, C_GL_V:C_GL_V + 256],
                         gl_acc[rows, :], gl_dec[pl.ds(r0, 1), :], mgl)
        o_rt = _gla_step(nrt_ref, rt_qd[rows, :], rt_kd[rows, :], x_ref[rows, C_RT_V:C_RT_V + 256],
                         rt_acc[rows, :], rt_dec[pl.ds(r0, 1), :], m256)
        o_rw = _rwkv_block(rw_kt[rows, :], rw_ah[rows, :], rw_kh[rows, :], rw_rt[rows, :],
                           rw_ap[rows, :], rw_kp[rows, :], rw_v[rows, :], rw_dec[pl.ds(r0, 1), :],
                           nrw_ref, hmasks, strict, incl, bd4, eye4, m256, c)
        osc[rows, 0:256] = o_hg
        osc[rows, 256:512] = o_gl
        osc[rows, 512:768] = o_rw
        osc[rows, 768:1024] = o_rt
        return carry_

    lax.fori_loop(0, tl // c, body, 0)

    o_ref[:, 0:256] = _head_rms_gate(osc[:, 0:256], ind64, prm(P_HG_NW), x_ref[:, C_HG_G:C_HG_G + 256])
    o_ref[:, 256:512] = _head_rms_gate(osc[:, 256:512], ind64, prm(P_GL_NW), x_ref[:, C_GL_G:C_GL_G + 256])
    orw = osc[:, 512:768]
    mean = _dotl2(orw, ind64) * (1.0 / HEAD_V)
    xc = orw - mean
    var = _dotl2(xc * xc, ind64) * (1.0 / HEAD_V)
    o_ref[:, 512:768] = (xc * lax.rsqrt(var + RW_LN_EPS) * prm(P_LNW) + prm(P_LNB) + bonus) * g
    ort = osc[:, 768:1024]
    ms = _dotl2(ort * ort, ind64) * (1.0 / HEAD_V)
    o_ref[:, 768:1024] = ort * lax.rsqrt(ms + NORM_EPS) * _silu(x_ref[:, C_RT_G:C_RT_G + 256])


def _mixer_constants(tl, c):
    t = np.arange(tl)
    same = (t[:, None] // c) == (t[None, :] // c)
    tri = (same & (t[None, :] <= t[:, None])).astype(np.float32)
    onesb = same.astype(np.float32)
    kv = np.arange(256)
    ind64 = ((kv[:, None] // HEAD_V) == (kv[None, :] // HEAD_V)).astype(np.float32)
    kg = np.arange(GLA_KW)
    indgl = ((kg[:, None] // GLA_DK) == (kv[None, :] // HEAD_V)).astype(np.float32)
    mgl = indgl.T
    return (jnp.asarray(tri, BF16), jnp.asarray(onesb, BF16), jnp.asarray(ind64, BF16),
            jnp.asarray(indgl, BF16), jnp.asarray(ind64, F32), jnp.asarray(mgl, F32))


def _mixer(proj, states, shift, cosf, sins, lp, c, tl):
    nb, length, _ = proj.shape
    nj = length // tl
    consts = _mixer_constants(tl, c)

    def seq_spec(w):
        return pl.BlockSpec((None, tl, w), lambda b, j: (b, j, 0))

    def state_spec(w):
        return pl.BlockSpec((None, 256, w), lambda b, j: (b, 0, 0))

    def const_spec(a):
        nd = a.ndim
        return pl.BlockSpec(a.shape, lambda b, j: (0,) * nd)

    pos_spec = pl.BlockSpec((tl, 256), lambda b, j: (j, 0))
    params = (lp['p256'], lp['glba'], lp['mu'], lp['wa2'], lp['w2'], lp['a2'], lp['g2'])
    in_specs = ([seq_spec(N_MIX), state_spec(256), state_spec(GLA_KW), state_spec(256), state_spec(256),
                 pl.BlockSpec((None, 1, RW_COLS), lambda b, j: (b, 0, 0)), pos_spec, pos_spec]
                + [const_spec(a) for a in params] + [const_spec(a) for a in consts])
    out_specs = [seq_spec(N_BRANCH * BRANCH_W), state_spec(256), state_spec(GLA_KW), state_spec(256),
                 state_spec(256)]
    out_shape = [jax.ShapeDtypeStruct((nb, length, N_BRANCH * BRANCH_W), F32),
                 jax.ShapeDtypeStruct((nb, 256, 256), F32), jax.ShapeDtypeStruct((nb, 256, GLA_KW), F32),
                 jax.ShapeDtypeStruct((nb, 256, 256), F32), jax.ShapeDtypeStruct((nb, 256, 256), F32)]

    def vm(w):
        return pltpu.VMEM((tl, w), F32)

    scratch = ([pltpu.VMEM((1, RW_COLS), F32)]
               + [vm(256), vm(256), vm(256), vm(256)]
               + [vm(GLA_KW), vm(GLA_KW), vm(256), vm(GLA_KW)]
               + [vm(256), vm(256), vm(256), vm(256)]
               + [vm(256)] * 8
               + [vm(N_BRANCH * BRANCH_W)])
    return pl.pallas_call(
        functools.partial(_mixer_kernel, c=c, tl=tl),
        grid=(nb, nj),
        in_specs=in_specs,
        out_specs=out_specs,
        out_shape=out_shape,
        scratch_shapes=scratch,
        compiler_params=pltpu.CompilerParams(dimension_semantics=("arbitrary", "arbitrary"),
                                             vmem_limit_bytes=VMEM_LIMIT_BYTES),
        name="mixer",
    )(proj, *states, shift, cosf, sins, *params, *consts)


_RW_PERM = np.concatenate([np.arange(0, 256), np.arange(320, 576), np.arange(576, 832),
                           np.arange(256, 320), np.arange(832, 896), np.arange(896, 1024)])
_RW_INV = np.argsort(_RW_PERM)
_HALF_SWAP = (np.arange(256) // HEAD_V) * HEAD_V + (np.arange(256) % HEAD_V + HEAD_V // 2) % HEAD_V


def _tile_heads(v):
    return jnp.tile(v, N_HEAD)


def _layer_params(l, lower_bounds, w_in, hg_norm_w, gla_wa2, gla_ba, gla_norm_w, rw_mu, rw_w0, rw_w2,
                  rw_a0, rw_a2, rw_g2, rw_kk, rw_ka, rw_rk, rw_ln_w, rw_ln_b, w_branch, w_out,
                  w_ffn_in, w_ffn_out):
    w = w_in[l]
    rwc = w[:, 1808:2832][:, _RW_PERM]
    rq = w[:, 2832:3088]
    rk = w[:, 3088:3344]
    wmix = jnp.concatenate([
        w[:, 0:1024],
        w[:, 1024:1536],
        w[:, 1536:1552], jnp.zeros((D_MODEL, LANES - GLA_RANK), F32),
        w[:, 1552:1808],
        rwc,
        rq, rk, w[:, 3344:3856],
        rq[:, _HALF_SWAP], rk[:, _HALF_SWAP],
    ], axis=1).astype(BF16)
    log_gamma = jnp.log1p(-jnp.exp2(-5.0 - jnp.arange(N_HEAD, dtype=F32)))
    rows = [lower_bounds[l], _tile_heads(hg_norm_w[l]), _tile_heads(gla_norm_w[l]), rw_w0[l], rw_a0[l],
            rw_kk[l], rw_ka[l], rw_rk[l], rw_ln_w[l], rw_ln_b[l], jnp.repeat(log_gamma, HEAD_V)]
    p256 = jnp.concatenate([jnp.stack(rows), jnp.zeros((16 - len(rows), 256), F32)], axis=0)
    zeros64 = jnp.zeros((64, 256), F32)
    return {
        'wmix': wmix,
        'wg': w[:, 3856:].astype(BF16),
        'p256': p256,
        'glba': jnp.concatenate([gla_ba[l][None, :], jnp.zeros((7, GLA_KW), F32)], axis=0),
        'mu': rw_mu[l][_RW_PERM][None, :],
        'wa2': jnp.concatenate([gla_wa2[l], jnp.zeros((LANES - GLA_RANK, GLA_KW), F32)], axis=0).astype(BF16),
        'w2': jnp.concatenate([rw_w2[l], zeros64], axis=0).astype(BF16),
        'a2': jnp.concatenate([zeros64, rw_a2[l]], axis=0).astype(BF16),
        'g2': rw_g2[l].astype(BF16),
        'wb': w_branch[l].astype(BF16),
        'wo': w_out[l].astype(BF16),
        'wi': w_ffn_in[l].astype(BF16),
        'wfo': w_ffn_out[l].astype(BF16),
    }


def _rotary_tables(pos):
    half = HEAD_V // 2
    inv = ROPE_BASE ** (-jnp.arange(half, dtype=F32) / half)
    ang = pos[:, None] * inv[None, :]
    cos, sin = jnp.cos(ang), jnp.sin(ang)
    cosf = jnp.tile(jnp.concatenate([cos, cos], axis=1), (1, N_HEAD))
    sins = jnp.tile(jnp.concatenate([-sin, sin], axis=1), (1, N_HEAD))
    return cosf, sins


def _state_to_bd(s):
    nb, h, dk, dv = s.shape
    st = jnp.swapaxes(s, 2, 3)
    bd = jnp.einsum('bhvk,hg->bhvgk', st, jnp.eye(h, dtype=s.dtype))
    return bd.reshape(nb, h * dv, h * dk)


def _bd_to_state(bd, dk):
    nb = bd.shape[0]
    r = bd.reshape(nb, N_HEAD, HEAD_V, N_HEAD, dk)
    return jnp.einsum('bhvhk->bhkv', r)


def _group_layer(x, states, shift, tables, lp, attn_nw, ffn_nw, final_nw, final, c, tl):
    nb, length, _ = x.shape
    xf = x.reshape(nb * length, D_MODEL)
    proj = _inproj(xf, attn_nw, lp['wmix']).reshape(nb, length, N_MIX)
    o, n_hg, n_gl, n_rw, n_rt = _mixer(proj, states, shift[:, None, _RW_PERM], tables[0], tables[1],
                                       lp, c, tl)
    x1 = _merge(xf, o.reshape(nb * length, N_BRANCH * BRANCH_W), attn_nw, lp['wg'], lp['wb'], lp['wo'])
    x2 = _ffn(x1, ffn_nw, lp['wi'], lp['wfo'], final_nw, final)
    new_shift = proj[:, length - 1, C_RW:C_RW + RW_COLS][:, _RW_INV]
    new_states = (_bd_to_state(n_hg, HEAD_V), _bd_to_state(n_gl, GLA_DK), _bd_to_state(n_rw, HEAD_V),
                  new_shift, _bd_to_state(n_rt, HEAD_V))
    return x2.reshape(nb, length, D_MODEL), new_states


def kernel(x_prompt, x_sample, state_hgrn, state_gla, state_rwkv, state_rwkv_shift, state_ret,
           attn_norm_w, w_in, hg_lb_logits, hg_norm_w, gla_wa2, gla_ba, gla_norm_w,
           rw_mu, rw_w0, rw_w2, rw_a0, rw_a2, rw_g2, rw_kk, rw_ka, rw_rk, rw_ln_w, rw_ln_b,
           w_branch, w_out, ffn_norm_w, w_ffn_in, w_ffn_out, final_norm_w):
    depth = w_in.shape[0]
    lb_p = jax.nn.softmax(hg_lb_logits.astype(F32), axis=0)
    lower_bounds = jnp.cumsum(lb_p, axis=0) - lb_p[0:1]

    bp, lp_len, _ = x_prompt.shape
    bs, ls_len, _ = x_sample.shape
    past_len = 16384
    tab_p = _rotary_tables(jnp.arange(lp_len, dtype=F32))
    tab_s = _rotary_tables(float(past_len) + jnp.arange(ls_len, dtype=F32))
    zero_states = (jnp.zeros((bp, 256, 256), F32), jnp.zeros((bp, 256, GLA_KW), F32),
                   jnp.zeros((bp, 256, 256), F32), jnp.zeros((bp, 256, 256), F32))
    zero_shift = jnp.zeros((bp, RW_COLS), F32)
    final_nw = final_norm_w[None, :]

    xp, xs = x_prompt, x_sample
    p_states, s_states = [], []
    for l in range(depth):
        lp = _layer_params(l, lower_bounds, w_in, hg_norm_w, gla_wa2, gla_ba, gla_norm_w, rw_mu, rw_w0,
                           rw_w2, rw_a0, rw_a2, rw_g2, rw_kk, rw_ka, rw_rk, rw_ln_w, rw_ln_b, w_branch,
                           w_out, w_ffn_in, w_ffn_out)
        final = l == depth - 1
        anw, fnw = attn_norm_w[l][None, :], ffn_norm_w[l][None, :]
        xp, sp = _group_layer(xp, zero_states, zero_shift, tab_p, lp, anw, fnw, final_nw, final,
                              c=16, tl=128)
        s_in = (_state_to_bd(state_hgrn[l]), _state_to_bd(state_gla[l]), _state_to_bd(state_rwkv[l]),
                _state_to_bd(state_ret[l]))
        xs, ss = _group_layer(xs, s_in, state_rwkv_shift[l], tab_s, lp, anw, fnw, final_nw, final,
                              c=ls_len, tl=ls_len)
        p_states.append(sp)
        s_states.append(ss)

    def stk(states, i):
        return jnp.stack([s[i] for s in states])

    return (xp, xs,
            stk(p_states, 0), stk(p_states, 1), stk(p_states, 2), stk(p_states, 3), stk(p_states, 4),
            stk(s_states, 0), stk(s_states, 1), stk(s_states, 2), stk(s_states, 3), stk(s_states, 4))
```

```python
import functools
import math

import jax
import jax.numpy as jnp
import numpy as np
from jax import lax
from jax.experimental import pallas as pl
from jax.experimental.pallas import tpu as pltpu

F32 = jnp.float32
BF16 = jnp.bfloat16

D_MODEL = 1024
N_HEAD = 4
HEAD_V = 64
BRANCH_W = N_HEAD * HEAD_V
N_BRANCH = 4
GLA_DK = 32
GLA_KW = N_HEAD * GLA_DK
GLA_RANK = 16
GLA_TAU = 16.0
RW_COLS = 1024
NORM_EPS = 1e-6
RW_LN_EPS = 64e-5
MIN_FORGET = 1e-30
ROPE_BASE = 10000.0
LOG2E = 1.4426950408889634
D_FF = 2816
N_MIX = 4480
N_GATE = N_BRANCH * D_MODEL

VMEM_LIMIT_BYTES = 56 * 1024 * 1024
LANES = 128

C_HG_Q, C_HG_F, C_HG_I, C_HG_G = 0, 256, 512, 768
C_GL_Q, C_GL_K, C_GL_V, C_GL_A, C_GL_G = 1024, 1152, 1280, 1536, 1664
C_RW = 1920
C_RT_Q, C_RT_K, C_RT_V, C_RT_G, C_RT_QS, C_RT_KS = 2944, 3200, 3456, 3712, 3968, 4224

P_LB, P_HG_NW, P_GL_NW, P_W0, P_A0, P_KK, P_KA, P_RK, P_LNW, P_LNB, P_RT_LG = range(11)

_NN = (((1,), (0,)), ((), ()))
_NT = (((1,), (1,)), ((), ()))
_TN = (((0,), (0,)), ((), ()))


def _dg(a, b, dn):
    return lax.dot_general(a, b, dn, preferred_element_type=F32)


def _split(x):
    hi = x.astype(BF16)
    lo = (x - hi.astype(F32)).astype(BF16)
    return hi, lo


def _dot1(a, b, dn=_NN):
    return _dg(a.astype(BF16), b.astype(BF16), dn)


def _dot3s(ah, al, bh, bl, dn=_NN):
    return _dg(ah, bh, dn) + (_dg(ah, bl, dn) + _dg(al, bh, dn))


def _dot3(a, b, dn=_NN):
    return _dot3s(*_split(a), *_split(b), dn)


def _dotl2(a, b_exact, dn=_NN):
    ah, al = _split(a)
    return _dg(ah, b_exact, dn) + _dg(al, b_exact, dn)


def _dotr3(a_exact, b, dn=_NN):
    b1 = b.astype(BF16)
    r = b - b1.astype(F32)
    b2 = r.astype(BF16)
    b3 = (r - b2.astype(F32)).astype(BF16)
    return _dg(a_exact, b1, dn) + (_dg(a_exact, b2, dn) + _dg(a_exact, b3, dn))


def _sigmoid(x):
    return jax.nn.sigmoid(x)


def _silu(x):
    return x * jax.nn.sigmoid(x)


def _rms(x, w):
    return x * lax.rsqrt(jnp.mean(x * x, axis=-1, keepdims=True) + NORM_EPS) * w


def _inproj_kernel(x_ref, nw_ref, w_ref, o_ref):
    h = _rms(x_ref[...], nw_ref[...])
    o_ref[...] = jnp.dot(h.astype(BF16), w_ref[...], preferred_element_type=F32)


def _merge_kernel(x_ref, o_ref, nw_ref, wg_ref, wb_ref, wo_ref, out_ref):
    x = x_ref[...]
    h = _rms(x, nw_ref[...]).astype(BF16)
    merged = None
    for b in range(N_BRANCH):
        gl = jnp.dot(h, wg_ref[:, b * D_MODEL:(b + 1) * D_MODEL], preferred_element_type=F32)
        ob = o_ref[:, b * BRANCH_W:(b + 1) * BRANCH_W].astype(BF16)
        up = jnp.dot(ob, wb_ref[b], preferred_element_type=F32)
        t = _sigmoid(gl) * up
        merged = t if merged is None else merged + t
    y = jnp.dot(merged.astype(BF16), wo_ref[...], preferred_element_type=F32)
    out_ref[...] = x + y


FFN_CHUNK = 1408


def _ffn_kernel(x_ref, nw_ref, wi_ref, wo_ref, fw_ref, out_ref, *, final):
    x = x_ref[...]
    h = _rms(x, nw_ref[...]).astype(BF16)
    acc = None
    for j in range(D_FF // FFN_CHUNK):
        lo = j * FFN_CHUNK
        g = jnp.dot(h, wi_ref[:, lo:lo + FFN_CHUNK], preferred_element_type=F32)
        u = jnp.dot(h, wi_ref[:, D_FF + lo:D_FF + lo + FFN_CHUNK], preferred_element_type=F32)
        a = (_silu(g) * u).astype(BF16)
        t = jnp.dot(a, wo_ref[lo:lo + FFN_CHUNK, :], preferred_element_type=F32)
        acc = t if acc is None else acc + t
    x2 = x + acc
    if final:
        x2 = _rms(x2, fw_ref[...])
    out_ref[...] = x2


def _dense_params():
    return pltpu.CompilerParams(dimension_semantics=("arbitrary",), vmem_limit_bytes=VMEM_LIMIT_BYTES)


def _const_spec(shape):
    nd = len(shape)
    return pl.BlockSpec(shape, lambda i: (0,) * nd)


def _token_tile(t):
    return 256 if t % 256 == 0 else t


def _inproj(x, nw, wmix):
    t = x.shape[0]
    tm = _token_tile(t)
    return pl.pallas_call(
        _inproj_kernel,
        grid=(t // tm,),
        in_specs=[pl.BlockSpec((tm, D_MODEL), lambda i: (i, 0)),
                  _const_spec((1, D_MODEL)), _const_spec((D_MODEL, N_MIX))],
        out_specs=pl.BlockSpec((tm, N_MIX), lambda i: (i, 0)),
        out_shape=jax.ShapeDtypeStruct((t, N_MIX), F32),
        compiler_params=_dense_params(),
        name="inproj",
    )(x, nw, wmix)


def _merge(x, o, nw, wg, wb, wo):
    t = x.shape[0]
    tm = _token_tile(t)
    return pl.pallas_call(
        _merge_kernel,
        grid=(t // tm,),
        in_specs=[pl.BlockSpec((tm, D_MODEL), lambda i: (i, 0)),
                  pl.BlockSpec((tm, N_BRANCH * BRANCH_W), lambda i: (i, 0)),
                  _const_spec((1, D_MODEL)), _const_spec((D_MODEL, N_GATE)),
                  _const_spec((N_BRANCH, BRANCH_W, D_MODEL)), _const_spec((D_MODEL, D_MODEL))],
        out_specs=pl.BlockSpec((tm, D_MODEL), lambda i: (i, 0)),
        out_shape=jax.ShapeDtypeStruct((t, D_MODEL), F32),
        compiler_params=_dense_params(),
        name="merge",
    )(x, o, nw, wg, wb, wo)


def _ffn(x, nw, wi, wo, fw, final):
    t = x.shape[0]
    tm = _token_tile(t)
    return pl.pallas_call(
        functools.partial(_ffn_kernel, final=final),
        grid=(t // tm,),
        in_specs=[pl.BlockSpec((tm, D_MODEL), lambda i: (i, 0)),
                  _const_spec((1, D_MODEL)), _const_spec((D_MODEL, 2 * D_FF)),
                  _const_spec((D_FF, D_MODEL)), _const_spec((1, D_MODEL))],
        out_specs=pl.BlockSpec((tm, D_MODEL), lambda i: (i, 0)),
        out_shape=jax.ShapeDtypeStruct((t, D_MODEL), F32),
        compiler_params=_dense_params(),
        name="ffn",
    )(x, nw, wi, wo, fw)


def _stack_heads(x, hm):
    return jnp.concatenate([x * hm[h] for h in range(N_HEAD)], axis=0)


def _gla_tile(q, k, v, g, tri, ind, hm_k, hm_v, st_ref, mask, c):
    tl = q.shape[0]
    nblk = tl // c
    b = _dotr3(tri, g)
    b2 = b * LOG2E
    rowmod = lax.broadcasted_iota(jnp.int32, q.shape, 0) & (c - 1)
    acc = _dot1(q * k, ind) * v
    for d in range(1, c):
        ks = pltpu.roll(k, d, 0)
        bs = pltpu.roll(b2, d, 0)
        vs = pltpu.roll(v, d, 0)
        p = jnp.where(rowmod >= d, q * ks * jnp.exp2(b2 - bs), 0.0)
        acc = acc + _dot1(p, ind) * vs
    if nblk > 1:
        scores = []
        for i in range(1, nblk):
            r = b[i * c - 1:i * c, :]
            qi = q[i * c:(i + 1) * c] * jnp.exp(b[i * c:(i + 1) * c] - r)
            ki = k[:i * c] * jnp.exp(r - b[:i * c])
            scores.append(_dot3(_stack_heads(qi, hm_k), ki, _NT))
        parts = [jnp.zeros((c, v.shape[1]), F32)]
        for i in range(1, nblk):
            res = _dot1(scores[i - 1], v[:i * c])
            oi = res[0:c] * hm_v[0]
            for h in range(1, N_HEAD):
                oi = oi + res[h * c:(h + 1) * c] * hm_v[h]
            parts.append(oi)
        acc = acc + jnp.concatenate(parts, axis=0)
    blast = b[tl - 1:tl, :]
    st = st_ref[...]
    o = acc + _dot1(q * jnp.exp(b), st, _NT)
    st_ref[...] = st * jnp.exp(blast) + _dot1(v, k * jnp.exp(blast - b), _TN) * mask
    return o


def _ret_tile(q, k, v, dm, lg, hm, st_ref, mask):
    tl = q.shape[0]
    tau = lax.broadcasted_iota(jnp.int32, q.shape, 0).astype(F32)
    a = _dot3(q, _stack_heads(k, hm), _NT) * dm
    st = st_ref[...]
    o = _dot1(a, _stack_heads(v, hm)) + _dot1(q * jnp.exp((tau + 1.0) * lg), st, _NT)
    st_ref[...] = st * jnp.exp(float(tl) * lg) + _dot1(v, k * jnp.exp((float(tl) - 1.0 - tau) * lg), _TN) * mask
    return o


def _head_rms_gate(o, ind64, nw, gate):
    ms = _dotl2(o * o, ind64) * (1.0 / HEAD_V)
    return o * lax.rsqrt(ms + NORM_EPS) * nw * _silu(gate)


def _rwkv_blocks(kt, ah, kh, rt, ap, kp, v, dec, hmasks, m256, m_scr, ct_scr, c):
    nblk = kt.shape[0] // c
    blocks = range(nblk)
    r4 = lax.broadcasted_iota(jnp.int32, (c, N_HEAD * c), 0)
    s4 = lax.broadcasted_iota(jnp.int32, (c, N_HEAD * c), 1) & (c - 1)
    strict = s4 < r4
    incl = s4 <= r4
    rr = lax.broadcasted_iota(jnp.int32, (N_HEAD * c, N_HEAD * c), 0)
    cc = lax.broadcasted_iota(jnp.int32, (N_HEAD * c, N_HEAD * c), 1)
    sh = int(math.log2(c))
    bd4 = ((rr >> sh) == (cc >> sh)).astype(F32)
    eye4 = (rr == cc).astype(F32)
    eye256 = (lax.broadcasted_iota(jnp.int32, (256, 256), 0) == lax.broadcasted_iota(jnp.int32, (256, 256), 1))

    def blk(x, i):
        return x[i * c:(i + 1) * c]

    def stack(x):
        return _stack_heads(x, hmasks)

    kr = [jnp.concatenate([blk(kt, i), blk(rt, i)], axis=0) for i in blocks]
    la = [_dot3(kr[i], stack(blk(ah, i)), _NT) for i in blocks]
    lk = [_dot3(kr[i], stack(blk(kh, i)), _NT) for i in blocks]
    l_a = [jnp.where(strict, x[:c], 0.0) for x in la]
    m_a = [jnp.where(incl, x[c:], 0.0) for x in la]
    l_k = [jnp.where(strict, x[:c], 0.0) for x in lk]
    m_k = [jnp.where(incl, x[c:], 0.0) for x in lk]
    nk = [-(jnp.concatenate([x] * N_HEAD, axis=0) * bd4) for x in l_a]
    pm = [eye4 + x for x in nk]
    for _ in range(sh - 1):
        nk = [_dot3(x, x) for x in nk]
        pm = [pm[i] + _dot3(pm[i], nk[i]) for i in blocks]
    ts = []
    for x in pm:
        t = x[0:c]
        for h in range(1, N_HEAD):
            t = t + x[h * c:(h + 1) * c]
        ts.append(t)
    vs = [stack(blk(v, i)) for i in blocks]
    lkv = [_dot3(l_k[i], vs[i]) for i in blocks]
    mkv = [_dot3(m_k[i], vs[i]) for i in blocks]
    kbar = [_dot3(ts[i], stack(blk(kt, i))) for i in blocks]
    u0 = [-_dot3(ts[i], stack(lkv[i])) for i in blocks]
    for i in blocks:
        m_scr[i] = jnp.where(eye256, dec[i * c:i * c + 1], 0.0) - _dot3(blk(ap, i), kbar[i], _TN) * m256
    for i in blocks:
        ct_scr[i] = _dot3(jnp.concatenate([u0[i], blk(v, i)], axis=0),
                          jnp.concatenate([blk(ap, i), blk(kp, i)], axis=0), _TN) * m256
    return kbar, u0, m_a, mkv


def _mixer_kernel(x_ref, shg_ref, sgl_ref, srw_ref, srt_ref, shift_ref, cos_ref, sin_ref,
                  p256_ref, glba_ref, mu_ref, wa2_ref, w2_ref, a2_ref, g2_ref,
                  tri_ref, trib_ref, onesb_ref, ind64_ref, indgl_ref, m256_ref, mgl_ref, dm_ref,
                  o_ref, nhg_ref, ngl_ref, nrw_ref, nrt_ref,
                  carry, m_scr, ct_scr, *, c, c_rw, tl):
    nblk = tl // c_rw
    j = pl.program_id(1)

    @pl.when(j == 0)
    def _():
        nhg_ref[...] = shg_ref[...]
        ngl_ref[...] = sgl_ref[...]
        nrw_ref[...] = srw_ref[...]
        nrt_ref[...] = srt_ref[...]
        carry[...] = shift_ref[...]

    def prm(i):
        return p256_ref[i:i + 1, :]

    tri = tri_ref[...]
    trib = trib_ref[...]
    onesb = onesb_ref[...]
    ind64 = ind64_ref[...]
    indgl = indgl_ref[...]
    m256 = m256_ref[...]
    lane = lax.broadcasted_iota(jnp.int32, (1, 256), 1)
    hmasks = [((lane >> 6) == h).astype(F32) for h in range(N_HEAD)]
    lane_k = lax.broadcasted_iota(jnp.int32, (1, GLA_KW), 1)
    hmasks_glk = [((lane_k >> 5) == h).astype(F32) for h in range(N_HEAD)]

    lb = prm(P_LB)
    forget = lb + (1.0 - lb) * _sigmoid(x_ref[:, C_HG_F:C_HG_F + 256])
    o_hg = _gla_tile(_silu(x_ref[:, C_HG_Q:C_HG_Q + 256]), 1.0 - forget, x_ref[:, C_HG_I:C_HG_I + 256],
                     jnp.log(jnp.maximum(forget, MIN_FORGET)), tri, ind64, hmasks, hmasks, nhg_ref, m256, c)
    o_ref[:, 0:256] = _head_rms_gate(o_hg, ind64, prm(P_HG_NW), x_ref[:, C_HG_G:C_HG_G + 256])

    za = _dot1(x_ref[:, C_GL_A:C_GL_A + LANES], wa2_ref[...]) + glba_ref[0:1, :]
    log_a = (jnp.minimum(za, 0.0) - jnp.log1p(jnp.exp(-jnp.abs(za)))) / GLA_TAU
    o_gl = _gla_tile(x_ref[:, C_GL_Q:C_GL_Q + GLA_KW] * GLA_DK ** -0.5, x_ref[:, C_GL_K:C_GL_K + GLA_KW],
                     x_ref[:, C_GL_V:C_GL_V + 256], log_a, tri, indgl, hmasks_glk, hmasks, ngl_ref,
                     mgl_ref[...], c)
    o_ref[:, 256:512] = _head_rms_gate(o_gl, ind64, prm(P_GL_NW), x_ref[:, C_GL_G:C_GL_G + 256])

    cosf = cos_ref[...]
    sins = sin_ref[...]
    q_r = x_ref[:, C_RT_Q:C_RT_Q + 256] * cosf + x_ref[:, C_RT_QS:C_RT_QS + 256] * sins
    k_r = (x_ref[:, C_RT_K:C_RT_K + 256] * cosf + x_ref[:, C_RT_KS:C_RT_KS + 256] * sins) * HEAD_V ** -0.5
    ort = _ret_tile(q_r, k_r, x_ref[:, C_RT_V:C_RT_V + 256], dm_ref[...], prm(P_RT_LG), hmasks, nrt_ref, m256)
    ms = _dotl2(ort * ort, ind64) * (1.0 / HEAD_V)
    o_ref[:, 768:1024] = ort * lax.rsqrt(ms + NORM_EPS) * _silu(x_ref[:, C_RT_G:C_RT_G + 256])

    rw = x_ref[:, C_RW:C_RW + RW_COLS]
    row = lax.broadcasted_iota(jnp.int32, (tl, RW_COLS), 0)
    prev = jnp.where(row == 0, jnp.broadcast_to(carry[...], (tl, RW_COLS)), pltpu.roll(rw, 1, 0))
    carry[...] = rw[tl - 1:tl, :]
    mix = rw + (prev - rw) * mu_ref[...]
    r = mix[:, 0:256]
    k0 = mix[:, 256:512]
    v = mix[:, 512:768]
    wa = mix[:, 768:896]
    gd = mix[:, 896:1024]
    lw = -math.exp(-0.5) * _sigmoid(prm(P_W0) + _dot1(jnp.tanh(wa), w2_ref[...]))
    a = _sigmoid(prm(P_A0) + _dot1(wa, a2_ref[...]))
    g = _dot1(_sigmoid(gd), g2_ref[...])
    kk = k0 * prm(P_KK)
    kk = kk / jnp.maximum(jnp.sqrt(_dotl2(kk * kk, ind64)), 1e-12)
    k = k0 * (1.0 + (a - 1.0) * prm(P_KA))
    bw = _dotr3(trib, lw)
    bc = _dotr3(onesb, lw)
    alpha = a * kk
    e_neg = jnp.exp(-bw)
    e_rem = jnp.exp(bc - bw)
    rw_kt = kk * jnp.exp(bw - lw)
    rw_ah = alpha * e_neg
    rw_kh = k * e_neg
    rw_rt = r * jnp.exp(bw)
    rw_ap = alpha * e_rem
    rw_kp = k * e_rem
    rw_dec = jnp.exp(bc)
    bonus = _dotl2(r * k * prm(P_RK), ind64) * v

    kbar, u0, m_a, mkv = _rwkv_blocks(rw_kt, rw_ah, rw_kh, rw_rt, rw_ap, rw_kp, v, rw_dec,
                                      hmasks, m256, m_scr, ct_scr, c_rw)

    st = nrw_ref[...]
    outs = []
    for i in range(nblk):
        sth, stl = _split(st)
        lhs = jnp.concatenate([kbar[i], rw_rt[i * c_rw:(i + 1) * c_rw]], axis=0)
        xr = _dot3s(*_split(lhs), sth, stl, _NT)
        u = u0[i] - xr[:c_rw]
        outs.append(xr[c_rw:] + _dot3(m_a[i], _stack_heads(u, hmasks)) + mkv[i])
        st = _dot3s(sth, stl, *_split(m_scr[i]), _NT) + ct_scr[i]
    nrw_ref[...] = st
    orw = outs[0] if nblk == 1 else jnp.concatenate(outs, axis=0)
    mean = _dotl2(orw, ind64) * (1.0 / HEAD_V)
    xc = orw - mean
    var = _dotl2(xc * xc, ind64) * (1.0 / HEAD_V)
    o_ref[:, 512:768] = (xc * lax.rsqrt(var + RW_LN_EPS) * prm(P_LNW) + prm(P_LNB) + bonus) * g


def _mixer_constants(tl, c_rw, log_gamma):
    t = np.arange(tl)
    same = (t[:, None] // c_rw) == (t[None, :] // c_rw)
    causal = t[None, :] <= t[:, None]
    tri = causal.astype(np.float32)
    trib = (same & causal).astype(np.float32)
    onesb = same.astype(np.float32)
    kv = np.arange(256)
    ind64 = ((kv[:, None] // HEAD_V) == (kv[None, :] // HEAD_V)).astype(np.float32)
    kg = np.arange(GLA_KW)
    indgl = ((kg[:, None] // GLA_DK) == (kv[None, :] // HEAD_V)).astype(np.float32)
    mgl = indgl.T
    diff = jnp.asarray((t[:, None] - t[None, :]).astype(np.float32))
    dm = jnp.concatenate([jnp.where(jnp.asarray(causal), jnp.exp(diff * log_gamma[h]), 0.0)
                          for h in range(N_HEAD)], axis=1)
    return (jnp.asarray(tri, BF16), jnp.asarray(trib, BF16), jnp.asarray(onesb, BF16),
            jnp.asarray(ind64, BF16), jnp.asarray(indgl, BF16), jnp.asarray(ind64, F32),
            jnp.asarray(mgl, F32), dm)


def _mixer(proj, states, shift, cosf, sins, lp, c, c_rw, tl):
    nb, length, _ = proj.shape
    nj = length // tl
    consts = _mixer_constants(tl, c_rw, lp['log_gamma'])

    def seq_spec(w):
        return pl.BlockSpec((None, tl, w), lambda b, j: (b, j, 0))

    def state_spec(w):
        return pl.BlockSpec((None, 256, w), lambda b, j: (b, 0, 0))

    def const_spec(a):
        nd = a.ndim
        return pl.BlockSpec(a.shape, lambda b, j: (0,) * nd)

    pos_spec = pl.BlockSpec((tl, 256), lambda b, j: (j, 0))
    params = (lp['p256'], lp['glba'], lp['mu'], lp['wa2'], lp['w2'], lp['a2'], lp['g2'])
    in_specs = ([seq_spec(N_MIX), state_spec(256), state_spec(GLA_KW), state_spec(256), state_spec(256),
                 pl.BlockSpec((None, 1, RW_COLS), lambda b, j: (b, 0, 0)), pos_spec, pos_spec]
                + [const_spec(a) for a in params] + [const_spec(a) for a in consts])
    out_specs = [seq_spec(N_BRANCH * BRANCH_W), state_spec(256), state_spec(GLA_KW), state_spec(256),
                 state_spec(256)]
    out_shape = [jax.ShapeDtypeStruct((nb, length, N_BRANCH * BRANCH_W), F32),
                 jax.ShapeDtypeStruct((nb, 256, 256), F32), jax.ShapeDtypeStruct((nb, 256, GLA_KW), F32),
                 jax.ShapeDtypeStruct((nb, 256, 256), F32), jax.ShapeDtypeStruct((nb, 256, 256), F32)]

    scratch = [pltpu.VMEM((1, RW_COLS), F32),
               pltpu.VMEM((tl // c_rw, 256, 256), F32), pltpu.VMEM((tl // c_rw, 256, 256), F32)]
    return pl.pallas_call(
        functools.partial(_mixer_kernel, c=c, c_rw=c_rw, tl=tl),
        grid=(nb, nj),
        in_specs=in_specs,
        out_specs=out_specs,
        out_shape=out_shape,
        scratch_shapes=scratch,
        compiler_params=pltpu.CompilerParams(dimension_semantics=("arbitrary", "arbitrary"),
                                             vmem_limit_bytes=VMEM_LIMIT_BYTES),
        name="mixer",
    )(proj, *states, shift, cosf, sins, *params, *consts)


_RW_PERM = np.concatenate([np.arange(0, 256), np.arange(320, 576), np.arange(576, 832),
                           np.arange(256, 320), np.arange(832, 896), np.arange(896, 1024)])
_RW_INV = np.argsort(_RW_PERM)
_HALF_SWAP = (np.arange(256) // HEAD_V) * HEAD_V + (np.arange(256) % HEAD_V + HEAD_V // 2) % HEAD_V


def _tile_heads(v):
    return jnp.tile(v, N_HEAD)


def _layer_params(l, lower_bounds, w_in, hg_norm_w, gla_wa2, gla_ba, gla_norm_w, rw_mu, rw_w0, rw_w2,
                  rw_a0, rw_a2, rw_g2, rw_kk, rw_ka, rw_rk, rw_ln_w, rw_ln_b, w_branch, w_out,
                  w_ffn_in, w_ffn_out):
    w = w_in[l]
    rwc = w[:, 1808:2832][:, _RW_PERM]
    rq = w[:, 2832:3088]
    rk = w[:, 3088:3344]
    wmix = jnp.concatenate([
        w[:, 0:1024],
        w[:, 1024:1536],
        w[:, 1536:1552], jnp.zeros((D_MODEL, LANES - GLA_RANK), F32),
        w[:, 1552:1808],
        rwc,
        rq, rk, w[:, 3344:3856],
        rq[:, _HALF_SWAP], rk[:, _HALF_SWAP],
    ], axis=1).astype(BF16)
    log_gamma = jnp.log1p(-jnp.exp2(-5.0 - jnp.arange(N_HEAD, dtype=F32)))
    rows = [lower_bounds[l], _tile_heads(hg_norm_w[l]), _tile_heads(gla_norm_w[l]), rw_w0[l], rw_a0[l],
            rw_kk[l], rw_ka[l], rw_rk[l], rw_ln_w[l], rw_ln_b[l], jnp.repeat(log_gamma, HEAD_V)]
    p256 = jnp.concatenate([jnp.stack(rows), jnp.zeros((16 - len(rows), 256), F32)], axis=0)
    zeros64 = jnp.zeros((64, 256), F32)
    return {
        'wmix': wmix,
        'log_gamma': log_gamma,
        'wg': w[:, 3856:].astype(BF16),
        'p256': p256,
        'glba': jnp.concatenate([gla_ba[l][None, :], jnp.zeros((7, GLA_KW), F32)], axis=0),
        'mu': rw_mu[l][_RW_PERM][None, :],
        'wa2': jnp.concatenate([gla_wa2[l], jnp.zeros((LANES - GLA_RANK, GLA_KW), F32)], axis=0).astype(BF16),
        'w2': jnp.concatenate([rw_w2[l], zeros64], axis=0).astype(BF16),
        'a2': jnp.concatenate([zeros64, rw_a2[l]], axis=0).astype(BF16),
        'g2': rw_g2[l].astype(BF16),
        'wb': w_branch[l].astype(BF16),
        'wo': w_out[l].astype(BF16),
        'wi': w_ffn_in[l].astype(BF16),
        'wfo': w_ffn_out[l].astype(BF16),
    }


def _rotary_tables(pos):
    half = HEAD_V // 2
    inv = ROPE_BASE ** (-jnp.arange(half, dtype=F32) / half)
    ang = pos[:, None] * inv[None, :]
    cos, sin = jnp.cos(ang), jnp.sin(ang)
    cosf = jnp.tile(jnp.concatenate([cos, cos], axis=1), (1, N_HEAD))
    sins = jnp.tile(jnp.concatenate([-sin, sin], axis=1), (1, N_HEAD))
    return cosf, sins


def _state_to_bd(s):
    nb, h, dk, dv = s.shape
    st = jnp.swapaxes(s, 2, 3)
    bd = jnp.einsum('bhvk,hg->bhvgk', st, jnp.eye(h, dtype=s.dtype))
    return bd.reshape(nb, h * dv, h * dk)


def _bd_to_state(bd, dk):
    nb = bd.shape[0]
    r = bd.reshape(nb, N_HEAD, HEAD_V, N_HEAD, dk)
    return jnp.einsum('bhvhk->bhkv', r)


def _group_layer(x, states, shift, tables, lp, attn_nw, ffn_nw, final_nw, final, c, c_rw, tl):
    nb, length, _ = x.shape
    xf = x.reshape(nb * length, D_MODEL)
    proj = _inproj(xf, attn_nw, lp['wmix']).reshape(nb, length, N_MIX)
    o, n_hg, n_gl, n_rw, n_rt = _mixer(proj, states, shift[:, None, _RW_PERM], tables[0], tables[1],
                                       lp, c, c_rw, tl)
    x1 = _merge(xf, o.reshape(nb * length, N_BRANCH * BRANCH_W), attn_nw, lp['wg'], lp['wb'], lp['wo'])
    x2 = _ffn(x1, ffn_nw, lp['wi'], lp['wfo'], final_nw, final)
    new_shift = proj[:, length - 1, C_RW:C_RW + RW_COLS][:, _RW_INV]
    new_states = (_bd_to_state(n_hg, HEAD_V), _bd_to_state(n_gl, GLA_DK), _bd_to_state(n_rw, HEAD_V),
                  new_shift, _bd_to_state(n_rt, HEAD_V))
    return x2.reshape(nb, length, D_MODEL), new_states


def kernel(x_prompt, x_sample, state_hgrn, state_gla, state_rwkv, state_rwkv_shift, state_ret,
           attn_norm_w, w_in, hg_lb_logits, hg_norm_w, gla_wa2, gla_ba, gla_norm_w,
           rw_mu, rw_w0, rw_w2, rw_a0, rw_a2, rw_g2, rw_kk, rw_ka, rw_rk, rw_ln_w, rw_ln_b,
           w_branch, w_out, ffn_norm_w, w_ffn_in, w_ffn_out, final_norm_w):
    depth = w_in.shape[0]
    lb_p = jax.nn.softmax(hg_lb_logits.astype(F32), axis=0)
    lower_bounds = jnp.cumsum(lb_p, axis=0) - lb_p[0:1]

    bp, lp_len, _ = x_prompt.shape
    bs, ls_len, _ = x_sample.shape
    past_len = 16384
    tab_p = _rotary_tables(jnp.arange(lp_len, dtype=F32))
    tab_s = _rotary_tables(float(past_len) + jnp.arange(ls_len, dtype=F32))
    zero_states = (jnp.zeros((bp, 256, 256), F32), jnp.zeros((bp, 256, GLA_KW), F32),
                   jnp.zeros((bp, 256, 256), F32), jnp.zeros((bp, 256, 256), F32))
    zero_shift = jnp.zeros((bp, RW_COLS), F32)
    final_nw = final_norm_w[None, :]

    xp, xs = x_prompt, x_sample
    p_states, s_states = [], []
    for l in range(depth):
        lp = _layer_params(l, lower_bounds, w_in, hg_norm_w, gla_wa2, gla_ba, gla_norm_w, rw_mu, rw_w0,
                           rw_w2, rw_a0, rw_a2, rw_g2, rw_kk, rw_ka, rw_rk, rw_ln_w, rw_ln_b, w_branch,
                           w_out, w_ffn_in, w_ffn_out)
        final = l == depth - 1
        anw, fnw = attn_norm_w[l][None, :], ffn_norm_w[l][None, :]
        xp, sp = _group_layer(xp, zero_states, zero_shift, tab_p, lp, anw, fnw, final_nw, final,
                              c=8, c_rw=16, tl=128)
        s_in = (_state_to_bd(state_hgrn[l]), _state_to_bd(state_gla[l]), _state_to_bd(state_rwkv[l]),
                _state_to_bd(state_ret[l]))
        xs, ss = _group_layer(xs, s_in, state_rwkv_shift[l], tab_s, lp, anw, fnw, final_nw, final,
                              c=ls_len, c_rw=ls_len, tl=ls_len)
        p_states.append(sp)
        s_states.append(ss)

    def stk(states, i):
        return jnp.stack([s[i] for s in states])

    return (xp, xs,
            stk(p_states, 0), stk(p_states, 1), stk(p_states, 2), stk(p_states, 3), stk(p_states, 4),
            stk(s_states, 0), stk(s_states, 1), stk(s_states, 2), stk(s_states, 3), stk(s_states, 4))
```

```python
import functools
import math

import jax
import jax.numpy as jnp
import numpy as np
from jax import lax
from jax.experimental import pallas as pl
from jax.experimental.pallas import tpu as pltpu

F32 = jnp.float32
BF16 = jnp.bfloat16

D_MODEL = 1024
N_HEAD = 4
HEAD_V = 64
BRANCH_W = N_HEAD * HEAD_V
N_BRANCH = 4
GLA_DK = 32
GLA_KW = N_HEAD * GLA_DK
GLA_RANK = 16
GLA_TAU = 16.0
RW_COLS = 1024
NORM_EPS = 1e-6
RW_LN_EPS = 64e-5
MIN_FORGET = 1e-30
ROPE_BASE = 10000.0
LOG2E = 1.4426950408889634
D_FF = 2816
N_MIX = 3968
PROMPT_TILE = 128
SAMPLE_SEQS_PER_TILE = 8
N_GATE = N_BRANCH * D_MODEL

VMEM_LIMIT_BYTES = 56 * 1024 * 1024
LANES = 128
SUBLANES = 8
BF16_ROWS = 16

C_HG_Q, C_HG_F, C_HG_I, C_HG_G = 0, 256, 512, 768
C_GL_Q, C_GL_K, C_GL_V, C_GL_A, C_GL_G = 1024, 1152, 1280, 1536, 1664
C_RW = 1920
C_RT_Q, C_RT_K, C_RT_V, C_RT_G = 2944, 3200, 3456, 3712

P_LB, P_HG_NW, P_GL_NW, P_W0, P_A0, P_KK, P_KA, P_RK, P_LNW, P_LNB, P_RT_LG = range(11)

_NN = (((1,), (0,)), ((), ()))
_NT = (((1,), (1,)), ((), ()))
_TN = (((0,), (0,)), ((), ()))


def _dg(a, b, dn):
    return lax.dot_general(a, b, dn, preferred_element_type=F32)


def _split(x):
    hi = x.astype(BF16)
    lo = (x - hi.astype(F32)).astype(BF16)
    return hi, lo


def _split3(x):
    x1 = x.astype(BF16)
    r = x - x1.astype(F32)
    x2 = r.astype(BF16)
    x3 = (r - x2.astype(F32)).astype(BF16)
    return x1, x2, x3


def _dot1(a, b, dn=_NN):
    return _dg(a.astype(BF16), b.astype(BF16), dn)


def _dot3s(ah, al, bh, bl, dn=_NN):
    return _dg(ah, bh, dn) + (_dg(ah, bl, dn) + _dg(al, bh, dn))


def _dot3(a, b, dn=_NN):
    return _dot3s(*_split(a), *_split(b), dn)


def _dots(pieces, b_exact, dn=_NN):
    out = _dg(pieces[0], b_exact, dn)
    for p in pieces[1:]:
        out = out + _dg(p, b_exact, dn)
    return out


def _dotsr(a_exact, pieces, dn=_NN):
    out = _dg(a_exact, pieces[0], dn)
    for p in pieces[1:]:
        out = out + _dg(a_exact, p, dn)
    return out


def _dotl2(a, b_exact, dn=_NN):
    return _dots(_split(a), b_exact, dn)


def _sigmoid(x):
    return jax.nn.sigmoid(x)


def _silu(x):
    return x * jax.nn.sigmoid(x)


def _rms(x, w):
    return x * lax.rsqrt(jnp.mean(x * x, axis=-1, keepdims=True) + NORM_EPS) * w


def _cat(parts, axis=0):
    return parts[0] if len(parts) == 1 else jnp.concatenate(parts, axis=axis)


def _by_rows(fn, rows, chunk, start=0):
    outs = [fn(slice(i, min(i + chunk, rows))) for i in range(start, rows, chunk)]
    if isinstance(outs[0], tuple):
        return tuple(_cat([o[j] for o in outs]) for j in range(len(outs[0])))
    return _cat(outs)


def _inproj_kernel(x_ref, nw_ref, w_ref, o_ref):
    h = _rms(x_ref[...], nw_ref[...])
    o_ref[...] = jnp.dot(h.astype(BF16), w_ref[...], preferred_element_type=F32)


def _merge_kernel(x_ref, o_ref, nw_ref, wg_ref, wb_ref, wo_ref, out_ref):
    x = x_ref[...]
    h = _rms(x, nw_ref[...]).astype(BF16)
    merged = None
    for b in range(N_BRANCH):
        gl = jnp.dot(h, wg_ref[:, b * D_MODEL:(b + 1) * D_MODEL], preferred_element_type=F32)
        ob = o_ref[:, b * BRANCH_W:(b + 1) * BRANCH_W].astype(BF16)
        up = jnp.dot(ob, wb_ref[b], preferred_element_type=F32)
        t = _sigmoid(gl) * up
        merged = t if merged is None else merged + t
    y = jnp.dot(merged.astype(BF16), wo_ref[...], preferred_element_type=F32)
    out_ref[...] = x + y


FFN_CHUNK = 1408


def _ffn_kernel(x_ref, nw_ref, wi_ref, wo_ref, fw_ref, out_ref, *, final):
    x = x_ref[...]
    h = _rms(x, nw_ref[...]).astype(BF16)
    acc = None
    for j in range(D_FF // FFN_CHUNK):
        lo = j * FFN_CHUNK
        g = jnp.dot(h, wi_ref[:, lo:lo + FFN_CHUNK], preferred_element_type=F32)
        u = jnp.dot(h, wi_ref[:, D_FF + lo:D_FF + lo + FFN_CHUNK], preferred_element_type=F32)
        a = (_silu(g) * u).astype(BF16)
        t = jnp.dot(a, wo_ref[lo:lo + FFN_CHUNK, :], preferred_element_type=F32)
        acc = t if acc is None else acc + t
    x2 = x + acc
    if final:
        x2 = _rms(x2, fw_ref[...])
    out_ref[...] = x2


def _dense_params():
    return pltpu.CompilerParams(dimension_semantics=("arbitrary",), vmem_limit_bytes=VMEM_LIMIT_BYTES)


def _const_spec(shape):
    nd = len(shape)
    return pl.BlockSpec(shape, lambda i: (0,) * nd)


def _token_tile(t):
    return 256 if t % 256 == 0 else t


def _inproj(x, nw, wmix):
    t = x.shape[0]
    tm = _token_tile(t)
    return pl.pallas_call(
        _inproj_kernel,
        grid=(t // tm,),
        in_specs=[pl.BlockSpec((tm, D_MODEL), lambda i: (i, 0)),
                  _const_spec((1, D_MODEL)), _const_spec((D_MODEL, N_MIX))],
        out_specs=pl.BlockSpec((tm, N_MIX), lambda i: (i, 0)),
        out_shape=jax.ShapeDtypeStruct((t, N_MIX), F32),
        compiler_params=_dense_params(),
        name="inproj",
    )(x, nw, wmix)


def _merge(x, o, nw, wg, wb, wo):
    t = x.shape[0]
    tm = _token_tile(t)
    return pl.pallas_call(
        _merge_kernel,
        grid=(t // tm,),
        in_specs=[pl.BlockSpec((tm, D_MODEL), lambda i: (i, 0)),
                  pl.BlockSpec((tm, N_BRANCH * BRANCH_W), lambda i: (i, 0)),
                  _const_spec((1, D_MODEL)), _const_spec((D_MODEL, N_GATE)),
                  _const_spec((N_BRANCH, BRANCH_W, D_MODEL)), _const_spec((D_MODEL, D_MODEL))],
        out_specs=pl.BlockSpec((tm, D_MODEL), lambda i: (i, 0)),
        out_shape=jax.ShapeDtypeStruct((t, D_MODEL), F32),
        compiler_params=_dense_params(),
        name="merge",
    )(x, o, nw, wg, wb, wo)


def _ffn(x, nw, wi, wo, fw, final):
    t = x.shape[0]
    tm = _token_tile(t)
    return pl.pallas_call(
        functools.partial(_ffn_kernel, final=final),
        grid=(t // tm,),
        in_specs=[pl.BlockSpec((tm, D_MODEL), lambda i: (i, 0)),
                  _const_spec((1, D_MODEL)), _const_spec((D_MODEL, 2 * D_FF)),
                  _const_spec((D_FF, D_MODEL)), _const_spec((1, D_MODEL))],
        out_specs=pl.BlockSpec((tm, D_MODEL), lambda i: (i, 0)),
        out_shape=jax.ShapeDtypeStruct((t, D_MODEL), F32),
        compiler_params=_dense_params(),
        name="ffn",
    )(x, nw, wi, wo, fw)


def _stack_heads(x, hm):
    return jnp.concatenate([x * hm[h] for h in range(N_HEAD)], axis=0)


def _state_in(s, mask):
    return jnp.concatenate([s] * N_HEAD, axis=1) * mask


def _state_out(st):
    return (st[:, 0:HEAD_V] + st[:, HEAD_V:2 * HEAD_V]) + (st[:, 2 * HEAD_V:3 * HEAD_V] + st[:, 3 * HEAD_V:])


def _gla_tile(res, q, k, v, g3, tri, ones_t, ind, hm_k, hm_v, st_ref, mask, c, nseg, seglen):
    tl = q.shape[0]
    nblk = seglen // c
    b = _dotsr(tri, g3)
    yield
    rowi = lax.broadcasted_iota(jnp.int32, (c, q.shape[1]), 0)
    group = 4 * c
    accs, qes = [], []
    for g0 in range(0, tl, group):
        pieces, vrots = [], []
        for r0 in range(g0, g0 + group, c):
            qb, kb, vb = q[r0:r0 + c], k[r0:r0 + c], v[r0:r0 + c]
            bb = b[r0:r0 + c]
            b2 = bb * LOG2E
            qes.append(qb * jnp.exp2(b2))
            pieces.append(qb * kb)
            vr = [vb]
            for d in range(1, c):
                e = jnp.exp2(b2 - pltpu.roll(b2, d, 0))
                pieces.append(jnp.where(rowi >= d, qb * pltpu.roll(kb, d, 0) * e, 0.0))
                vr.append(pltpu.roll(vb, d, 0))
            vrots.append(vr)
        a = _dot1(jnp.concatenate(pieces, axis=0), ind)
        for i, vr in enumerate(vrots):
            acc = a[i * c * c:i * c * c + c] * vr[0]
            for d in range(1, c):
                acc = acc + a[(i * c + d) * c:(i * c + d + 1) * c] * vr[d]
            accs.append(acc)
        yield
    acc = _cat(accs)
    qe = _cat(qes)
    if nblk > 1:
        scores = []
        for seg in range(nseg):
            base = seg * seglen
            for i in range(1, nblk):
                lo = base + i * c
                r = b[lo - 1:lo, :]
                qi = q[lo:lo + c] * jnp.exp(b[lo:lo + c] - r)
                ki = _by_rows(lambda s: k[s] * jnp.exp(r - b[s]), lo, 2 * BF16_ROWS, base)
                scores.append(_dot1(_stack_heads(qi, hm_k), ki, _NT))
                if i % 3 == 0:
                    yield
        parts = []
        n = 0
        for seg in range(nseg):
            base = seg * seglen
            parts.append(jnp.zeros((c, N_HEAD * HEAD_V), F32))
            for i in range(1, nblk):
                r4 = _dot1(scores[n], v[base:base + i * c])
                n += 1
                oi = r4[0:c] * hm_v[0]
                for h in range(1, N_HEAD):
                    oi = oi + r4[h * c:(h + 1) * c] * hm_v[h]
                parts.append(oi)
                if i % 3 == 0:
                    yield
        acc = acc + jnp.concatenate(parts, axis=0)
    outs = []
    for seg in range(nseg):
        lo, hi = seg * seglen, (seg + 1) * seglen
        blast = b[hi - 1:hi, :]
        dcol = jnp.exp(_dots([p[lo:hi] for p in g3], ones_t[lo:hi], _TN))
        st = st_ref[seg]
        outs.append(acc[lo:hi] + _dot1(qe[lo:hi], st))
        ke = _by_rows(lambda s: k[s] * jnp.exp(blast - b[s]), hi, 2 * BF16_ROWS, lo)
        st_ref[seg] = st * dcol + _dot1(ke, v[lo:hi], _TN) * mask
        yield
    res.append(_cat(outs))
    yield


def _ret_tile(res, q, k, v, dm, lg, rtdec, hm, st_ref, mask, nseg, seglen):
    tau = (lax.broadcasted_iota(jnp.int32, q.shape, 0) & (seglen - 1)).astype(F32)
    a = _dot3(q, _stack_heads(k, hm), _NT) * dm
    yield
    intra = _dot1(a, _stack_heads(v, hm))
    qe = q * jnp.exp((tau + 1.0) * lg)
    ke = k * jnp.exp((float(seglen) - 1.0 - tau) * lg)
    yield
    outs = []
    for seg in range(nseg):
        lo, hi = seg * seglen, (seg + 1) * seglen
        st = st_ref[seg]
        outs.append(intra[lo:hi] + _dot1(qe[lo:hi], st))
        st_ref[seg] = st * rtdec + _dot1(ke[lo:hi], v[lo:hi], _TN) * mask
    res.append(_cat(outs))
    yield


def _head_rms_gate(o, ind64, nw, gate):
    ms = _dotl2(o * o, ind64) * (1.0 / HEAD_V)
    return o * lax.rsqrt(ms + NORM_EPS) * nw * _silu(gate)


def _rwkv_blocks(kt, ah, kh, rt, ap, kp, v, dec, hmasks, m256, m_scr, c_scr, c):
    nblk = kt.shape[0] // c
    blocks = range(nblk)
    r4 = lax.broadcasted_iota(jnp.int32, (c, N_HEAD * c), 0)
    s4 = lax.broadcasted_iota(jnp.int32, (c, N_HEAD * c), 1) & (c - 1)
    strict = s4 < r4
    incl = s4 <= r4
    rr = lax.broadcasted_iota(jnp.int32, (N_HEAD * c, N_HEAD * c), 0)
    cc = lax.broadcasted_iota(jnp.int32, (N_HEAD * c, N_HEAD * c), 1)
    sh = int(math.log2(c))
    bd4 = ((rr >> sh) == (cc >> sh)).astype(F32)
    eye4 = (rr == cc).astype(F32)
    eye256 = (lax.broadcasted_iota(jnp.int32, (256, 256), 0) == lax.broadcasted_iota(jnp.int32, (256, 256), 1))

    def blk(x, i):
        return x[i * c:(i + 1) * c]

    def stack(x):
        return _stack_heads(x, hmasks)

    kr = [jnp.concatenate([blk(kt, i), blk(rt, i)], axis=0) for i in blocks]
    la = [_dot3(kr[i], stack(blk(ah, i)), _NT) for i in blocks]
    lk = [_dot3(kr[i], stack(blk(kh, i)), _NT) for i in blocks]
    l_a = [jnp.where(strict, x[:c], 0.0) for x in la]
    m_a = [jnp.where(incl, x[c:], 0.0) for x in la]
    l_k = [jnp.where(strict, x[:c], 0.0) for x in lk]
    m_k = [jnp.where(incl, x[c:], 0.0) for x in lk]
    nk = [-(jnp.concatenate([x] * N_HEAD, axis=0) * bd4) for x in l_a]
    pm = [eye4 + x for x in nk]
    for _ in range(sh - 1):
        nk = [_dot3(x, x) for x in nk]
        pm = [pm[i] + _dot3(pm[i], nk[i]) for i in blocks]
    ts = []
    for x in pm:
        t = x[0:c]
        for h in range(1, N_HEAD):
            t = t + x[h * c:(h + 1) * c]
        ts.append(t)
    vs = [stack(blk(v, i)) for i in blocks]
    lkv = [_dot3(l_k[i], vs[i]) for i in blocks]
    mkv = [_dot3(m_k[i], vs[i]) for i in blocks]
    kbar = [_dot3(ts[i], stack(blk(kt, i))) for i in blocks]
    u0 = [-_dot3(ts[i], stack(lkv[i])) for i in blocks]
    for i in blocks:
        m = jnp.where(eye256, dec[i * c:i * c + 1], 0.0) - _dot1(blk(ap, i), kbar[i], _TN) * m256
        m_scr[i] = m.astype(BF16)
    for i in blocks:
        c_scr[i] = _dot1(jnp.concatenate([blk(ap, i), blk(kp, i)], axis=0),
                         jnp.concatenate([u0[i], blk(v, i)], axis=0), _TN) * m256
    return kbar, u0, m_a, mkv


def _mixer_kernel(*refs, c, c_rw, nseg, seglen, nj, has_state):
    refs = list(refs)
    x_ref = refs.pop(0)
    if has_state:
        shg_ref, sgl_ref, srw_ref, srt_ref, shift_ref = refs[:5]
        refs = refs[5:]
    (cos_ref, sin_ref, p256_ref, glba_ref, mu_ref, wa2_ref, w2_ref, a2_ref, g2_ref,
     tri_ref, trib_ref, onesb_ref, ind64_ref, indgl_ref, m256_ref, mgl_ref, dm_ref, rtdec_ref, ones_ref,
     o_ref, nhg_ref, ngl_ref, nrw_ref, nrt_ref,
     st_hg, st_gl, st_rw, st_rt, carry, m_scr, c_scr) = refs
    tl = nseg * seglen
    nblk_seg = seglen // c_rw
    j = pl.program_id(1)

    def prm(i):
        return p256_ref[i:i + 1, :]

    tri = tri_ref[...]
    trib = trib_ref[...]
    onesb = onesb_ref[...]
    ones_t = ones_ref[...]
    ind64 = ind64_ref[...]
    indgl = indgl_ref[...]
    m256 = m256_ref[...]
    mgl = mgl_ref[...]
    lane = lax.broadcasted_iota(jnp.int32, (1, 256), 1)
    hmasks = [((lane >> 6) == h).astype(F32) for h in range(N_HEAD)]
    lane_k = lax.broadcasted_iota(jnp.int32, (1, GLA_KW), 1)
    hmasks_glk = [((lane_k >> 5) == h).astype(F32) for h in range(N_HEAD)]

    @pl.when(j == 0)
    def _():
        for seg in range(nseg):
            if has_state:
                st_hg[seg] = _state_in(shg_ref[seg], m256)
                st_gl[seg] = _state_in(sgl_ref[seg], mgl)
                st_rw[seg] = _state_in(srw_ref[seg], m256)
                st_rt[seg] = _state_in(srt_ref[seg], m256)
                carry[seg] = shift_ref[seg]
            else:
                st_hg[seg] = jnp.zeros((256, 256), F32)
                st_gl[seg] = jnp.zeros((GLA_KW, 256), F32)
                st_rw[seg] = jnp.zeros((256, 256), F32)
                st_rt[seg] = jnp.zeros((256, 256), F32)
                carry[seg] = jnp.zeros((1, RW_COLS), F32)

    def cols(c0, w):
        return x_ref.at[:, c0:c0 + w]

    def hgrn_work(res):
        lb = prm(P_LB)
        xq, xf = cols(C_HG_Q, 256), cols(C_HG_F, 256)

        def prep(s):
            forget = lb + (1.0 - lb) * _sigmoid(xf[s])
            return (_silu(xq[s]), 1.0 - forget) + _split3(jnp.log(jnp.maximum(forget, MIN_FORGET)))

        q, k, g1, g2_, g3_ = _by_rows(prep, tl, BF16_ROWS)
        yield
        yield from _gla_tile(res, q, k, cols(C_HG_I, 256), (g1, g2_, g3_), tri, ones_t, ind64,
                             hmasks, hmasks, st_hg, m256, c, nseg, seglen)
        o_ref[:, 0:256] = _head_rms_gate(res.pop(), ind64, prm(P_HG_NW), x_ref[:, C_HG_G:C_HG_G + 256])
        yield

    def gla_work(res):
        za = _dot1(x_ref[:, C_GL_A:C_GL_A + LANES], wa2_ref[...]) + glba_ref[0:1, :]

        def prep(s):
            z = za[s]
            return _split3((jnp.minimum(z, 0.0) - jnp.log1p(jnp.exp(-jnp.abs(z)))) / GLA_TAU)

        g3 = _by_rows(prep, tl, BF16_ROWS)
        yield
        yield from _gla_tile(res, x_ref[:, C_GL_Q:C_GL_Q + GLA_KW] * GLA_DK ** -0.5,
                             x_ref[:, C_GL_K:C_GL_K + GLA_KW], cols(C_GL_V, 256), g3,
                             tri, ones_t, indgl, hmasks_glk, hmasks, st_gl, mgl, c, nseg, seglen)
        o_ref[:, 256:512] = _head_rms_gate(res.pop(), ind64, prm(P_GL_NW), x_ref[:, C_GL_G:C_GL_G + 256])
        yield

    def ret_work(res):
        upper = (lane & (HEAD_V // 2)) != 0
        xq, xk = cols(C_RT_Q, 256), cols(C_RT_K, 256)

        def rot(x, s):
            sw = jnp.where(upper, pltpu.roll(x, HEAD_V // 2, 1), pltpu.roll(x, 256 - HEAD_V // 2, 1))
            return x * cos_ref[s, :] + sw * sin_ref[s, :]

        q_r, k_r = _by_rows(lambda s: (rot(xq[s], s), rot(xk[s], s) * HEAD_V ** -0.5), tl, SUBLANES)
        yield
        yield from _ret_tile(res, q_r, k_r, x_ref[:, C_RT_V:C_RT_V + 256], dm_ref[...], prm(P_RT_LG),
                             rtdec_ref[...], hmasks, st_rt, m256, nseg, seglen)
        ort = res.pop()
        ms = _dotl2(ort * ort, ind64) * (1.0 / HEAD_V)
        o_ref[:, 768:1024] = ort * lax.rsqrt(ms + NORM_EPS) * _silu(x_ref[:, C_RT_G:C_RT_G + 256])
        yield

    work = [ret_work([]), hgrn_work([]), gla_work([])]

    def advance(n):
        for _ in range(n):
            while work:
                try:
                    next(work[0])
                    break
                except StopIteration:
                    work.pop(0)

    xrw = cols(C_RW, RW_COLS)
    first = lax.broadcasted_iota(jnp.int32, (SUBLANES, RW_COLS), 0) == 0
    mu = mu_ref[...]

    def mix_rows(s):
        rw = xrw[s]
        if s.start % seglen == 0:
            prev0 = carry[s.start // seglen]
        else:
            prev0 = xrw[s.start - 1:s.start]
        prev = jnp.where(first, jnp.broadcast_to(prev0, (SUBLANES, RW_COLS)), pltpu.roll(rw, 1, 0))
        mix = rw + (prev - rw) * mu
        k0 = pltpu.roll(mix[:, 256:640], 320, 1)[:, 0:256]
        v_ = pltpu.roll(mix[:, 512:896], 320, 1)[:, 0:256]
        kk0 = k0 * prm(P_KK)
        return (mix[:, 0:256], k0, v_, jnp.tanh(mix[:, 256:384]), mix[:, 768:896],
                _sigmoid(mix[:, 896:1024]), kk0, kk0 * kk0)

    r, k0, v, t_w, s_a, s_g, kk0, kk0sq = _by_rows(mix_rows, tl, SUBLANES)
    for seg in range(nseg):
        hi = (seg + 1) * seglen
        carry[seg] = xrw[hi - 1:hi]
    lw_pre = _dot1(t_w, w2_ref[...])
    a_pre = _dot1(s_a, a2_ref[...])
    g = _dot1(s_g, g2_ref[...])
    n2 = _dotl2(kk0sq, ind64)

    def gates(s):
        lw = -math.exp(-0.5) * _sigmoid(prm(P_W0) + lw_pre[s])
        a = _sigmoid(prm(P_A0) + a_pre[s])
        kk = kk0[s] / jnp.maximum(jnp.sqrt(n2[s]), 1e-12)
        k = k0[s] * (1.0 + (a - 1.0) * prm(P_KA))
        return (lw, a * kk, kk, k, r[s] * k * prm(P_RK)) + _split3(lw)

    lw, alpha, kk, k, rkp, lw1, lw2, lw3 = _by_rows(gates, tl, BF16_ROWS)
    bw = _dotsr(trib, (lw1, lw2, lw3))
    bc = _dotsr(onesb, (lw1, lw2, lw3))

    def scaled(s):
        bws, bcs = bw[s], bc[s]
        e_neg = jnp.exp(-bws)
        e_rem = jnp.exp(bcs - bws)
        return (kk[s] * jnp.exp(bws - lw[s]), alpha[s] * e_neg, k[s] * e_neg, r[s] * jnp.exp(bws),
                alpha[s] * e_rem, k[s] * e_rem, jnp.exp(bcs))

    rw_kt, rw_ah, rw_kh, rw_rt, rw_ap, rw_kp, rw_dec = _by_rows(scaled, tl, SUBLANES)
    bonus = _dotl2(rkp, ind64) * v

    kbar, u0, m_a, mkv = _rwkv_blocks(rw_kt, rw_ah, rw_kh, rw_rt, rw_ap, rw_kp, v, rw_dec,
                                      hmasks, m256, m_scr, c_scr, c_rw)

    n_units = 3 * (tl // (4 * c) + 6) + (6 * (nblk_seg // 3 + 1) if seglen > c else 0)
    per_step = -(-n_units // nblk_seg)
    sts = [st_rw[seg] for seg in range(nseg)]
    outs = [[None] * nblk_seg for _ in range(nseg)]
    for i in range(nblk_seg):
        for seg in range(nseg):
            blk = seg * nblk_seg + i
            sth, stl = _split(sts[seg])
            krh, krl = _split(jnp.concatenate([kbar[blk], rw_rt[blk * c_rw:(blk + 1) * c_rw]], axis=0))
            big = jnp.concatenate([m_scr[blk], krh], axis=0)
            res = _dg(big, sth, _NN) + _dg(big, stl, _NN)
            xr = res[256:] + _dg(krl, sth, _NN)
            u = u0[blk] - xr[:c_rw]
            outs[seg][i] = xr[c_rw:] + _dot3(m_a[blk], _stack_heads(u, hmasks)) + mkv[blk]
            sts[seg] = res[:256] + c_scr[blk]
        advance(per_step)
    while work:
        advance(1)
    for seg in range(nseg):
        st_rw[seg] = sts[seg]
    orw = _cat([o for seg_outs in outs for o in seg_outs])
    mean = _dotl2(orw, ind64) * (1.0 / HEAD_V)
    xc = orw - mean
    var = _dotl2(xc * xc, ind64) * (1.0 / HEAD_V)
    o_ref[:, 512:768] = (xc * lax.rsqrt(var + RW_LN_EPS) * prm(P_LNW) + prm(P_LNB) + bonus) * g

    @pl.when(j == nj - 1)
    def _():
        for seg in range(nseg):
            nhg_ref[seg] = _state_out(st_hg[seg])
            ngl_ref[seg] = _state_out(st_gl[seg])
            nrw_ref[seg] = _state_out(st_rw[seg])
            nrt_ref[seg] = _state_out(st_rt[seg])


def _mixer_constants(nseg, seglen, c_rw, log_gamma):
    tl = nseg * seglen
    t = np.arange(tl)
    same = (t[:, None] // c_rw) == (t[None, :] // c_rw)
    same_seg = (t[:, None] // seglen) == (t[None, :] // seglen)
    causal = (t[None, :] <= t[:, None]) & same_seg
    tri = causal.astype(np.float32)
    trib = (same & causal).astype(np.float32)
    onesb = same.astype(np.float32)
    kv = np.arange(256)
    ind64 = ((kv[:, None] // HEAD_V) == (kv[None, :] // HEAD_V)).astype(np.float32)
    kg = np.arange(GLA_KW)
    indgl = ((kg[:, None] // GLA_DK) == (kv[None, :] // HEAD_V)).astype(np.float32)
    diff = jnp.asarray((t[:, None] - t[None, :]).astype(np.float32))
    dm = jnp.concatenate([jnp.where(jnp.asarray(causal), jnp.exp(diff * log_gamma[h]), 0.0)
                          for h in range(N_HEAD)], axis=1)
    rtdec = jnp.broadcast_to(jnp.exp(float(seglen) * jnp.repeat(log_gamma, HEAD_V))[:, None], (256, 256))
    ones_t = np.ones((tl, 256), np.float32)
    return (jnp.asarray(tri, BF16), jnp.asarray(trib, BF16), jnp.asarray(onesb, BF16),
            jnp.asarray(ind64, BF16), jnp.asarray(indgl, BF16), jnp.asarray(ind64, F32),
            jnp.asarray(indgl, F32), dm, rtdec, jnp.asarray(ones_t, BF16))


def _mixer(proj, states, shift, cosf, sins, lp, c, c_rw, nseg, seglen):
    ng, rows, _ = proj.shape
    tl = nseg * seglen
    nj = rows // tl
    has_state = states is not None
    consts = _mixer_constants(nseg, seglen, c_rw, lp['log_gamma'])

    def seq_spec(w):
        return pl.BlockSpec((None, tl, w), lambda g, j: (g, j, 0))

    def state_spec(hdk):
        return pl.BlockSpec((nseg, hdk, HEAD_V), lambda g, j: (g, 0, 0))

    def const_spec(a):
        nd = a.ndim
        return pl.BlockSpec(a.shape, lambda g, j: (0,) * nd)

    pos_spec = pl.BlockSpec((tl, 256), lambda g, j: (j, 0))
    params = (lp['p256'], lp['glba'], lp['mu'], lp['wa2'], lp['w2'], lp['a2'], lp['g2'])
    state_specs = [state_spec(256), state_spec(GLA_KW), state_spec(256), state_spec(256)]
    in_specs = [seq_spec(N_MIX)]
    args = [proj]
    if has_state:
        in_specs += state_specs + [pl.BlockSpec((nseg, 1, RW_COLS), lambda g, j: (g, 0, 0))]
        args += list(states) + [shift]
    in_specs += [pos_spec, pos_spec] + [const_spec(a) for a in params] + [const_spec(a) for a in consts]
    args += [cosf, sins, *params, *consts]
    out_specs = [seq_spec(N_BRANCH * BRANCH_W)] + state_specs
    nseq = ng * nseg
    out_shape = [jax.ShapeDtypeStruct((ng, rows, N_BRANCH * BRANCH_W), F32),
                 jax.ShapeDtypeStruct((nseq, 256, HEAD_V), F32), jax.ShapeDtypeStruct((nseq, GLA_KW, HEAD_V), F32),
                 jax.ShapeDtypeStruct((nseq, 256, HEAD_V), F32), jax.ShapeDtypeStruct((nseq, 256, HEAD_V), F32)]
    nblk = tl // c_rw
    scratch = [pltpu.VMEM((nseg, 256, 256), F32), pltpu.VMEM((nseg, GLA_KW, 256), F32),
               pltpu.VMEM((nseg, 256, 256), F32), pltpu.VMEM((nseg, 256, 256), F32),
               pltpu.VMEM((nseg, 1, RW_COLS), F32),
               pltpu.VMEM((nblk, 256, 256), BF16), pltpu.VMEM((nblk, 256, 256), F32)]
    return pl.pallas_call(
        functools.partial(_mixer_kernel, c=c, c_rw=c_rw, nseg=nseg, seglen=seglen, nj=nj,
                          has_state=has_state),
        grid=(ng, nj),
        in_specs=in_specs,
        out_specs=out_specs,
        out_shape=out_shape,
        scratch_shapes=scratch,
        compiler_params=pltpu.CompilerParams(dimension_semantics=("arbitrary", "arbitrary"),
                                             vmem_limit_bytes=VMEM_LIMIT_BYTES),
        name="mixer",
    )(*args)


def _tile_heads(v):
    return jnp.tile(v, N_HEAD)


def _layer_params(l, lower_bounds, w_in, hg_norm_w, gla_wa2, gla_ba, gla_norm_w, rw_mu, rw_w0, rw_w2,
                  rw_a0, rw_a2, rw_g2, rw_kk, rw_ka, rw_rk, rw_ln_w, rw_ln_b, w_branch, w_out,
                  w_ffn_in, w_ffn_out):
    w = w_in[l]
    wmix = jnp.concatenate([w[:, 0:1552], jnp.zeros((D_MODEL, LANES - GLA_RANK), F32), w[:, 1552:3856]],
                           axis=1).astype(BF16)
    log_gamma = jnp.log1p(-jnp.exp2(-5.0 - jnp.arange(N_HEAD, dtype=F32)))
    rows = [lower_bounds[l], _tile_heads(hg_norm_w[l]), _tile_heads(gla_norm_w[l]), rw_w0[l], rw_a0[l],
            rw_kk[l], rw_ka[l], rw_rk[l], rw_ln_w[l], rw_ln_b[l], jnp.repeat(log_gamma, HEAD_V)]
    p256 = jnp.concatenate([jnp.stack(rows), jnp.zeros((16 - len(rows), 256), F32)], axis=0)
    zeros64 = jnp.zeros((64, 256), F32)
    return {
        'wmix': wmix,
        'log_gamma': log_gamma,
        'wg': w[:, 3856:].astype(BF16),
        'p256': p256,
        'glba': jnp.concatenate([gla_ba[l][None, :], jnp.zeros((7, GLA_KW), F32)], axis=0),
        'mu': rw_mu[l][None, :],
        'wa2': jnp.concatenate([gla_wa2[l], jnp.zeros((LANES - GLA_RANK, GLA_KW), F32)], axis=0).astype(BF16),
        'w2': jnp.concatenate([rw_w2[l], zeros64], axis=0).astype(BF16),
        'a2': jnp.concatenate([zeros64, rw_a2[l]], axis=0).astype(BF16),
        'g2': rw_g2[l].astype(BF16),
        'wb': w_branch[l].astype(BF16),
        'wo': w_out[l].astype(BF16),
        'wi': w_ffn_in[l].astype(BF16),
        'wfo': w_ffn_out[l].astype(BF16),
    }


def _rotary_tables(pos):
    half = HEAD_V // 2
    inv = ROPE_BASE ** (-jnp.arange(half, dtype=F32) / half)
    ang = pos[:, None] * inv[None, :]
    cos, sin = jnp.cos(ang), jnp.sin(ang)
    cosf = jnp.tile(jnp.concatenate([cos, cos], axis=1), (1, N_HEAD))
    sins = jnp.tile(jnp.concatenate([-sin, sin], axis=1), (1, N_HEAD))
    return cosf, sins


def _group_layer(x, states, shift, tables, lp, attn_nw, ffn_nw, final_nw, final, c, c_rw, nseg, seglen):
    nb, length, _ = x.shape
    xf = x.reshape(nb * length, D_MODEL)
    proj = _inproj(xf, attn_nw, lp['wmix'])
    flat_states = None
    if states is not None:
        flat_states = [s.reshape(nb, s.shape[1] * s.shape[2], HEAD_V) for s in states]
        shift = shift[:, None, :]
    o, n_hg, n_gl, n_rw, n_rt = _mixer(proj.reshape(nb // nseg, nseg * length, N_MIX), flat_states, shift,
                                       tables[0], tables[1], lp, c, c_rw, nseg, seglen)
    x1 = _merge(xf, o.reshape(nb * length, N_BRANCH * BRANCH_W), attn_nw, lp['wg'], lp['wb'], lp['wo'])
    x2 = _ffn(x1, ffn_nw, lp['wi'], lp['wfo'], final_nw, final)
    new_shift = proj.reshape(nb, length, N_MIX)[:, length - 1, C_RW:C_RW + RW_COLS]
    new_states = (n_hg.reshape(nb, N_HEAD, HEAD_V, HEAD_V), n_gl.reshape(nb, N_HEAD, GLA_DK, HEAD_V),
                  n_rw.reshape(nb, N_HEAD, HEAD_V, HEAD_V), new_shift,
                  n_rt.reshape(nb, N_HEAD, HEAD_V, HEAD_V))
    return x2.reshape(nb, length, D_MODEL), new_states


def kernel(x_prompt, x_sample, state_hgrn, state_gla, state_rwkv, state_rwkv_shift, state_ret,
           attn_norm_w, w_in, hg_lb_logits, hg_norm_w, gla_wa2, gla_ba, gla_norm_w,
           rw_mu, rw_w0, rw_w2, rw_a0, rw_a2, rw_g2, rw_kk, rw_ka, rw_rk, rw_ln_w, rw_ln_b,
           w_branch, w_out, ffn_norm_w, w_ffn_in, w_ffn_out, final_norm_w):
    depth = w_in.shape[0]
    lb_p = jax.nn.softmax(hg_lb_logits.astype(F32), axis=0)
    lower_bounds = jnp.cumsum(lb_p, axis=0) - lb_p[0:1]

    bp, lp_len, _ = x_prompt.shape
    bs, ls_len, _ = x_sample.shape
    past_len = 16384
    tab_p = _rotary_tables(jnp.arange(lp_len, dtype=F32))
    tab_s = _rotary_tables(float(past_len) + jnp.arange(ls_len, dtype=F32))
    tab_s = tuple(jnp.tile(t, (SAMPLE_SEQS_PER_TILE, 1)) for t in tab_s)
    final_nw = final_norm_w[None, :]

    xp, xs = x_prompt, x_sample
    p_states, s_states = [], []
    for l in range(depth):
        lp = _layer_params(l, lower_bounds, w_in, hg_norm_w, gla_wa2, gla_ba, gla_norm_w, rw_mu, rw_w0,
                           rw_w2, rw_a0, rw_a2, rw_g2, rw_kk, rw_ka, rw_rk, rw_ln_w, rw_ln_b, w_branch,
                           w_out, w_ffn_in, w_ffn_out)
        final = l == depth - 1
        anw, fnw = attn_norm_w[l][None, :], ffn_norm_w[l][None, :]
        xp, sp = _group_layer(xp, None, None, tab_p, lp, anw, fnw, final_nw, final,
                              c=8, c_rw=16, nseg=1, seglen=PROMPT_TILE)
        s_in = (state_hgrn[l], state_gla[l], state_rwkv[l], state_ret[l])
        xs, ss = _group_layer(xs, s_in, state_rwkv_shift[l], tab_s, lp, anw, fnw, final_nw, final,
                              c=ls_len, c_rw=ls_len, nseg=SAMPLE_SEQS_PER_TILE, seglen=ls_len)
        p_states.append(sp)
        s_states.append(ss)

    def stk(states, i):
        return jnp.stack([s[i] for s in states])

    return (xp, xs,
            stk(p_states, 0), stk(p_states, 1), stk(p_states, 2), stk(p_states, 3), stk(p_states, 4),
            stk(s_states, 0), stk(s_states, 1), stk(s_states, 2), stk(s_states, 3), stk(s_states, 4))
```

```python
import functools
import math

import jax
import jax.numpy as jnp
import numpy as np
from jax import lax
from jax.experimental import pallas as pl
from jax.experimental.pallas import tpu as pltpu

F32 = jnp.float32
BF16 = jnp.bfloat16

D_MODEL = 1024
N_HEAD = 4
HEAD_V = 64
BRANCH_W = N_HEAD * HEAD_V
N_BRANCH = 4
GLA_DK = 32
GLA_KW = N_HEAD * GLA_DK
GLA_RANK = 16
GLA_TAU = 16.0
RW_COLS = 1024
NORM_EPS = 1e-6
RW_LN_EPS = 64e-5
MIN_FORGET = 1e-30
ROPE_BASE = 10000.0
LOG2E = 1.4426950408889634
D_FF = 2816
N_MIX = 3968
PROMPT_TILE = 128
SAMPLE_SEQS_PER_TILE = 8
N_GATE = N_BRANCH * D_MODEL

VMEM_LIMIT_BYTES = 56 * 1024 * 1024
LANES = 128
SUBLANES = 8
BF16_ROWS = 16

C_HG_Q, C_HG_F, C_HG_I, C_HG_G = 0, 256, 512, 768
C_GL_Q, C_GL_K, C_GL_V, C_GL_A, C_GL_G = 1024, 1152, 1280, 1536, 1664
C_RW = 1920
C_RT_Q, C_RT_K, C_RT_V, C_RT_G = 2944, 3200, 3456, 3712

P_LB, P_HG_NW, P_GL_NW, P_W0, P_A0, P_KK, P_KA, P_RK, P_LNW, P_LNB, P_RT_LG = range(11)

_NN = (((1,), (0,)), ((), ()))
_NT = (((1,), (1,)), ((), ()))
_TN = (((0,), (0,)), ((), ()))


def _dg(a, b, dn):
    return lax.dot_general(a, b, dn, preferred_element_type=F32)


def _split(x):
    hi = x.astype(BF16)
    lo = (x - hi.astype(F32)).astype(BF16)
    return hi, lo


def _split3(x):
    x1 = x.astype(BF16)
    r = x - x1.astype(F32)
    x2 = r.astype(BF16)
    x3 = (r - x2.astype(F32)).astype(BF16)
    return x1, x2, x3


def _dot1(a, b, dn=_NN):
    return _dg(a.astype(BF16), b.astype(BF16), dn)


def _dots(pieces, b_exact, dn=_NN):
    out = _dg(pieces[0], b_exact, dn)
    for p in pieces[1:]:
        out = out + _dg(p, b_exact, dn)
    return out


def _dotsr(a_exact, pieces, dn=_NN):
    out = _dg(a_exact, pieces[0], dn)
    for p in pieces[1:]:
        out = out + _dg(a_exact, p, dn)
    return out


def _headsum(x, ind):
    return _dg(x.astype(BF16), ind, _NN)


def _sigmoid(x):
    return jax.nn.sigmoid(x)


def _silu(x):
    return x * jax.nn.sigmoid(x)


def _rms(x, w):
    return x * lax.rsqrt(jnp.mean(x * x, axis=-1, keepdims=True) + NORM_EPS) * w


def _cat(parts, axis=0):
    return parts[0] if len(parts) == 1 else jnp.concatenate(parts, axis=axis)


def _by_rows(fn, rows, chunk, start=0):
    outs = [fn(slice(i, min(i + chunk, rows))) for i in range(start, rows, chunk)]
    if isinstance(outs[0], tuple):
        return tuple(_cat([o[j] for o in outs]) for j in range(len(outs[0])))
    return _cat(outs)


def _inproj_kernel(x_ref, nw_ref, w_ref, o_ref):
    h = _rms(x_ref[...], nw_ref[...])
    o_ref[...] = jnp.dot(h.astype(BF16), w_ref[...], preferred_element_type=F32)


def _merge_kernel(x_ref, o_ref, nw_ref, wg_ref, wb_ref, wo_ref, out_ref):
    x = x_ref[...]
    h = _rms(x, nw_ref[...]).astype(BF16)
    merged = None
    for b in range(N_BRANCH):
        gl = jnp.dot(h, wg_ref[:, b * D_MODEL:(b + 1) * D_MODEL], preferred_element_type=F32)
        ob = o_ref[:, b * BRANCH_W:(b + 1) * BRANCH_W].astype(BF16)
        up = jnp.dot(ob, wb_ref[b], preferred_element_type=F32)
        t = _sigmoid(gl) * up
        merged = t if merged is None else merged + t
    y = jnp.dot(merged.astype(BF16), wo_ref[...], preferred_element_type=F32)
    out_ref[...] = x + y


FFN_CHUNK = 1408


def _ffn_kernel(x_ref, nw_ref, wi_ref, wo_ref, fw_ref, out_ref, *, final):
    x = x_ref[...]
    h = _rms(x, nw_ref[...]).astype(BF16)
    acc = None
    for j in range(D_FF // FFN_CHUNK):
        lo = j * FFN_CHUNK
        g = jnp.dot(h, wi_ref[:, lo:lo + FFN_CHUNK], preferred_element_type=F32)
        u = jnp.dot(h, wi_ref[:, D_FF + lo:D_FF + lo + FFN_CHUNK], preferred_element_type=F32)
        a = (_silu(g) * u).astype(BF16)
        t = jnp.dot(a, wo_ref[lo:lo + FFN_CHUNK, :], preferred_element_type=F32)
        acc = t if acc is None else acc + t
    x2 = x + acc
    if final:
        x2 = _rms(x2, fw_ref[...])
    out_ref[...] = x2


def _dense_params():
    return pltpu.CompilerParams(dimension_semantics=("arbitrary",), vmem_limit_bytes=VMEM_LIMIT_BYTES)


def _const_spec(shape):
    nd = len(shape)
    return pl.BlockSpec(shape, lambda i: (0,) * nd)


def _token_tile(t):
    return 512 if t % 512 == 0 else t


N_IN = 7952
GATE_COL0 = 3856
WPREP_ROWS = 128


def _wprep_kernel(w_ref, wmix_ref, wg_ref):
    a0 = C_GL_A
    lane = lax.broadcasted_iota(jnp.int32, (1, LANES), 1)
    wmix_ref[:, 0:a0] = w_ref[:, 0:a0].astype(BF16)
    wmix_ref[:, a0:a0 + LANES] = jnp.where(lane < GLA_RANK, w_ref[:, a0:a0 + LANES], 0.0).astype(BF16)
    wmix_ref[:, a0 + LANES:N_MIX] = w_ref[:, a0 + GLA_RANK:GATE_COL0].astype(BF16)
    wg_ref[...] = w_ref[:, GATE_COL0:N_IN].astype(BF16)


def _wprep(w_in, l):
    return pl.pallas_call(
        _wprep_kernel,
        grid=(D_MODEL // WPREP_ROWS,),
        in_specs=[pl.BlockSpec((None, WPREP_ROWS, N_IN), lambda i: (l, i, 0))],
        out_specs=[pl.BlockSpec((WPREP_ROWS, N_MIX), lambda i: (i, 0)),
                   pl.BlockSpec((WPREP_ROWS, N_GATE), lambda i: (i, 0))],
        out_shape=[jax.ShapeDtypeStruct((D_MODEL, N_MIX), BF16), jax.ShapeDtypeStruct((D_MODEL, N_GATE), BF16)],
        compiler_params=_dense_params(),
        name="wprep",
    )(w_in)


def _inproj(x, nw, wmix):
    t = x.shape[0]
    tm = _token_tile(t)
    return pl.pallas_call(
        _inproj_kernel,
        grid=(t // tm,),
        in_specs=[pl.BlockSpec((tm, D_MODEL), lambda i: (i, 0)),
                  _const_spec((1, D_MODEL)), _const_spec((D_MODEL, N_MIX))],
        out_specs=pl.BlockSpec((tm, N_MIX), lambda i: (i, 0)),
        out_shape=jax.ShapeDtypeStruct((t, N_MIX), F32),
        compiler_params=_dense_params(),
        name="inproj",
    )(x, nw, wmix)


def _merge(x, o, nw, wg, wb, wo):
    t = x.shape[0]
    tm = _token_tile(t)
    return pl.pallas_call(
        _merge_kernel,
        grid=(t // tm,),
        in_specs=[pl.BlockSpec((tm, D_MODEL), lambda i: (i, 0)),
                  pl.BlockSpec((tm, N_BRANCH * BRANCH_W), lambda i: (i, 0)),
                  _const_spec((1, D_MODEL)), _const_spec((D_MODEL, N_GATE)),
                  _const_spec((N_BRANCH, BRANCH_W, D_MODEL)), _const_spec((D_MODEL, D_MODEL))],
        out_specs=pl.BlockSpec((tm, D_MODEL), lambda i: (i, 0)),
        out_shape=jax.ShapeDtypeStruct((t, D_MODEL), F32),
        compiler_params=_dense_params(),
        name="merge",
    )(x, o, nw, wg, wb, wo)


def _ffn(x, nw, wi, wo, fw, final):
    t = x.shape[0]
    tm = _token_tile(t)
    return pl.pallas_call(
        functools.partial(_ffn_kernel, final=final),
        grid=(t // tm,),
        in_specs=[pl.BlockSpec((tm, D_MODEL), lambda i: (i, 0)),
                  _const_spec((1, D_MODEL)), _const_spec((D_MODEL, 2 * D_FF)),
                  _const_spec((D_FF, D_MODEL)), _const_spec((1, D_MODEL))],
        out_specs=pl.BlockSpec((tm, D_MODEL), lambda i: (i, 0)),
        out_shape=jax.ShapeDtypeStruct((t, D_MODEL), F32),
        compiler_params=_dense_params(),
        name="ffn",
    )(x, nw, wi, wo, fw)


def _stack_heads(x, hm):
    return jnp.concatenate([x * hm[h] for h in range(N_HEAD)], axis=0)


def _state_in(s, mask):
    return jnp.concatenate([s] * N_HEAD, axis=1) * mask


def _state_out(st):
    return (st[:, 0:HEAD_V] + st[:, HEAD_V:2 * HEAD_V]) + (st[:, 2 * HEAD_V:3 * HEAD_V] + st[:, 3 * HEAD_V:])


def _gla_tile(res, q, k, v, g3, tri, ones_t, ind, hm_k, hm_v, st_ref, mask, c, nseg, seglen):
    tl = q.shape[0]
    nblk = seglen // c
    b = _dotsr(tri, g3)
    yield
    rowi = lax.broadcasted_iota(jnp.int32, (c, q.shape[1]), 0)
    group = 4 * c
    accs, qes = [], []
    for g0 in range(0, tl, group):
        pieces, vrots = [], []
        for r0 in range(g0, g0 + group, c):
            qb, kb, vb = q[r0:r0 + c], k[r0:r0 + c], v[r0:r0 + c]
            bb = b[r0:r0 + c]
            b2 = bb * LOG2E
            qes.append(qb * jnp.exp2(b2))
            pieces.append(qb * kb)
            vr = [vb]
            for d in range(1, c):
                e = jnp.exp2(b2 - pltpu.roll(b2, d, 0))
                pieces.append(jnp.where(rowi >= d, qb * pltpu.roll(kb, d, 0) * e, 0.0))
                vr.append(pltpu.roll(vb, d, 0))
            vrots.append(vr)
        a = _dot1(jnp.concatenate(pieces, axis=0), ind)
        for i, vr in enumerate(vrots):
            acc = a[i * c * c:i * c * c + c] * vr[0]
            for d in range(1, c):
                acc = acc + a[(i * c + d) * c:(i * c + d + 1) * c] * vr[d]
            accs.append(acc)
        yield
    acc = _cat(accs)
    qe = _cat(qes)
    if nblk > 1:
        scores = []
        for seg in range(nseg):
            base = seg * seglen
            for i in range(1, nblk):
                lo = base + i * c
                r = b[lo - 1:lo, :]
                qi = q[lo:lo + c] * jnp.exp(b[lo:lo + c] - r)
                ki = _by_rows(lambda s: k[s] * jnp.exp(r - b[s]), lo, 2 * BF16_ROWS, base)
                scores.append(_dot1(_stack_heads(qi, hm_k), ki, _NT))
                if i % 3 == 0:
                    yield
        parts = []
        n = 0
        for seg in range(nseg):
            base = seg * seglen
            parts.append(jnp.zeros((c, N_HEAD * HEAD_V), F32))
            for i in range(1, nblk):
                r4 = _dot1(scores[n], v[base:base + i * c])
                n += 1
                oi = r4[0:c] * hm_v[0]
                for h in range(1, N_HEAD):
                    oi = oi + r4[h * c:(h + 1) * c] * hm_v[h]
                parts.append(oi)
                if i % 3 == 0:
                    yield
        acc = acc + jnp.concatenate(parts, axis=0)
    outs = []
    for seg in range(nseg):
        lo, hi = seg * seglen, (seg + 1) * seglen
        blast = b[hi - 1:hi, :]
        dcol = jnp.exp(_dots([p[lo:hi] for p in g3], ones_t[lo:hi], _TN))
        st = st_ref[seg]
        outs.append(acc[lo:hi] + _dot1(qe[lo:hi], st))
        ke = _by_rows(lambda s: k[s] * jnp.exp(blast - b[s]), hi, 2 * BF16_ROWS, lo)
        st_ref[seg] = st * dcol + _dot1(ke, v[lo:hi], _TN) * mask
        yield
    res.append(_cat(outs))
    yield


def _ret_tile(res, q, k, v, dm, lg, rtdec, hm, st_ref, mask, nseg, seglen):
    tau = (lax.broadcasted_iota(jnp.int32, q.shape, 0) & (seglen - 1)).astype(F32)
    a = _dot1(q, _stack_heads(k, hm), _NT) * dm
    yield
    intra = _dot1(a, _stack_heads(v, hm))
    qe = q * jnp.exp((tau + 1.0) * lg)
    ke = k * jnp.exp((float(seglen) - 1.0 - tau) * lg)
    yield
    outs = []
    for seg in range(nseg):
        lo, hi = seg * seglen, (seg + 1) * seglen
        st = st_ref[seg]
        outs.append(intra[lo:hi] + _dot1(qe[lo:hi], st))
        st_ref[seg] = st * rtdec + _dot1(ke[lo:hi], v[lo:hi], _TN) * mask
    res.append(_cat(outs))
    yield


def _head_rms_gate(o, ind64, nw, gate):
    ms = _headsum(o * o, ind64) * (1.0 / HEAD_V)
    return o * lax.rsqrt(ms + NORM_EPS) * nw * _silu(gate)


def _rwkv_blocks(kt, ah, kh, rt, ap, kp, v, dec, hmasks, m256, m_scr, c_scr, c):
    nblk = kt.shape[0] // c
    blocks = range(nblk)
    r4 = lax.broadcasted_iota(jnp.int32, (c, N_HEAD * c), 0)
    s4 = lax.broadcasted_iota(jnp.int32, (c, N_HEAD * c), 1) & (c - 1)
    strict = s4 < r4
    incl = s4 <= r4
    rr = lax.broadcasted_iota(jnp.int32, (N_HEAD * c, N_HEAD * c), 0)
    cc = lax.broadcasted_iota(jnp.int32, (N_HEAD * c, N_HEAD * c), 1)
    sh = int(math.log2(c))
    bd4 = ((rr >> sh) == (cc >> sh)).astype(F32)
    eye4 = (rr == cc).astype(F32)
    eye256 = (lax.broadcasted_iota(jnp.int32, (256, 256), 0) == lax.broadcasted_iota(jnp.int32, (256, 256), 1))

    def blk(x, i):
        return x[i * c:(i + 1) * c]

    def stack(x):
        return _stack_heads(x, hmasks)

    def same(size):
        s = int(math.log2(size))
        return (rr >> s) == (cc >> s)

    kr = [jnp.concatenate([blk(kt, i), blk(rt, i)], axis=0).astype(BF16) for i in blocks]
    la = [_dot1(kr[i], stack(blk(ah, i)), _NT) for i in blocks]
    lk = [_dot1(kr[i], stack(blk(kh, i)), _NT) for i in blocks]
    l_a = [jnp.where(strict, x[:c], 0.0) for x in la]
    m_a = [jnp.where(incl, x[c:], 0.0) for x in la]
    l_k = [jnp.where(strict, x[:c], 0.0) for x in lk]
    m_k = [jnp.where(incl, x[c:], 0.0) for x in lk]
    lbd = [jnp.concatenate([x] * N_HEAD, axis=0) * bd4 for x in l_a]
    tm = [eye4 - jnp.where(same(2), x, 0.0) for x in lbd]
    size = 2
    while size < c:
        cross = same(2 * size) & jnp.logical_not(same(size))
        xm = [_dot1(tm[i], jnp.where(cross, lbd[i], 0.0)) for i in blocks]
        tm = [tm[i] - _dot1(xm[i], tm[i]) for i in blocks]
        size *= 2
    ts = []
    for x in tm:
        t = x[0:c]
        for h in range(1, N_HEAD):
            t = t + x[h * c:(h + 1) * c]
        ts.append(t)
    vs = [stack(blk(v, i)).astype(BF16) for i in blocks]
    lkv = [_dot1(l_k[i], vs[i]) for i in blocks]
    mkv = [_dot1(m_k[i], vs[i]) for i in blocks]
    kbar = [_dot1(ts[i], stack(blk(kt, i))) for i in blocks]
    u0 = [-_dot1(ts[i], stack(lkv[i])) for i in blocks]
    for i in blocks:
        m = jnp.where(eye256, dec[i * c:i * c + 1], 0.0) - _dot1(blk(ap, i), kbar[i], _TN) * m256
        m_scr[i] = m.astype(BF16)
    for i in blocks:
        c_scr[i] = _dot1(jnp.concatenate([blk(ap, i), blk(kp, i)], axis=0),
                         jnp.concatenate([u0[i], blk(v, i)], axis=0), _TN) * m256
    return kbar, u0, m_a, mkv


def _mixer_kernel(*refs, c, c_rw, nseg, seglen, nj, has_state):
    refs = list(refs)
    x_ref = refs.pop(0)
    if has_state:
        shg_ref, sgl_ref, srw_ref, srt_ref, shift_ref = refs[:5]
        refs = refs[5:]
    (cos_ref, sin_ref, p256_ref, glba_ref, mu_ref, wa2_ref, w2_ref, a2_ref, g2_ref,
     tri_ref, trib_ref, onesb_ref, ind64_ref, indgl_ref, m256_ref, mgl_ref, dm_ref, rtdec_ref, ones_ref,
     o_ref, nhg_ref, ngl_ref, nrw_ref, nrt_ref,
     st_hg, st_gl, st_rw, st_rt, carry, m_scr, c_scr) = refs
    tl = nseg * seglen
    nblk_seg = seglen // c_rw
    j = pl.program_id(1)

    def prm(i):
        return p256_ref[i:i + 1, :]

    tri = tri_ref[...]
    trib = trib_ref[...]
    onesb = onesb_ref[...]
    ones_t = ones_ref[...]
    ind64 = ind64_ref[...]
    indgl = indgl_ref[...]
    m256 = m256_ref[...]
    mgl = mgl_ref[...]
    lane = lax.broadcasted_iota(jnp.int32, (1, 256), 1)
    hmasks = [((lane >> 6) == h).astype(F32) for h in range(N_HEAD)]
    lane_k = lax.broadcasted_iota(jnp.int32, (1, GLA_KW), 1)
    hmasks_glk = [((lane_k >> 5) == h).astype(F32) for h in range(N_HEAD)]

    @pl.when(j == 0)
    def _():
        for seg in range(nseg):
            if has_state:
                st_hg[seg] = _state_in(shg_ref[seg], m256)
                st_gl[seg] = _state_in(sgl_ref[seg], mgl)
                st_rw[seg] = _state_in(srw_ref[seg], m256)
                st_rt[seg] = _state_in(srt_ref[seg], m256)
                carry[seg] = shift_ref[seg]
            else:
                st_hg[seg] = jnp.zeros((256, 256), F32)
                st_gl[seg] = jnp.zeros((GLA_KW, 256), F32)
                st_rw[seg] = jnp.zeros((256, 256), F32)
                st_rt[seg] = jnp.zeros((256, 256), F32)
                carry[seg] = jnp.zeros((1, RW_COLS), F32)

    def cols(c0, w):
        return x_ref.at[:, c0:c0 + w]

    def hgrn_work(res):
        lb = prm(P_LB)
        xq, xf = cols(C_HG_Q, 256), cols(C_HG_F, 256)

        def prep(s):
            forget = lb + (1.0 - lb) * _sigmoid(xf[s])
            return (_silu(xq[s]), 1.0 - forget) + _split3(jnp.log(jnp.maximum(forget, MIN_FORGET)))

        q, k, g1, g2_, g3_ = _by_rows(prep, tl, BF16_ROWS)
        yield
        yield from _gla_tile(res, q, k, cols(C_HG_I, 256), (g1, g2_, g3_), tri, ones_t, ind64,
                             hmasks, hmasks, st_hg, m256, c, nseg, seglen)
        o_ref[:, 0:256] = _head_rms_gate(res.pop(), ind64, prm(P_HG_NW), x_ref[:, C_HG_G:C_HG_G + 256])
        yield

    def gla_work(res):
        za = _dot1(x_ref[:, C_GL_A:C_GL_A + LANES], wa2_ref[...]) + glba_ref[0:1, :]

        def prep(s):
            z = za[s]
            return _split3((jnp.minimum(z, 0.0) - jnp.log1p(jnp.exp(-jnp.abs(z)))) / GLA_TAU)

        g3 = _by_rows(prep, tl, BF16_ROWS)
        yield
        yield from _gla_tile(res, x_ref[:, C_GL_Q:C_GL_Q + GLA_KW] * GLA_DK ** -0.5,
                             x_ref[:, C_GL_K:C_GL_K + GLA_KW], cols(C_GL_V, 256), g3,
                             tri, ones_t, indgl, hmasks_glk, hmasks, st_gl, mgl, c, nseg, seglen)
        o_ref[:, 256:512] = _head_rms_gate(res.pop(), ind64, prm(P_GL_NW), x_ref[:, C_GL_G:C_GL_G + 256])
        yield

    def ret_work(res):
        upper = (lane & (HEAD_V // 2)) != 0
        xq, xk = cols(C_RT_Q, 256), cols(C_RT_K, 256)

        def rot(x, s):
            sw = jnp.where(upper, pltpu.roll(x, HEAD_V // 2, 1), pltpu.roll(x, 256 - HEAD_V // 2, 1))
            return x * cos_ref[s, :] + sw * sin_ref[s, :]

        q_r, k_r = _by_rows(lambda s: (rot(xq[s], s), rot(xk[s], s) * HEAD_V ** -0.5), tl, SUBLANES)
        yield
        yield from _ret_tile(res, q_r, k_r, x_ref[:, C_RT_V:C_RT_V + 256], dm_ref[...], prm(P_RT_LG),
                             rtdec_ref[...], hmasks, st_rt, m256, nseg, seglen)
        ort = res.pop()
        ms = _headsum(ort * ort, ind64) * (1.0 / HEAD_V)
        o_ref[:, 768:1024] = ort * lax.rsqrt(ms + NORM_EPS) * _silu(x_ref[:, C_RT_G:C_RT_G + 256])
        yield

    work = [ret_work([]), hgrn_work([]), gla_work([])]

    def advance(n):
        for _ in range(n):
            while work:
                try:
                    next(work[0])
                    break
                except StopIteration:
                    work.pop(0)

    xrw = cols(C_RW, RW_COLS)
    first = lax.broadcasted_iota(jnp.int32, (SUBLANES, RW_COLS), 0) == 0
    mu = mu_ref[...]

    def mix_rows(s):
        rw = xrw[s]
        if s.start % seglen == 0:
            prev0 = carry[s.start // seglen]
        else:
            prev0 = xrw[s.start - 1:s.start]
        prev = jnp.where(first, jnp.broadcast_to(prev0, (SUBLANES, RW_COLS)), pltpu.roll(rw, 1, 0))
        mix = rw + (prev - rw) * mu
        k0 = pltpu.roll(mix[:, 256:640], 320, 1)[:, 0:256]
        v_ = pltpu.roll(mix[:, 512:896], 320, 1)[:, 0:256]
        kk0 = k0 * prm(P_KK)
        return (mix[:, 0:256], k0, v_, jnp.tanh(mix[:, 256:384]), mix[:, 768:896],
                _sigmoid(mix[:, 896:1024]), kk0, kk0 * kk0)

    r, k0, v, t_w, s_a, s_g, kk0, kk0sq = _by_rows(mix_rows, tl, SUBLANES)
    for seg in range(nseg):
        hi = (seg + 1) * seglen
        carry[seg] = xrw[hi - 1:hi]
    lw_pre = _dot1(t_w, w2_ref[...])
    a_pre = _dot1(s_a, a2_ref[...])
    g = _dot1(s_g, g2_ref[...])
    n2 = _headsum(kk0sq, ind64)

    def gates(s):
        lw = -math.exp(-0.5) * _sigmoid(prm(P_W0) + lw_pre[s])
        a = _sigmoid(prm(P_A0) + a_pre[s])
        kk = kk0[s] / jnp.maximum(jnp.sqrt(n2[s]), 1e-12)
        k = k0[s] * (1.0 + (a - 1.0) * prm(P_KA))
        return (lw, a * kk, kk, k, r[s] * k * prm(P_RK)) + _split3(lw)

    lw, alpha, kk, k, rkp, lw1, lw2, lw3 = _by_rows(gates, tl, BF16_ROWS)
    bw = _dotsr(trib, (lw1, lw2, lw3))
    bc = _dotsr(onesb, (lw1, lw2, lw3))

    def scaled(s):
        bws, bcs = bw[s], bc[s]
        e_neg = jnp.exp(-bws)
        e_rem = jnp.exp(bcs - bws)
        return (kk[s] * jnp.exp(bws - lw[s]), alpha[s] * e_neg, k[s] * e_neg, r[s] * jnp.exp(bws),
                alpha[s] * e_rem, k[s] * e_rem, jnp.exp(bcs))

    rw_kt, rw_ah, rw_kh, rw_rt, rw_ap, rw_kp, rw_dec = _by_rows(scaled, tl, SUBLANES)
    bonus = _headsum(rkp, ind64) * v

    kbar, u0, m_a, mkv = _rwkv_blocks(rw_kt, rw_ah, rw_kh, rw_rt, rw_ap, rw_kp, v, rw_dec,
                                      hmasks, m256, m_scr, c_scr, c_rw)

    n_units = 3 * (tl // (4 * c) + 6) + (6 * (nblk_seg // 3 + 1) if seglen > c else 0)
    per_step = -(-n_units // nblk_seg)
    sts = [st_rw[seg] for seg in range(nseg)]
    outs = [[None] * nblk_seg for _ in range(nseg)]
    for i in range(nblk_seg):
        for seg in range(nseg):
            blk = seg * nblk_seg + i
            sth, stl = _split(sts[seg])
            kr = jnp.concatenate([kbar[blk], rw_rt[blk * c_rw:(blk + 1) * c_rw]], axis=0).astype(BF16)
            mb = m_scr[blk]
            res = _dg(jnp.concatenate([mb, kr], axis=0), sth, _NN)
            xr = res[256:]
            u = u0[blk] - xr[:c_rw]
            outs[seg][i] = xr[c_rw:] + _dot1(m_a[blk], _stack_heads(u, hmasks)) + mkv[blk]
            sts[seg] = (res[:256] + _dg(mb, stl, _NN)) + c_scr[blk]
        advance(per_step)
    while work:
        advance(1)
    for seg in range(nseg):
        st_rw[seg] = sts[seg]
    orw = _cat([o for seg_outs in outs for o in seg_outs])
    mean = _headsum(orw, ind64) * (1.0 / HEAD_V)
    xc = orw - mean
    var = _headsum(xc * xc, ind64) * (1.0 / HEAD_V)
    o_ref[:, 512:768] = (xc * lax.rsqrt(var + RW_LN_EPS) * prm(P_LNW) + prm(P_LNB) + bonus) * g

    @pl.when(j == nj - 1)
    def _():
        for seg in range(nseg):
            nhg_ref[seg] = _state_out(st_hg[seg])
            ngl_ref[seg] = _state_out(st_gl[seg])
            nrw_ref[seg] = _state_out(st_rw[seg])
            nrt_ref[seg] = _state_out(st_rt[seg])


def _mixer_constants(nseg, seglen, c_rw, log_gamma):
    tl = nseg * seglen
    t = np.arange(tl)
    same = (t[:, None] // c_rw) == (t[None, :] // c_rw)
    same_seg = (t[:, None] // seglen) == (t[None, :] // seglen)
    causal = (t[None, :] <= t[:, None]) & same_seg
    tri = causal.astype(np.float32)
    trib = (same & causal).astype(np.float32)
    onesb = same.astype(np.float32)
    kv = np.arange(256)
    ind64 = ((kv[:, None] // HEAD_V) == (kv[None, :] // HEAD_V)).astype(np.float32)
    kg = np.arange(GLA_KW)
    indgl = ((kg[:, None] // GLA_DK) == (kv[None, :] // HEAD_V)).astype(np.float32)
    diff = jnp.asarray((t[:, None] - t[None, :]).astype(np.float32))
    dm = jnp.concatenate([jnp.where(jnp.asarray(causal), jnp.exp(diff * log_gamma[h]), 0.0)
                          for h in range(N_HEAD)], axis=1)
    rtdec = jnp.broadcast_to(jnp.exp(float(seglen) * jnp.repeat(log_gamma, HEAD_V))[:, None], (256, 256))
    ones_t = np.ones((tl, 256), np.float32)
    return (jnp.asarray(tri, BF16), jnp.asarray(trib, BF16), jnp.asarray(onesb, BF16),
            jnp.asarray(ind64, BF16), jnp.asarray(indgl, BF16), jnp.asarray(ind64, F32),
            jnp.asarray(indgl, F32), dm, rtdec, jnp.asarray(ones_t, BF16))


def _mixer(proj, states, shift, cosf, sins, lp, c, c_rw, nseg, seglen):
    ng, rows, _ = proj.shape
    tl = nseg * seglen
    nj = rows // tl
    has_state = states is not None
    consts = _mixer_constants(nseg, seglen, c_rw, lp['log_gamma'])

    def seq_spec(w):
        return pl.BlockSpec((None, tl, w), lambda g, j: (g, j, 0))

    def state_spec(hdk):
        return pl.BlockSpec((nseg, hdk, HEAD_V), lambda g, j: (g, 0, 0))

    def const_spec(a):
        nd = a.ndim
        return pl.BlockSpec(a.shape, lambda g, j: (0,) * nd)

    pos_spec = pl.BlockSpec((tl, 256), lambda g, j: (j, 0))
    params = (lp['p256'], lp['glba'], lp['mu'], lp['wa2'], lp['w2'], lp['a2'], lp['g2'])
    state_specs = [state_spec(256), state_spec(GLA_KW), state_spec(256), state_spec(256)]
    in_specs = [seq_spec(N_MIX)]
    args = [proj]
    if has_state:
        in_specs += state_specs + [pl.BlockSpec((nseg, 1, RW_COLS), lambda g, j: (g, 0, 0))]
        args += list(states) + [shift]
    in_specs += [pos_spec, pos_spec] + [const_spec(a) for a in params] + [const_spec(a) for a in consts]
    args += [cosf, sins, *params, *consts]
    out_specs = [seq_spec(N_BRANCH * BRANCH_W)] + state_specs
    nseq = ng * nseg
    out_shape = [jax.ShapeDtypeStruct((ng, rows, N_BRANCH * BRANCH_W), F32),
                 jax.ShapeDtypeStruct((nseq, 256, HEAD_V), F32), jax.ShapeDtypeStruct((nseq, GLA_KW, HEAD_V), F32),
                 jax.ShapeDtypeStruct((nseq, 256, HEAD_V), F32), jax.ShapeDtypeStruct((nseq, 256, HEAD_V), F32)]
    nblk = tl // c_rw
    scratch = [pltpu.VMEM((nseg, 256, 256), F32), pltpu.VMEM((nseg, GLA_KW, 256), F32),
               pltpu.VMEM((nseg, 256, 256), F32), pltpu.VMEM((nseg, 256, 256), F32),
               pltpu.VMEM((nseg, 1, RW_COLS), F32),
               pltpu.VMEM((nblk, 256, 256), BF16), pltpu.VMEM((nblk, 256, 256), F32)]
    return pl.pallas_call(
        functools.partial(_mixer_kernel, c=c, c_rw=c_rw, nseg=nseg, seglen=seglen, nj=nj,
                          has_state=has_state),
        grid=(ng, nj),
        in_specs=in_specs,
        out_specs=out_specs,
        out_shape=out_shape,
        scratch_shapes=scratch,
        compiler_params=pltpu.CompilerParams(dimension_semantics=("arbitrary", "arbitrary"),
                                             vmem_limit_bytes=VMEM_LIMIT_BYTES),
        name="mixer",
    )(*args)


def _tile_heads(v):
    return jnp.tile(v, N_HEAD)


def _layer_params(l, lower_bounds, w_in, hg_norm_w, gla_wa2, gla_ba, gla_norm_w, rw_mu, rw_w0, rw_w2,
                  rw_a0, rw_a2, rw_g2, rw_kk, rw_ka, rw_rk, rw_ln_w, rw_ln_b, w_branch, w_out,
                  w_ffn_in, w_ffn_out):
    wmix, wg = _wprep(w_in, l)
    log_gamma = jnp.log1p(-jnp.exp2(-5.0 - jnp.arange(N_HEAD, dtype=F32)))
    rows = [lower_bounds[l], _tile_heads(hg_norm_w[l]), _tile_heads(gla_norm_w[l]), rw_w0[l], rw_a0[l],
            rw_kk[l], rw_ka[l], rw_rk[l], rw_ln_w[l], rw_ln_b[l], jnp.repeat(log_gamma, HEAD_V)]
    p256 = jnp.concatenate([jnp.stack(rows), jnp.zeros((16 - len(rows), 256), F32)], axis=0)
    zeros64 = jnp.zeros((64, 256), F32)
    return {
        'wmix': wmix,
        'log_gamma': log_gamma,
        'wg': wg,
        'p256': p256,
        'glba': jnp.concatenate([gla_ba[l][None, :], jnp.zeros((7, GLA_KW), F32)], axis=0),
        'mu': rw_mu[l][None, :],
        'wa2': jnp.concatenate([gla_wa2[l], jnp.zeros((LANES - GLA_RANK, GLA_KW), F32)], axis=0).astype(BF16),
        'w2': jnp.concatenate([rw_w2[l], zeros64], axis=0).astype(BF16),
        'a2': jnp.concatenate([zeros64, rw_a2[l]], axis=0).astype(BF16),
        'g2': rw_g2[l].astype(BF16),
        'wb': w_branch[l].astype(BF16),
        'wo': w_out[l].astype(BF16),
        'wi': w_ffn_in[l].astype(BF16),
        'wfo': w_ffn_out[l].astype(BF16),
    }


def _rotary_tables(pos):
    half = HEAD_V // 2
    inv = ROPE_BASE ** (-jnp.arange(half, dtype=F32) / half)
    ang = pos[:, None] * inv[None, :]
    cos, sin = jnp.cos(ang), jnp.sin(ang)
    cosf = jnp.tile(jnp.concatenate([cos, cos], axis=1), (1, N_HEAD))
    sins = jnp.tile(jnp.concatenate([-sin, sin], axis=1), (1, N_HEAD))
    return cosf, sins


def _group_layer(x, states, shift, tables, lp, attn_nw, ffn_nw, final_nw, final, c, c_rw, nseg, seglen):
    nb, length, _ = x.shape
    xf = x.reshape(nb * length, D_MODEL)
    proj = _inproj(xf, attn_nw, lp['wmix'])
    flat_states = None
    if states is not None:
        flat_states = [s.reshape(nb, s.shape[1] * s.shape[2], HEAD_V) for s in states]
        shift = shift[:, None, :]
    o, n_hg, n_gl, n_rw, n_rt = _mixer(proj.reshape(nb // nseg, nseg * length, N_MIX), flat_states, shift,
                                       tables[0], tables[1], lp, c, c_rw, nseg, seglen)
    x1 = _merge(xf, o.reshape(nb * length, N_BRANCH * BRANCH_W), attn_nw, lp['wg'], lp['wb'], lp['wo'])
    x2 = _ffn(x1, ffn_nw, lp['wi'], lp['wfo'], final_nw, final)
    new_shift = proj.reshape(nb, length, N_MIX)[:, length - 1, C_RW:C_RW + RW_COLS]
    new_states = (n_hg.reshape(nb, N_HEAD, HEAD_V, HEAD_V), n_gl.reshape(nb, N_HEAD, GLA_DK, HEAD_V),
                  n_rw.reshape(nb, N_HEAD, HEAD_V, HEAD_V), new_shift,
                  n_rt.reshape(nb, N_HEAD, HEAD_V, HEAD_V))
    return x2.reshape(nb, length, D_MODEL), new_states


def kernel(x_prompt, x_sample, state_hgrn, state_gla, state_rwkv, state_rwkv_shift, state_ret,
           attn_norm_w, w_in, hg_lb_logits, hg_norm_w, gla_wa2, gla_ba, gla_norm_w,
           rw_mu, rw_w0, rw_w2, rw_a0, rw_a2, rw_g2, rw_kk, rw_ka, rw_rk, rw_ln_w, rw_ln_b,
           w_branch, w_out, ffn_norm_w, w_ffn_in, w_ffn_out, final_norm_w):
    depth = w_in.shape[0]
    lb_p = jax.nn.softmax(hg_lb_logits.astype(F32), axis=0)
    lower_bounds = jnp.cumsum(lb_p, axis=0) - lb_p[0:1]

    bp, lp_len, _ = x_prompt.shape
    bs, ls_len, _ = x_sample.shape
    past_len = 16384
    tab_p = _rotary_tables(jnp.arange(lp_len, dtype=F32))
    tab_s = _rotary_tables(float(past_len) + jnp.arange(ls_len, dtype=F32))
    tab_s = tuple(jnp.tile(t, (SAMPLE_SEQS_PER_TILE, 1)) for t in tab_s)
    final_nw = final_norm_w[None, :]

    xp, xs = x_prompt, x_sample
    p_states, s_states = [], []
    for l in range(depth):
        lp = _layer_params(l, lower_bounds, w_in, hg_norm_w, gla_wa2, gla_ba, gla_norm_w, rw_mu, rw_w0,
                           rw_w2, rw_a0, rw_a2, rw_g2, rw_kk, rw_ka, rw_rk, rw_ln_w, rw_ln_b, w_branch,
                           w_out, w_ffn_in, w_ffn_out)
        final = l == depth - 1
        anw, fnw = attn_norm_w[l][None, :], ffn_norm_w[l][None, :]
        xp, sp = _group_layer(xp, None, None, tab_p, lp, anw, fnw, final_nw, final,
                              c=8, c_rw=32, nseg=1, seglen=PROMPT_TILE)
        s_in = (state_hgrn[l], state_gla[l], state_rwkv[l], state_ret[l])
        xs, ss = _group_layer(xs, s_in, state_rwkv_shift[l], tab_s, lp, anw, fnw, final_nw, final,
                              c=ls_len, c_rw=ls_len, nseg=SAMPLE_SEQS_PER_TILE, seglen=ls_len)
        p_states.append(sp)
        s_states.append(ss)

    def stk(states, i):
        return jnp.stack([s[i] for s in states])

    return (xp, xs,
            stk(p_states, 0), stk(p_states, 1), stk(p_states, 2), stk(p_states, 3), stk(p_states, 4),
            stk(s_states, 0), stk(s_states, 1), stk(s_states, 2), stk(s_states, 3), stk(s_states, 4))
```

```python
import functools
import math

import jax
import jax.numpy as jnp
import numpy as np
from jax import lax
from jax.experimental import pallas as pl
from jax.experimental.pallas import tpu as pltpu

F32 = jnp.float32
BF16 = jnp.bfloat16

D_MODEL = 1024
N_HEAD = 4
HEAD_V = 64
BRANCH_W = N_HEAD * HEAD_V
N_BRANCH = 4
GLA_DK = 32
GLA_KW = N_HEAD * GLA_DK
GLA_RANK = 16
GLA_TAU = 16.0
RW_COLS = 1024
NORM_EPS = 1e-6
RW_LN_EPS = 64e-5
MIN_FORGET = 1e-30
ROPE_BASE = 10000.0
LOG2E = 1.4426950408889634
D_FF = 2816
N_MIX = 3968
PROMPT_TILE = 128
SAMPLE_SEQS_PER_TILE = 8
N_GATE = N_BRANCH * D_MODEL

VMEM_LIMIT_BYTES = 56 * 1024 * 1024
LANES = 128
SUBLANES = 8
BF16_ROWS = 16

C_HG_Q, C_HG_F, C_HG_I, C_HG_G = 0, 256, 512, 768
C_GL_Q, C_GL_K, C_GL_V, C_GL_A, C_GL_G = 1024, 1152, 1280, 1536, 1664
C_RW = 1920
C_RT_Q, C_RT_K, C_RT_V, C_RT_G = 2944, 3200, 3456, 3712

P_LB, P_HG_NW, P_GL_NW, P_W0, P_A0, P_KK, P_KA, P_RK, P_LNW, P_LNB, P_RT_LG = range(11)

_NN = (((1,), (0,)), ((), ()))
_NT = (((1,), (1,)), ((), ()))
_TN = (((0,), (0,)), ((), ()))


def _dg(a, b, dn):
    return lax.dot_general(a, b, dn, preferred_element_type=F32)


def _split(x):
    hi = x.astype(BF16)
    lo = (x - hi.astype(F32)).astype(BF16)
    return hi, lo


def _split3(x):
    x1 = x.astype(BF16)
    r = x - x1.astype(F32)
    x2 = r.astype(BF16)
    x3 = (r - x2.astype(F32)).astype(BF16)
    return x1, x2, x3


def _dot1(a, b, dn=_NN):
    return _dg(a.astype(BF16), b.astype(BF16), dn)


def _dots(pieces, b_exact, dn=_NN):
    out = _dg(pieces[0], b_exact, dn)
    for p in pieces[1:]:
        out = out + _dg(p, b_exact, dn)
    return out


def _dotsr(a_exact, pieces, dn=_NN):
    out = _dg(a_exact, pieces[0], dn)
    for p in pieces[1:]:
        out = out + _dg(a_exact, p, dn)
    return out


def _headsum(x, ind):
    return _dg(x.astype(BF16), ind, _NN)


def _sigmoid(x):
    return jax.nn.sigmoid(x)


def _silu(x):
    return x * jax.nn.sigmoid(x)


def _rms(x, w):
    return x * lax.rsqrt(jnp.mean(x * x, axis=-1, keepdims=True) + NORM_EPS) * w


def _cat(parts, axis=0):
    return parts[0] if len(parts) == 1 else jnp.concatenate(parts, axis=axis)


def _by_rows(fn, rows, chunk, start=0):
    outs = [fn(slice(i, min(i + chunk, rows))) for i in range(start, rows, chunk)]
    if isinstance(outs[0], tuple):
        return tuple(_cat([o[j] for o in outs]) for j in range(len(outs[0])))
    return _cat(outs)


def _inproj_kernel(x_ref, nw_ref, w_ref, o_ref):
    h = _rms(x_ref[...], nw_ref[...])
    o_ref[...] = jnp.dot(h.astype(BF16), w_ref[...], preferred_element_type=F32)


def _merge_kernel(x_ref, o_ref, nw_ref, wg_ref, wb_ref, wo_ref, out_ref):
    x = x_ref[...]
    h = _rms(x, nw_ref[...]).astype(BF16)
    merged = None
    for b in range(N_BRANCH):
        gl = jnp.dot(h, wg_ref[:, b * D_MODEL:(b + 1) * D_MODEL], preferred_element_type=F32)
        ob = o_ref[:, b * BRANCH_W:(b + 1) * BRANCH_W].astype(BF16)
        up = jnp.dot(ob, wb_ref[b], preferred_element_type=F32)
        t = _sigmoid(gl) * up
        merged = t if merged is None else merged + t
    y = jnp.dot(merged.astype(BF16), wo_ref[...], preferred_element_type=F32)
    out_ref[...] = x + y


FFN_CHUNK = 1408


def _ffn_kernel(x_ref, nw_ref, wi_ref, wo_ref, fw_ref, out_ref, *, final):
    x = x_ref[...]
    h = _rms(x, nw_ref[...]).astype(BF16)
    acc = None
    for j in range(D_FF // FFN_CHUNK):
        lo = j * FFN_CHUNK
        g = jnp.dot(h, wi_ref[:, lo:lo + FFN_CHUNK], preferred_element_type=F32)
        u = jnp.dot(h, wi_ref[:, D_FF + lo:D_FF + lo + FFN_CHUNK], preferred_element_type=F32)
        a = (_silu(g) * u).astype(BF16)
        t = jnp.dot(a, wo_ref[lo:lo + FFN_CHUNK, :], preferred_element_type=F32)
        acc = t if acc is None else acc + t
    x2 = x + acc
    if final:
        x2 = _rms(x2, fw_ref[...])
    out_ref[...] = x2


def _dense_params():
    return pltpu.CompilerParams(dimension_semantics=("arbitrary",), vmem_limit_bytes=VMEM_LIMIT_BYTES)


def _const_spec(shape):
    nd = len(shape)
    return pl.BlockSpec(shape, lambda i: (0,) * nd)


def _token_tile(t):
    return 512 if t % 512 == 0 else t


N_IN = 7952
GATE_COL0 = 3856
WPREP_ROWS = 128


def _wprep_kernel(w_ref, wmix_ref, wg_ref):
    a0 = C_GL_A
    lane = lax.broadcasted_iota(jnp.int32, (1, LANES), 1)
    wmix_ref[:, 0:a0] = w_ref[:, 0:a0].astype(BF16)
    wmix_ref[:, a0:a0 + LANES] = jnp.where(lane < GLA_RANK, w_ref[:, a0:a0 + LANES], 0.0).astype(BF16)
    wmix_ref[:, a0 + LANES:N_MIX] = w_ref[:, a0 + GLA_RANK:GATE_COL0].astype(BF16)
    wg_ref[...] = w_ref[:, GATE_COL0:N_IN].astype(BF16)


def _wprep(w_in, l):
    return pl.pallas_call(
        _wprep_kernel,
        grid=(D_MODEL // WPREP_ROWS,),
        in_specs=[pl.BlockSpec((WPREP_ROWS, N_IN), lambda i: (l * (D_MODEL // WPREP_ROWS) + i, 0))],
        out_specs=[pl.BlockSpec((WPREP_ROWS, N_MIX), lambda i: (i, 0)),
                   pl.BlockSpec((WPREP_ROWS, N_GATE), lambda i: (i, 0))],
        out_shape=[jax.ShapeDtypeStruct((D_MODEL, N_MIX), BF16), jax.ShapeDtypeStruct((D_MODEL, N_GATE), BF16)],
        compiler_params=_dense_params(),
        name="wprep",
    )(w_in.reshape(-1, N_IN))


def _inproj(x, nw, wmix):
    t = x.shape[0]
    tm = _token_tile(t)
    return pl.pallas_call(
        _inproj_kernel,
        grid=(t // tm,),
        in_specs=[pl.BlockSpec((tm, D_MODEL), lambda i: (i, 0)),
                  _const_spec((1, D_MODEL)), _const_spec((D_MODEL, N_MIX))],
        out_specs=pl.BlockSpec((tm, N_MIX), lambda i: (i, 0)),
        out_shape=jax.ShapeDtypeStruct((t, N_MIX), F32),
        compiler_params=_dense_params(),
        name="inproj",
    )(x, nw, wmix)


def _merge(x, o, nw, wg, wb, wo, l):
    t = x.shape[0]
    tm = _token_tile(t)
    return pl.pallas_call(
        _merge_kernel,
        grid=(t // tm,),
        in_specs=[pl.BlockSpec((tm, D_MODEL), lambda i: (i, 0)),
                  pl.BlockSpec((tm, N_BRANCH * BRANCH_W), lambda i: (i, 0)),
                  _const_spec((1, D_MODEL)), _const_spec((D_MODEL, N_GATE)),
                  pl.BlockSpec((None, N_BRANCH, BRANCH_W, D_MODEL), lambda i: (l, 0, 0, 0)),
                  pl.BlockSpec((None, D_MODEL, D_MODEL), lambda i: (l, 0, 0))],
        out_specs=pl.BlockSpec((tm, D_MODEL), lambda i: (i, 0)),
        out_shape=jax.ShapeDtypeStruct((t, D_MODEL), F32),
        compiler_params=_dense_params(),
        name="merge",
    )(x, o, nw, wg, wb, wo)


def _ffn(x, nw, wi, wo, fw, final, l):
    t = x.shape[0]
    tm = _token_tile(t)
    return pl.pallas_call(
        functools.partial(_ffn_kernel, final=final),
        grid=(t // tm,),
        in_specs=[pl.BlockSpec((tm, D_MODEL), lambda i: (i, 0)),
                  _const_spec((1, D_MODEL)), pl.BlockSpec((None, D_MODEL, 2 * D_FF), lambda i: (l, 0, 0)),
                  pl.BlockSpec((None, D_FF, D_MODEL), lambda i: (l, 0, 0)), _const_spec((1, D_MODEL))],
        out_specs=pl.BlockSpec((tm, D_MODEL), lambda i: (i, 0)),
        out_shape=jax.ShapeDtypeStruct((t, D_MODEL), F32),
        compiler_params=_dense_params(),
        name="ffn",
    )(x, nw, wi, wo, fw)


def _stack_heads(x, hm):
    return jnp.concatenate([x * hm[h] for h in range(N_HEAD)], axis=0)


def _state_in(s, mask):
    return jnp.concatenate([s] * N_HEAD, axis=1) * mask


def _state_out(st):
    return (st[:, 0:HEAD_V] + st[:, HEAD_V:2 * HEAD_V]) + (st[:, 2 * HEAD_V:3 * HEAD_V] + st[:, 3 * HEAD_V:])


def _gla_tile(res, q, k, v, g3, tri, ones_t, ind, hm_k, hm_v, st_ref, mask, c, nseg, seglen):
    tl = q.shape[0]
    nblk = seglen // c
    b = _dotsr(tri, g3)
    yield
    rowi = lax.broadcasted_iota(jnp.int32, (c, q.shape[1]), 0)
    group = 4 * c
    accs, qes = [], []
    for g0 in range(0, tl, group):
        pieces, vrots = [], []
        for r0 in range(g0, g0 + group, c):
            qb, kb, vb = q[r0:r0 + c], k[r0:r0 + c], v[r0:r0 + c]
            bb = b[r0:r0 + c]
            b2 = bb * LOG2E
            qes.append(qb * jnp.exp2(b2))
            pieces.append(qb * kb)
            vr = [vb]
            for d in range(1, c):
                e = jnp.exp2(b2 - pltpu.roll(b2, d, 0))
                pieces.append(jnp.where(rowi >= d, qb * pltpu.roll(kb, d, 0) * e, 0.0))
                vr.append(pltpu.roll(vb, d, 0))
            vrots.append(vr)
        a = _dot1(jnp.concatenate(pieces, axis=0), ind)
        for i, vr in enumerate(vrots):
            acc = a[i * c * c:i * c * c + c] * vr[0]
            for d in range(1, c):
                acc = acc + a[(i * c + d) * c:(i * c + d + 1) * c] * vr[d]
            accs.append(acc)
        yield
    acc = _cat(accs)
    qe = _cat(qes)
    if nblk > 1:
        scores = []
        for seg in range(nseg):
            base = seg * seglen
            for i in range(1, nblk):
                lo = base + i * c
                r = b[lo - 1:lo, :]
                qi = q[lo:lo + c] * jnp.exp(b[lo:lo + c] - r)
                ki = _by_rows(lambda s: k[s] * jnp.exp(r - b[s]), lo, 2 * BF16_ROWS, base)
                scores.append(_dot1(_stack_heads(qi, hm_k), ki, _NT))
                if i % 3 == 0:
                    yield
        parts = []
        n = 0
        for seg in range(nseg):
            base = seg * seglen
            parts.append(jnp.zeros((c, N_HEAD * HEAD_V), F32))
            for i in range(1, nblk):
                r4 = _dot1(scores[n], v[base:base + i * c])
                n += 1
                oi = r4[0:c] * hm_v[0]
                for h in range(1, N_HEAD):
                    oi = oi + r4[h * c:(h + 1) * c] * hm_v[h]
                parts.append(oi)
                if i % 3 == 0:
                    yield
        acc = acc + jnp.concatenate(parts, axis=0)
    outs = []
    for seg in range(nseg):
        lo, hi = seg * seglen, (seg + 1) * seglen
        blast = b[hi - 1:hi, :]
        dcol = jnp.exp(_dots([p[lo:hi] for p in g3], ones_t[lo:hi], _TN))
        st = st_ref[seg]
        outs.append(acc[lo:hi] + _dot1(qe[lo:hi], st))
        ke = _by_rows(lambda s: k[s] * jnp.exp(blast - b[s]), hi, 2 * BF16_ROWS, lo)
        st_ref[seg] = st * dcol + _dot1(ke, v[lo:hi], _TN) * mask
        yield
    res.append(_cat(outs))
    yield


def _ret_tile(res, q, k, v, dm, lg, rtdec, hm, st_ref, mask, nseg, seglen):
    tau = (lax.broadcasted_iota(jnp.int32, q.shape, 0) & (seglen - 1)).astype(F32)
    a = _dot1(q, _stack_heads(k, hm), _NT) * dm
    yield
    intra = _dot1(a, _stack_heads(v, hm))
    qe = q * jnp.exp((tau + 1.0) * lg)
    ke = k * jnp.exp((float(seglen) - 1.0 - tau) * lg)
    yield
    outs = []
    for seg in range(nseg):
        lo, hi = seg * seglen, (seg + 1) * seglen
        st = st_ref[seg]
        outs.append(intra[lo:hi] + _dot1(qe[lo:hi], st))
        st_ref[seg] = st * rtdec + _dot1(ke[lo:hi], v[lo:hi], _TN) * mask
    res.append(_cat(outs))
    yield


def _head_rms_gate(o, ind64, nw, gate):
    ms = _headsum(o * o, ind64) * (1.0 / HEAD_V)
    return o * lax.rsqrt(ms + NORM_EPS) * nw * _silu(gate)


def _rwkv_blocks(kt, ah, kh, rt, ap, kp, v, dec, hmasks, m256, m_scr, c_scr, c, tick):
    nblk = kt.shape[0] // c
    blocks = range(nblk)
    r4 = lax.broadcasted_iota(jnp.int32, (c, N_HEAD * c), 0)
    s4 = lax.broadcasted_iota(jnp.int32, (c, N_HEAD * c), 1) & (c - 1)
    strict = s4 < r4
    incl = s4 <= r4
    rr = lax.broadcasted_iota(jnp.int32, (N_HEAD * c, N_HEAD * c), 0)
    cc = lax.broadcasted_iota(jnp.int32, (N_HEAD * c, N_HEAD * c), 1)
    sh = int(math.log2(c))
    bd4 = ((rr >> sh) == (cc >> sh)).astype(F32)
    eye4 = (rr == cc).astype(F32)
    eye256 = (lax.broadcasted_iota(jnp.int32, (256, 256), 0) == lax.broadcasted_iota(jnp.int32, (256, 256), 1))

    def blk(x, i):
        return x[i * c:(i + 1) * c]

    def stack(x):
        return _stack_heads(x, hmasks)

    def same(size):
        s = int(math.log2(size))
        return (rr >> s) == (cc >> s)

    kr = [jnp.concatenate([blk(kt, i), blk(rt, i)], axis=0).astype(BF16) for i in blocks]
    la = [_dot1(kr[i], stack(blk(ah, i)), _NT) for i in blocks]
    lk = [_dot1(kr[i], stack(blk(kh, i)), _NT) for i in blocks]
    tick()
    l_a = [jnp.where(strict, x[:c], 0.0) for x in la]
    m_a = [jnp.where(incl, x[c:], 0.0) for x in la]
    l_k = [jnp.where(strict, x[:c], 0.0) for x in lk]
    m_k = [jnp.where(incl, x[c:], 0.0) for x in lk]
    lbd = [jnp.concatenate([x] * N_HEAD, axis=0) * bd4 for x in l_a]
    tm = [eye4 - jnp.where(same(2), x, 0.0) for x in lbd]
    size = 2
    while size < c:
        cross = same(2 * size) & jnp.logical_not(same(size))
        xm = [_dot1(tm[i], jnp.where(cross, lbd[i], 0.0)) for i in blocks]
        tick()
        tm = [tm[i] - _dot1(xm[i], tm[i]) for i in blocks]
        tick()
        size *= 2
    ts = []
    for x in tm:
        t = x[0:c]
        for h in range(1, N_HEAD):
            t = t + x[h * c:(h + 1) * c]
        ts.append(t)
    vs = [stack(blk(v, i)).astype(BF16) for i in blocks]
    lkv = [_dot1(l_k[i], vs[i]) for i in blocks]
    mkv = [_dot1(m_k[i], vs[i]) for i in blocks]
    tick()
    kbar = [_dot1(ts[i], stack(blk(kt, i))) for i in blocks]
    u0 = [-_dot1(ts[i], stack(lkv[i])) for i in blocks]
    tick()
    for i in blocks:
        m = jnp.where(eye256, dec[i * c:i * c + 1], 0.0) - _dot1(blk(ap, i), kbar[i], _TN) * m256
        m_scr[i] = m.astype(BF16)
    tick()
    for i in blocks:
        c_scr[i] = _dot1(jnp.concatenate([blk(ap, i), blk(kp, i)], axis=0),
                         jnp.concatenate([u0[i], blk(v, i)], axis=0), _TN) * m256
    return kbar, u0, m_a, mkv


N_MIXER_SHARED_INPUTS = 19


def _mixer_kernel(*refs, c, c_rw, nseg, seglen, nj, has_state, n_alias):
    refs = list(refs)
    x_ref = refs.pop(0)
    if has_state:
        shg_ref, sgl_ref, srw_ref, srt_ref, shift_ref = refs[:5]
        refs = refs[5:]
    (cos_ref, sin_ref, p256_ref, glba_ref, mu_ref, wa2_ref, w2_ref, a2_ref, g2_ref,
     tri_ref, trib_ref, onesb_ref, ind64_ref, indgl_ref, m256_ref, mgl_ref, dm_ref, rtdec_ref,
     ones_ref) = refs[:N_MIXER_SHARED_INPUTS]
    (o_ref, nhg_ref, ngl_ref, nrw_ref, nrt_ref,
     st_hg, st_gl, st_rw, st_rt, carry, m_scr, c_scr) = refs[N_MIXER_SHARED_INPUTS + n_alias:]
    tl = nseg * seglen
    nblk_seg = seglen // c_rw
    j = pl.program_id(1)

    def prm(i):
        return p256_ref[i:i + 1, :]

    tri = tri_ref[...]
    trib = trib_ref[...]
    onesb = onesb_ref[...]
    ones_t = ones_ref[...]
    ind64 = ind64_ref[...]
    indgl = indgl_ref[...]
    m256 = m256_ref[...]
    mgl = mgl_ref[...]
    lane = lax.broadcasted_iota(jnp.int32, (1, 256), 1)
    hmasks = [((lane >> 6) == h).astype(F32) for h in range(N_HEAD)]
    lane_k = lax.broadcasted_iota(jnp.int32, (1, GLA_KW), 1)
    hmasks_glk = [((lane_k >> 5) == h).astype(F32) for h in range(N_HEAD)]

    @pl.when(j == 0)
    def _():
        for seg in range(nseg):
            if has_state:
                st_hg[seg] = _state_in(shg_ref[seg], m256)
                st_gl[seg] = _state_in(sgl_ref[seg], mgl)
                st_rw[seg] = _state_in(srw_ref[seg], m256)
                st_rt[seg] = _state_in(srt_ref[seg], m256)
                carry[seg] = shift_ref[seg]
            else:
                st_hg[seg] = jnp.zeros((256, 256), F32)
                st_gl[seg] = jnp.zeros((GLA_KW, 256), F32)
                st_rw[seg] = jnp.zeros((256, 256), F32)
                st_rt[seg] = jnp.zeros((256, 256), F32)
                carry[seg] = jnp.zeros((1, RW_COLS), F32)

    def cols(c0, w):
        return x_ref.at[:, c0:c0 + w]

    def hgrn_work(res):
        lb = prm(P_LB)
        xq, xf = cols(C_HG_Q, 256), cols(C_HG_F, 256)

        def prep(s):
            forget = lb + (1.0 - lb) * _sigmoid(xf[s])
            return (_silu(xq[s]), 1.0 - forget) + _split3(jnp.log(jnp.maximum(forget, MIN_FORGET)))

        q, k, g1, g2_, g3_ = _by_rows(prep, tl, BF16_ROWS)
        yield
        yield from _gla_tile(res, q, k, cols(C_HG_I, 256), (g1, g2_, g3_), tri, ones_t, ind64,
                             hmasks, hmasks, st_hg, m256, c, nseg, seglen)
        o_ref[:, 0:256] = _head_rms_gate(res.pop(), ind64, prm(P_HG_NW), x_ref[:, C_HG_G:C_HG_G + 256])
        yield

    def gla_work(res):
        za = _dot1(x_ref[:, C_GL_A:C_GL_A + LANES], wa2_ref[...]) + glba_ref[0:1, :]

        def prep(s):
            z = za[s]
            return _split3((jnp.minimum(z, 0.0) - jnp.log1p(jnp.exp(-jnp.abs(z)))) / GLA_TAU)

        g3 = _by_rows(prep, tl, BF16_ROWS)
        yield
        yield from _gla_tile(res, x_ref[:, C_GL_Q:C_GL_Q + GLA_KW] * GLA_DK ** -0.5,
                             x_ref[:, C_GL_K:C_GL_K + GLA_KW], cols(C_GL_V, 256), g3,
                             tri, ones_t, indgl, hmasks_glk, hmasks, st_gl, mgl, c, nseg, seglen)
        o_ref[:, 256:512] = _head_rms_gate(res.pop(), ind64, prm(P_GL_NW), x_ref[:, C_GL_G:C_GL_G + 256])
        yield

    def ret_work(res):
        upper = (lane & (HEAD_V // 2)) != 0
        xq, xk = cols(C_RT_Q, 256), cols(C_RT_K, 256)

        def rot(x, s):
            sw = jnp.where(upper, pltpu.roll(x, HEAD_V // 2, 1), pltpu.roll(x, 256 - HEAD_V // 2, 1))
            return x * cos_ref[s, :] + sw * sin_ref[s, :]

        q_r, k_r = _by_rows(lambda s: (rot(xq[s], s), rot(xk[s], s) * HEAD_V ** -0.5), tl, SUBLANES)
        yield
        yield from _ret_tile(res, q_r, k_r, x_ref[:, C_RT_V:C_RT_V + 256], dm_ref[...], prm(P_RT_LG),
                             rtdec_ref[...], hmasks, st_rt, m256, nseg, seglen)
        ort = res.pop()
        ms = _headsum(ort * ort, ind64) * (1.0 / HEAD_V)
        o_ref[:, 768:1024] = ort * lax.rsqrt(ms + NORM_EPS) * _silu(x_ref[:, C_RT_G:C_RT_G + 256])
        yield

    work = [ret_work([]), hgrn_work([]), gla_work([])]

    def advance(n):
        for _ in range(n):
            while work:
                try:
                    next(work[0])
                    break
                except StopIteration:
                    work.pop(0)

    xrw = cols(C_RW, RW_COLS)
    first = lax.broadcasted_iota(jnp.int32, (SUBLANES, RW_COLS), 0) == 0
    mu = mu_ref[...]

    def mix_rows(s):
        rw = xrw[s]
        if s.start % seglen == 0:
            prev0 = carry[s.start // seglen]
        else:
            prev0 = xrw[s.start - 1:s.start]
        prev = jnp.where(first, jnp.broadcast_to(prev0, (SUBLANES, RW_COLS)), pltpu.roll(rw, 1, 0))
        mix = rw + (prev - rw) * mu
        k0 = pltpu.roll(mix[:, 256:640], 320, 1)[:, 0:256]
        v_ = pltpu.roll(mix[:, 512:896], 320, 1)[:, 0:256]
        kk0 = k0 * prm(P_KK)
        return (mix[:, 0:256], k0, v_, jnp.tanh(mix[:, 256:384]), mix[:, 768:896],
                _sigmoid(mix[:, 896:1024]), kk0, kk0 * kk0)

    r, k0, v, t_w, s_a, s_g, kk0, kk0sq = _by_rows(mix_rows, tl, SUBLANES)
    for seg in range(nseg):
        hi = (seg + 1) * seglen
        carry[seg] = xrw[hi - 1:hi]
    lw_pre = _dot1(t_w, w2_ref[...])
    a_pre = _dot1(s_a, a2_ref[...])
    g = _dot1(s_g, g2_ref[...])
    n2 = _headsum(kk0sq, ind64)

    def gates(s):
        lw = -math.exp(-0.5) * _sigmoid(prm(P_W0) + lw_pre[s])
        a = _sigmoid(prm(P_A0) + a_pre[s])
        kk = kk0[s] / jnp.maximum(jnp.sqrt(n2[s]), 1e-12)
        k = k0[s] * (1.0 + (a - 1.0) * prm(P_KA))
        return (lw, a * kk, kk, k, r[s] * k * prm(P_RK)) + _split3(lw)

    lw, alpha, kk, k, rkp, lw1, lw2, lw3 = _by_rows(gates, tl, BF16_ROWS)
    bw = _dotsr(trib, (lw1, lw2, lw3))
    bc = _dotsr(onesb, (lw1, lw2, lw3))

    def scaled(s):
        bws, bcs = bw[s], bc[s]
        e_neg = jnp.exp(-bws)
        e_rem = jnp.exp(bcs - bws)
        return (kk[s] * jnp.exp(bws - lw[s]), alpha[s] * e_neg, k[s] * e_neg, r[s] * jnp.exp(bws),
                alpha[s] * e_rem, k[s] * e_rem, jnp.exp(bcs))

    rw_kt, rw_ah, rw_kh, rw_rt, rw_ap, rw_kp, rw_dec = _by_rows(scaled, tl, SUBLANES)
    bonus = _headsum(rkp, ind64) * v

    kbar, u0, m_a, mkv = _rwkv_blocks(rw_kt, rw_ah, rw_kh, rw_rt, rw_ap, rw_kp, v, rw_dec,
                                      hmasks, m256, m_scr, c_scr, c_rw, lambda: advance(2))

    per_step = 4
    sts = [st_rw[seg] for seg in range(nseg)]
    outs = [[None] * nblk_seg for _ in range(nseg)]
    for i in range(nblk_seg):
        for seg in range(nseg):
            blk = seg * nblk_seg + i
            sth, stl = _split(sts[seg])
            kr = jnp.concatenate([kbar[blk], rw_rt[blk * c_rw:(blk + 1) * c_rw]], axis=0).astype(BF16)
            mb = m_scr[blk]
            res = _dg(jnp.concatenate([mb, kr], axis=0), sth, _NN)
            xr = res[256:]
            u = u0[blk] - xr[:c_rw]
            outs[seg][i] = xr[c_rw:] + _dot1(m_a[blk], _stack_heads(u, hmasks)) + mkv[blk]
            sts[seg] = (res[:256] + _dg(mb, stl, _NN)) + c_scr[blk]
        advance(per_step)
    while work:
        advance(1)
    for seg in range(nseg):
        st_rw[seg] = sts[seg]
    orw = _cat([o for seg_outs in outs for o in seg_outs])
    mean = _headsum(orw, ind64) * (1.0 / HEAD_V)
    xc = orw - mean
    var = _headsum(xc * xc, ind64) * (1.0 / HEAD_V)
    o_ref[:, 512:768] = (xc * lax.rsqrt(var + RW_LN_EPS) * prm(P_LNW) + prm(P_LNB) + bonus) * g

    @pl.when(j == nj - 1)
    def _():
        for seg in range(nseg):
            nhg_ref[seg] = _state_out(st_hg[seg])
            ngl_ref[seg] = _state_out(st_gl[seg])
            nrw_ref[seg] = _state_out(st_rw[seg])
            nrt_ref[seg] = _state_out(st_rt[seg])


def _mixer_constants(nseg, seglen, c_rw, log_gamma):
    tl = nseg * seglen
    t = np.arange(tl)
    same = (t[:, None] // c_rw) == (t[None, :] // c_rw)
    same_seg = (t[:, None] // seglen) == (t[None, :] // seglen)
    causal = (t[None, :] <= t[:, None]) & same_seg
    tri = causal.astype(np.float32)
    trib = (same & causal).astype(np.float32)
    onesb = same.astype(np.float32)
    kv = np.arange(256)
    ind64 = ((kv[:, None] // HEAD_V) == (kv[None, :] // HEAD_V)).astype(np.float32)
    kg = np.arange(GLA_KW)
    indgl = ((kg[:, None] // GLA_DK) == (kv[None, :] // HEAD_V)).astype(np.float32)
    diff = jnp.asarray((t[:, None] - t[None, :]).astype(np.float32))
    dm = jnp.concatenate([jnp.where(jnp.asarray(causal), jnp.exp(diff * log_gamma[h]), 0.0)
                          for h in range(N_HEAD)], axis=1)
    rtdec = jnp.broadcast_to(jnp.exp(float(seglen) * jnp.repeat(log_gamma, HEAD_V))[:, None], (256, 256))
    ones_t = np.ones((tl, 256), np.float32)
    return (jnp.asarray(tri, BF16), jnp.asarray(trib, BF16), jnp.asarray(onesb, BF16),
            jnp.asarray(ind64, BF16), jnp.asarray(indgl, BF16), jnp.asarray(ind64, F32),
            jnp.asarray(indgl, F32), dm, rtdec, jnp.asarray(ones_t, BF16))


def _mixer(proj, states, shift, cosf, sins, lp, c, c_rw, nseg, seglen, depth, prev_outs):
    ng, rows, _ = proj.shape
    tl = nseg * seglen
    nj = rows // tl
    has_state = states is not None
    g0 = lp['layer'] * ng
    consts = _mixer_constants(nseg, seglen, c_rw, lp['log_gamma'])

    def seq_spec(w):
        return pl.BlockSpec((None, tl, w), lambda g, j: (g, j, 0))

    def state_spec(hdk):
        return pl.BlockSpec((nseg, hdk, HEAD_V), lambda g, j: (g0 + g, 0, 0))

    def const_spec(a):
        nd = a.ndim
        return pl.BlockSpec(a.shape, lambda g, j: (0,) * nd)

    pos_spec = pl.BlockSpec((tl, 256), lambda g, j: (j, 0))
    params = (lp['p256'], lp['glba'], lp['mu'], lp['wa2'], lp['w2'], lp['a2'], lp['g2'])
    state_specs = [state_spec(256), state_spec(GLA_KW), state_spec(256), state_spec(256)]
    in_specs = [seq_spec(N_MIX)]
    args = [proj]
    if has_state:
        in_specs += state_specs + [pl.BlockSpec((nseg, 1, RW_COLS), lambda g, j: (g0 + g, 0, 0))]
        args += list(states) + [shift]
    in_specs += [pos_spec, pos_spec] + [const_spec(a) for a in params] + [const_spec(a) for a in consts]
    args += [cosf, sins, *params, *consts]
    aliases = {}
    if prev_outs is not None:
        aliases = {len(args) + i: 1 + i for i in range(len(prev_outs))}
        in_specs += [pl.BlockSpec(memory_space=pl.ANY)] * len(prev_outs)
        args += list(prev_outs)
    out_specs = [seq_spec(N_BRANCH * BRANCH_W)] + state_specs
    nseq = depth * ng * nseg
    out_shape = [jax.ShapeDtypeStruct((ng, rows, N_BRANCH * BRANCH_W), F32),
                 jax.ShapeDtypeStruct((nseq, 256, HEAD_V), F32), jax.ShapeDtypeStruct((nseq, GLA_KW, HEAD_V), F32),
                 jax.ShapeDtypeStruct((nseq, 256, HEAD_V), F32), jax.ShapeDtypeStruct((nseq, 256, HEAD_V), F32)]
    nblk = tl // c_rw
    scratch = [pltpu.VMEM((nseg, 256, 256), F32), pltpu.VMEM((nseg, GLA_KW, 256), F32),
               pltpu.VMEM((nseg, 256, 256), F32), pltpu.VMEM((nseg, 256, 256), F32),
               pltpu.VMEM((nseg, 1, RW_COLS), F32),
               pltpu.VMEM((nblk, 256, 256), BF16), pltpu.VMEM((nblk, 256, 256), F32)]
    return pl.pallas_call(
        functools.partial(_mixer_kernel, c=c, c_rw=c_rw, nseg=nseg, seglen=seglen, nj=nj,
                          has_state=has_state, n_alias=len(aliases)),
        grid=(ng, nj),
        in_specs=in_specs,
        out_specs=out_specs,
        out_shape=out_shape,
        scratch_shapes=scratch,
        input_output_aliases=aliases,
        compiler_params=pltpu.CompilerParams(dimension_semantics=("arbitrary", "arbitrary"),
                                             vmem_limit_bytes=VMEM_LIMIT_BYTES),
        name="mixer",
    )(*args)


def _tile_heads(v):
    return jnp.tile(v, N_HEAD)


def _layer_params(l, lower_bounds, w_in, hg_norm_w, gla_wa2, gla_ba, gla_norm_w, rw_mu, rw_w0, rw_w2,
                  rw_a0, rw_a2, rw_g2, rw_kk, rw_ka, rw_rk, rw_ln_w, rw_ln_b):
    wmix, wg = _wprep(w_in, l)
    log_gamma = jnp.log1p(-jnp.exp2(-5.0 - jnp.arange(N_HEAD, dtype=F32)))
    rows = [lower_bounds[l], _tile_heads(hg_norm_w[l]), _tile_heads(gla_norm_w[l]), rw_w0[l], rw_a0[l],
            rw_kk[l], rw_ka[l], rw_rk[l], rw_ln_w[l], rw_ln_b[l], jnp.repeat(log_gamma, HEAD_V)]
    p256 = jnp.concatenate([jnp.stack(rows), jnp.zeros((16 - len(rows), 256), F32)], axis=0)
    zeros64 = jnp.zeros((64, 256), F32)
    return {
        'wmix': wmix,
        'log_gamma': log_gamma,
        'wg': wg,
        'p256': p256,
        'glba': jnp.concatenate([gla_ba[l][None, :], jnp.zeros((7, GLA_KW), F32)], axis=0),
        'mu': rw_mu[l][None, :],
        'wa2': jnp.concatenate([gla_wa2[l], jnp.zeros((LANES - GLA_RANK, GLA_KW), F32)], axis=0).astype(BF16),
        'w2': jnp.concatenate([rw_w2[l], zeros64], axis=0).astype(BF16),
        'a2': jnp.concatenate([zeros64, rw_a2[l]], axis=0).astype(BF16),
        'g2': rw_g2[l].astype(BF16),
    }


def _rotary_tables(pos):
    half = HEAD_V // 2
    inv = ROPE_BASE ** (-jnp.arange(half, dtype=F32) / half)
    ang = pos[:, None] * inv[None, :]
    cos, sin = jnp.cos(ang), jnp.sin(ang)
    cosf = jnp.tile(jnp.concatenate([cos, cos], axis=1), (1, N_HEAD))
    sins = jnp.tile(jnp.concatenate([-sin, sin], axis=1), (1, N_HEAD))
    return cosf, sins


def _group_layer(x, states, shift, tables, lp, attn_nw, ffn_nw, final_nw, final, c, c_rw, nseg, seglen,
                 depth, prev_outs):
    nb, length, _ = x.shape
    xf = x.reshape(nb * length, D_MODEL)
    proj = _inproj(xf, attn_nw, lp['wmix'])
    o, *outs = _mixer(proj.reshape(nb // nseg, nseg * length, N_MIX), states, shift,
                      tables[0], tables[1], lp, c, c_rw, nseg, seglen, depth, prev_outs)
    x1 = _merge(xf, o.reshape(nb * length, N_BRANCH * BRANCH_W), attn_nw, lp['wg'], lp['wb'], lp['wo'],
                lp['layer'])
    x2 = _ffn(x1, ffn_nw, lp['wi'], lp['wfo'], final_nw, final, lp['layer'])
    new_shift = proj.reshape(nb, length, N_MIX)[:, length - 1, C_RW:C_RW + RW_COLS]
    return x2.reshape(nb, length, D_MODEL), outs, new_shift


def _states_out(outs, shifts, depth, nb):
    n_hg, n_gl, n_rw, n_rt = outs
    return (n_hg.reshape(depth, nb, N_HEAD, HEAD_V, HEAD_V), n_gl.reshape(depth, nb, N_HEAD, GLA_DK, HEAD_V),
            n_rw.reshape(depth, nb, N_HEAD, HEAD_V, HEAD_V), jnp.stack(shifts),
            n_rt.reshape(depth, nb, N_HEAD, HEAD_V, HEAD_V))


def kernel(x_prompt, x_sample, state_hgrn, state_gla, state_rwkv, state_rwkv_shift, state_ret,
           attn_norm_w, w_in, hg_lb_logits, hg_norm_w, gla_wa2, gla_ba, gla_norm_w,
           rw_mu, rw_w0, rw_w2, rw_a0, rw_a2, rw_g2, rw_kk, rw_ka, rw_rk, rw_ln_w, rw_ln_b,
           w_branch, w_out, ffn_norm_w, w_ffn_in, w_ffn_out, final_norm_w):
    depth = w_in.shape[0]
    lb_p = jax.nn.softmax(hg_lb_logits.astype(F32), axis=0)
    lower_bounds = jnp.cumsum(lb_p, axis=0) - lb_p[0:1]

    bp, lp_len, _ = x_prompt.shape
    bs, ls_len, _ = x_sample.shape
    past_len = 16384
    tab_p = _rotary_tables(jnp.arange(lp_len, dtype=F32))
    tab_s = _rotary_tables(float(past_len) + jnp.arange(ls_len, dtype=F32))
    tab_s = tuple(jnp.tile(t, (SAMPLE_SEQS_PER_TILE, 1)) for t in tab_s)
    final_nw = final_norm_w[None, :]
    dense = {'wb': w_branch.astype(BF16), 'wo': w_out.astype(BF16),
             'wi': w_ffn_in.astype(BF16), 'wfo': w_ffn_out.astype(BF16)}

    s_in = [s.reshape(depth * bs, s.shape[2] * s.shape[3], HEAD_V)
            for s in (state_hgrn, state_gla, state_rwkv, state_ret)]
    s_shift = state_rwkv_shift.reshape(depth * bs, 1, RW_COLS)
    xp, xs = x_prompt, x_sample
    p_outs, s_outs, p_shifts, s_shifts = None, None, [], []
    for l in range(depth):
        lp = _layer_params(l, lower_bounds, w_in, hg_norm_w, gla_wa2, gla_ba, gla_norm_w, rw_mu, rw_w0,
                           rw_w2, rw_a0, rw_a2, rw_g2, rw_kk, rw_ka, rw_rk, rw_ln_w, rw_ln_b)
        lp.update(dense)
        lp['layer'] = l
        final = l == depth - 1
        anw, fnw = attn_norm_w[l][None, :], ffn_norm_w[l][None, :]
        xp, p_outs, sh = _group_layer(xp, None, None, tab_p, lp, anw, fnw, final_nw, final, c=8, c_rw=32,
                                      nseg=1, seglen=PROMPT_TILE, depth=depth, prev_outs=p_outs)
        p_shifts.append(sh)
        xs, s_outs, sh = _group_layer(xs, s_in, s_shift, tab_s, lp, anw, fnw, final_nw, final, c=ls_len,
                                      c_rw=ls_len, nseg=SAMPLE_SEQS_PER_TILE, seglen=ls_len, depth=depth,
                                      prev_outs=s_outs)
        s_shifts.append(sh)

    return (xp, xs) + _states_out(p_outs, p_shifts, depth, bp) + _states_out(s_outs, s_shifts, depth, bs)
```

```python
import functools
import math

import jax
import jax.numpy as jnp
import numpy as np
from jax import lax
from jax.experimental import pallas as pl
from jax.experimental.pallas import tpu as pltpu

F32 = jnp.float32
BF16 = jnp.bfloat16

D_MODEL = 1024
N_HEAD = 4
HEAD_V = 64
BRANCH_W = N_HEAD * HEAD_V
N_BRANCH = 4
GLA_DK = 32
GLA_KW = N_HEAD * GLA_DK
GLA_RANK = 16
GLA_TAU = 16.0
RW_COLS = 1024
NORM_EPS = 1e-6
RW_LN_EPS = 64e-5
MIN_FORGET = 1e-30
ROPE_BASE = 10000.0
LOG2E = 1.4426950408889634
D_FF = 2816
N_MIX = 3968
PROMPT_TILE = 128
SAMPLE_SEQS_PER_TILE = 8
N_GATE = N_BRANCH * D_MODEL

VMEM_LIMIT_BYTES = 56 * 1024 * 1024
LANES = 128
SUBLANES = 8
BF16_ROWS = 16

C_HG_Q, C_HG_F, C_HG_I, C_HG_G = 0, 256, 512, 768
C_GL_Q, C_GL_K, C_GL_V, C_GL_A, C_GL_G = 1024, 1152, 1280, 1536, 1664
C_RW = 1920
C_RT_Q, C_RT_K, C_RT_V, C_RT_G = 2944, 3200, 3456, 3712

P_LB, P_HG_NW, P_GL_NW, P_W0, P_A0, P_KK, P_KA, P_RK, P_LNW, P_LNB, P_RT_LG = range(11)

_NN = (((1,), (0,)), ((), ()))
_NT = (((1,), (1,)), ((), ()))
_TN = (((0,), (0,)), ((), ()))


def _dg(a, b, dn):
    return lax.dot_general(a, b, dn, preferred_element_type=F32)


def _split(x):
    hi = x.astype(BF16)
    lo = (x - hi.astype(F32)).astype(BF16)
    return hi, lo


def _split3(x):
    x1 = x.astype(BF16)
    r = x - x1.astype(F32)
    x2 = r.astype(BF16)
    x3 = (r - x2.astype(F32)).astype(BF16)
    return x1, x2, x3


def _dot1(a, b, dn=_NN):
    return _dg(a.astype(BF16), b.astype(BF16), dn)


def _dots(pieces, b_exact, dn=_NN):
    out = _dg(pieces[0], b_exact, dn)
    for p in pieces[1:]:
        out = out + _dg(p, b_exact, dn)
    return out


def _dotsr(a_exact, pieces, dn=_NN):
    out = _dg(a_exact, pieces[0], dn)
    for p in pieces[1:]:
        out = out + _dg(a_exact, p, dn)
    return out


def _headsum(x, ind):
    return _dg(x.astype(BF16), ind, _NN)


def _sigmoid(x):
    return jax.nn.sigmoid(x)


def _silu(x):
    return x * jax.nn.sigmoid(x)


def _rms(x, w):
    return x * lax.rsqrt(jnp.mean(x * x, axis=-1, keepdims=True) + NORM_EPS) * w


def _cat(parts, axis=0):
    return parts[0] if len(parts) == 1 else jnp.concatenate(parts, axis=axis)


def _by_rows(fn, rows, chunk, start=0):
    outs = [fn(slice(i, min(i + chunk, rows))) for i in range(start, rows, chunk)]
    if isinstance(outs[0], tuple):
        return tuple(_cat([o[j] for o in outs]) for j in range(len(outs[0])))
    return _cat(outs)


def _inproj_kernel(x_ref, nw_ref, w_ref, o_ref):
    h = _rms(x_ref[...], nw_ref[...])
    o_ref[...] = jnp.dot(h.astype(BF16), w_ref[...], preferred_element_type=F32)


def _merge_body(x, o_ref, nw_ref, wg_ref, wb_ref, wo_ref):
    h = _rms(x, nw_ref[...]).astype(BF16)
    merged = None
    for b in range(N_BRANCH):
        gl = jnp.dot(h, wg_ref[:, b * D_MODEL:(b + 1) * D_MODEL], preferred_element_type=F32)
        ob = o_ref[:, b * BRANCH_W:(b + 1) * BRANCH_W].astype(BF16)
        up = jnp.dot(ob, wb_ref[b], preferred_element_type=F32)
        t = _sigmoid(gl) * up
        merged = t if merged is None else merged + t
    return x + jnp.dot(merged.astype(BF16), wo_ref[...], preferred_element_type=F32)


FFN_CHUNK = 1408


def _ffn_body(x, nw_ref, wi_ref, wo_ref):
    h = _rms(x, nw_ref[...]).astype(BF16)
    acc = None
    for j in range(D_FF // FFN_CHUNK):
        lo = j * FFN_CHUNK
        g = jnp.dot(h, wi_ref[:, lo:lo + FFN_CHUNK], preferred_element_type=F32)
        u = jnp.dot(h, wi_ref[:, D_FF + lo:D_FF + lo + FFN_CHUNK], preferred_element_type=F32)
        a = (_silu(g) * u).astype(BF16)
        t = jnp.dot(a, wo_ref[lo:lo + FFN_CHUNK, :], preferred_element_type=F32)
        acc = t if acc is None else acc + t
    return x + acc


def _post_kernel(x_ref, o_ref, anw_ref, wg_ref, wb_ref, wo_ref, fnw_ref, wi_ref, wfo_ref, fw_ref, out_ref,
                 *, final):
    x1 = _merge_body(x_ref[...], o_ref, anw_ref, wg_ref, wb_ref, wo_ref)
    x2 = _ffn_body(x1, fnw_ref, wi_ref, wfo_ref)
    if final:
        x2 = _rms(x2, fw_ref[...])
    out_ref[...] = x2


def _dense_params():
    return pltpu.CompilerParams(dimension_semantics=("arbitrary",), vmem_limit_bytes=VMEM_LIMIT_BYTES)


def _const_spec(shape):
    nd = len(shape)
    return pl.BlockSpec(shape, lambda i: (0,) * nd)


def _token_tile(t):
    return 512 if t % 512 == 0 else t


N_IN = 7952
GATE_COL0 = 3856
WPREP_ROWS = 128


def _wprep_kernel(w_ref, wmix_ref, wg_ref):
    a0 = C_GL_A
    lane = lax.broadcasted_iota(jnp.int32, (1, LANES), 1)
    wmix_ref[:, 0:a0] = w_ref[:, 0:a0].astype(BF16)
    wmix_ref[:, a0:a0 + LANES] = jnp.where(lane < GLA_RANK, w_ref[:, a0:a0 + LANES], 0.0).astype(BF16)
    wmix_ref[:, a0 + LANES:N_MIX] = w_ref[:, a0 + GLA_RANK:GATE_COL0].astype(BF16)
    wg_ref[...] = w_ref[:, GATE_COL0:N_IN].astype(BF16)


def _wprep(w_in, l):
    return pl.pallas_call(
        _wprep_kernel,
        grid=(D_MODEL // WPREP_ROWS,),
        in_specs=[pl.BlockSpec((WPREP_ROWS, N_IN), lambda i: (l * (D_MODEL // WPREP_ROWS) + i, 0))],
        out_specs=[pl.BlockSpec((WPREP_ROWS, N_MIX), lambda i: (i, 0)),
                   pl.BlockSpec((WPREP_ROWS, N_GATE), lambda i: (i, 0))],
        out_shape=[jax.ShapeDtypeStruct((D_MODEL, N_MIX), BF16), jax.ShapeDtypeStruct((D_MODEL, N_GATE), BF16)],
        compiler_params=_dense_params(),
        name="wprep",
    )(w_in.reshape(-1, N_IN))


def _inproj(x, nw, wmix):
    t = x.shape[0]
    tm = _token_tile(t)
    return pl.pallas_call(
        _inproj_kernel,
        grid=(t // tm,),
        in_specs=[pl.BlockSpec((tm, D_MODEL), lambda i: (i, 0)),
                  _const_spec((1, D_MODEL)), _const_spec((D_MODEL, N_MIX))],
        out_specs=pl.BlockSpec((tm, N_MIX), lambda i: (i, 0)),
        out_shape=jax.ShapeDtypeStruct((t, N_MIX), F32),
        compiler_params=_dense_params(),
        name="inproj",
    )(x, nw, wmix)


def _post(x, o, anw, wg, wb, wo, fnw, wi, wfo, fw, final, l):
    t = x.shape[0]
    tm = _token_tile(t)
    return pl.pallas_call(
        functools.partial(_post_kernel, final=final),
        grid=(t // tm,),
        in_specs=[pl.BlockSpec((tm, D_MODEL), lambda i: (i, 0)),
                  pl.BlockSpec((tm, N_BRANCH * BRANCH_W), lambda i: (i, 0)),
                  _const_spec((1, D_MODEL)), _const_spec((D_MODEL, N_GATE)),
                  pl.BlockSpec((None, N_BRANCH, BRANCH_W, D_MODEL), lambda i: (l, 0, 0, 0)),
                  pl.BlockSpec((None, D_MODEL, D_MODEL), lambda i: (l, 0, 0)),
                  _const_spec((1, D_MODEL)), pl.BlockSpec((None, D_MODEL, 2 * D_FF), lambda i: (l, 0, 0)),
                  pl.BlockSpec((None, D_FF, D_MODEL), lambda i: (l, 0, 0)), _const_spec((1, D_MODEL))],
        out_specs=pl.BlockSpec((tm, D_MODEL), lambda i: (i, 0)),
        out_shape=jax.ShapeDtypeStruct((t, D_MODEL), F32),
        compiler_params=_dense_params(),
        name="post",
    )(x, o, anw, wg, wb, wo, fnw, wi, wfo, fw)


def _stack_heads(x, hm):
    return jnp.concatenate([x * hm[h] for h in range(N_HEAD)], axis=0)


def _state_in(s, mask):
    return jnp.concatenate([s] * N_HEAD, axis=1) * mask


def _state_out(st):
    return (st[:, 0:HEAD_V] + st[:, HEAD_V:2 * HEAD_V]) + (st[:, 2 * HEAD_V:3 * HEAD_V] + st[:, 3 * HEAD_V:])


def _gla_tile(res, q, k, v, g3, tri, ones_t, ind, hm_k, hm_v, st_ref, mask, c, nseg, seglen):
    tl = q.shape[0]
    nblk = seglen // c
    b = _dotsr(tri, g3)
    yield
    rowi = lax.broadcasted_iota(jnp.int32, (c, q.shape[1]), 0)
    group = 4 * c
    accs, qes = [], []
    for g0 in range(0, tl, group):
        pieces, vrots = [], []
        for r0 in range(g0, g0 + group, c):
            qb, kb, vb = q[r0:r0 + c], k[r0:r0 + c], v[r0:r0 + c]
            bb = b[r0:r0 + c]
            b2 = bb * LOG2E
            qes.append(qb * jnp.exp2(b2))
            pieces.append(qb * kb)
            vr = [vb]
            for d in range(1, c):
                e = jnp.exp2(b2 - pltpu.roll(b2, d, 0))
                pieces.append(jnp.where(rowi >= d, qb * pltpu.roll(kb, d, 0) * e, 0.0))
                vr.append(pltpu.roll(vb, d, 0))
            vrots.append(vr)
        a = _dot1(jnp.concatenate(pieces, axis=0), ind)
        for i, vr in enumerate(vrots):
            acc = a[i * c * c:i * c * c + c] * vr[0]
            for d in range(1, c):
                acc = acc + a[(i * c + d) * c:(i * c + d + 1) * c] * vr[d]
            accs.append(acc)
        yield
    acc = _cat(accs)
    qe = _cat(qes)
    if nblk > 1:
        scores = []
        for seg in range(nseg):
            base = seg * seglen
            for i in range(1, nblk):
                lo = base + i * c
                r = b[lo - 1:lo, :]
                qi = q[lo:lo + c] * jnp.exp(b[lo:lo + c] - r)
                ki = _by_rows(lambda s: k[s] * jnp.exp(r - b[s]), lo, 2 * BF16_ROWS, base)
                scores.append(_dot1(_stack_heads(qi, hm_k), ki, _NT))
                if i % 3 == 0:
                    yield
        parts = []
        n = 0
        for seg in range(nseg):
            base = seg * seglen
            parts.append(jnp.zeros((c, N_HEAD * HEAD_V), F32))
            for i in range(1, nblk):
                r4 = _dot1(scores[n], v[base:base + i * c])
                n += 1
                oi = r4[0:c] * hm_v[0]
                for h in range(1, N_HEAD):
                    oi = oi + r4[h * c:(h + 1) * c] * hm_v[h]
                parts.append(oi)
                if i % 3 == 0:
                    yield
        acc = acc + jnp.concatenate(parts, axis=0)
    outs = []
    for seg in range(nseg):
        lo, hi = seg * seglen, (seg + 1) * seglen
        blast = b[hi - 1:hi, :]
        dcol = jnp.exp(_dots([p[lo:hi] for p in g3], ones_t[lo:hi], _TN))
        st = st_ref[seg]
        outs.append(acc[lo:hi] + _dot1(qe[lo:hi], st))
        ke = _by_rows(lambda s: k[s] * jnp.exp(blast - b[s]), hi, 2 * BF16_ROWS, lo)
        st_ref[seg] = st * dcol + _dot1(ke, v[lo:hi], _TN) * mask
        yield
    res.append(_cat(outs))
    yield


def _ret_tile(res, q, k, v, dm, lg, rtdec, hm, st_ref, mask, nseg, seglen):
    tau = (lax.broadcasted_iota(jnp.int32, q.shape, 0) & (seglen - 1)).astype(F32)
    a = _dot1(q, _stack_heads(k, hm), _NT) * dm
    yield
    intra = _dot1(a, _stack_heads(v, hm))
    qe = q * jnp.exp((tau + 1.0) * lg)
    ke = k * jnp.exp((float(seglen) - 1.0 - tau) * lg)
    yield
    outs = []
    for seg in range(nseg):
        lo, hi = seg * seglen, (seg + 1) * seglen
        st = st_ref[seg]
        outs.append(intra[lo:hi] + _dot1(qe[lo:hi], st))
        st_ref[seg] = st * rtdec + _dot1(ke[lo:hi], v[lo:hi], _TN) * mask
    res.append(_cat(outs))
    yield


def _head_rms_gate(o, ind64, nw, gate):
    ms = _headsum(o * o, ind64) * (1.0 / HEAD_V)
    return o * lax.rsqrt(ms + NORM_EPS) * nw * _silu(gate)


def _rwkv_blocks(kt, ah, kh, rt, ap, kp, v, dec, hmasks, m256, m_scr, c_scr, c, tick):
    nblk = kt.shape[0] // c
    blocks = range(nblk)
    r4 = lax.broadcasted_iota(jnp.int32, (c, N_HEAD * c), 0)
    s4 = lax.broadcasted_iota(jnp.int32, (c, N_HEAD * c), 1) & (c - 1)
    strict = s4 < r4
    incl = s4 <= r4
    rr = lax.broadcasted_iota(jnp.int32, (N_HEAD * c, N_HEAD * c), 0)
    cc = lax.broadcasted_iota(jnp.int32, (N_HEAD * c, N_HEAD * c), 1)
    sh = int(math.log2(c))
    bd4 = ((rr >> sh) == (cc >> sh)).astype(F32)
    eye4 = (rr == cc).astype(F32)
    eye256 = (lax.broadcasted_iota(jnp.int32, (256, 256), 0) == lax.broadcasted_iota(jnp.int32, (256, 256), 1))

    def blk(x, i):
        return x[i * c:(i + 1) * c]

    def stack(x):
        return _stack_heads(x, hmasks)

    def same(size):
        s = int(math.log2(size))
        return (rr >> s) == (cc >> s)

    kr = [jnp.concatenate([blk(kt, i), blk(rt, i)], axis=0).astype(BF16) for i in blocks]
    la = [_dot1(kr[i], stack(blk(ah, i)), _NT) for i in blocks]
    lk = [_dot1(kr[i], stack(blk(kh, i)), _NT) for i in blocks]
    tick()
    l_a = [jnp.where(strict, x[:c], 0.0) for x in la]
    m_a = [jnp.where(incl, x[c:], 0.0) for x in la]
    l_k = [jnp.where(strict, x[:c], 0.0) for x in lk]
    m_k = [jnp.where(incl, x[c:], 0.0) for x in lk]
    lbd = [jnp.concatenate([x] * N_HEAD, axis=0) * bd4 for x in l_a]
    tm = [eye4 - jnp.where(same(2), x, 0.0) for x in lbd]
    size = 2
    while size < c:
        cross = same(2 * size) & jnp.logical_not(same(size))
        xm = [_dot1(tm[i], jnp.where(cross, lbd[i], 0.0)) for i in blocks]
        tick()
        tm = [tm[i] - _dot1(xm[i], tm[i]) for i in blocks]
        tick()
        size *= 2
    ts = []
    for x in tm:
        t = x[0:c]
        for h in range(1, N_HEAD):
            t = t + x[h * c:(h + 1) * c]
        ts.append(t)
    vs = [stack(blk(v, i)).astype(BF16) for i in blocks]
    lkv = [_dot1(l_k[i], vs[i]) for i in blocks]
    mkv = [_dot1(m_k[i], vs[i]) for i in blocks]
    tick()
    kbar = [_dot1(ts[i], stack(blk(kt, i))) for i in blocks]
    u0 = [-_dot1(ts[i], stack(lkv[i])) for i in blocks]
    tick()
    for i in blocks:
        m = jnp.where(eye256, dec[i * c:i * c + 1], 0.0) - _dot1(blk(ap, i), kbar[i], _TN) * m256
        m_scr[i] = m.astype(BF16)
    tick()
    for i in blocks:
        c_scr[i] = _dot1(jnp.concatenate([blk(ap, i), blk(kp, i)], axis=0),
                         jnp.concatenate([u0[i], blk(v, i)], axis=0), _TN) * m256
    return kbar, u0, m_a, mkv


N_MIXER_SHARED_INPUTS = 19


def _mixer_kernel(*refs, c, c_rw, nseg, seglen, nj, has_state, n_alias):
    refs = list(refs)
    x_ref = refs.pop(0)
    if has_state:
        shg_ref, sgl_ref, srw_ref, srt_ref, shift_ref = refs[:5]
        refs = refs[5:]
    (cos_ref, sin_ref, p256_ref, glba_ref, mu_ref, wa2_ref, w2_ref, a2_ref, g2_ref,
     tri_ref, trib_ref, onesb_ref, ind64_ref, indgl_ref, m256_ref, mgl_ref, dm_ref, rtdec_ref,
     ones_ref) = refs[:N_MIXER_SHARED_INPUTS]
    (o_ref, nhg_ref, ngl_ref, nrw_ref, nrt_ref,
     st_hg, st_gl, st_rw, st_rt, carry, m_scr, c_scr) = refs[N_MIXER_SHARED_INPUTS + n_alias:]
    tl = nseg * seglen
    nblk_seg = seglen // c_rw
    j = pl.program_id(1)

    def prm(i):
        return p256_ref[i:i + 1, :]

    tri = tri_ref[...]
    trib = trib_ref[...]
    onesb = onesb_ref[...]
    ones_t = ones_ref[...]
    ind64 = ind64_ref[...]
    indgl = indgl_ref[...]
    m256 = m256_ref[...]
    mgl = mgl_ref[...]
    lane = lax.broadcasted_iota(jnp.int32, (1, 256), 1)
    hmasks = [((lane >> 6) == h).astype(F32) for h in range(N_HEAD)]
    lane_k = lax.broadcasted_iota(jnp.int32, (1, GLA_KW), 1)
    hmasks_glk = [((lane_k >> 5) == h).astype(F32) for h in range(N_HEAD)]

    @pl.when(j == 0)
    def _():
        for seg in range(nseg):
            if has_state:
                st_hg[seg] = _state_in(shg_ref[seg].reshape(256, HEAD_V), m256)
                st_gl[seg] = _state_in(sgl_ref[seg].reshape(GLA_KW, HEAD_V), mgl)
                st_rw[seg] = _state_in(srw_ref[seg].reshape(256, HEAD_V), m256)
                st_rt[seg] = _state_in(srt_ref[seg].reshape(256, HEAD_V), m256)
                carry[seg] = shift_ref[seg]
            else:
                st_hg[seg] = jnp.zeros((256, 256), F32)
                st_gl[seg] = jnp.zeros((GLA_KW, 256), F32)
                st_rw[seg] = jnp.zeros((256, 256), F32)
                st_rt[seg] = jnp.zeros((256, 256), F32)
                carry[seg] = jnp.zeros((1, RW_COLS), F32)

    def cols(c0, w):
        return x_ref.at[:, c0:c0 + w]

    def hgrn_work(res):
        lb = prm(P_LB)
        xq, xf = cols(C_HG_Q, 256), cols(C_HG_F, 256)

        def prep(s):
            forget = lb + (1.0 - lb) * _sigmoid(xf[s])
            return (_silu(xq[s]), 1.0 - forget) + _split3(jnp.log(jnp.maximum(forget, MIN_FORGET)))

        q, k, g1, g2_, g3_ = _by_rows(prep, tl, BF16_ROWS)
        yield
        yield from _gla_tile(res, q, k, cols(C_HG_I, 256), (g1, g2_, g3_), tri, ones_t, ind64,
                             hmasks, hmasks, st_hg, m256, c, nseg, seglen)
        o_ref[:, 0:256] = _head_rms_gate(res.pop(), ind64, prm(P_HG_NW), x_ref[:, C_HG_G:C_HG_G + 256])
        yield

    def gla_work(res):
        za = _dot1(x_ref[:, C_GL_A:C_GL_A + LANES], wa2_ref[...]) + glba_ref[0:1, :]

        def prep(s):
            z = za[s]
            return _split3((jnp.minimum(z, 0.0) - jnp.log1p(jnp.exp(-jnp.abs(z)))) / GLA_TAU)

        g3 = _by_rows(prep, tl, BF16_ROWS)
        yield
        yield from _gla_tile(res, x_ref[:, C_GL_Q:C_GL_Q + GLA_KW] * GLA_DK ** -0.5,
                             x_ref[:, C_GL_K:C_GL_K + GLA_KW], cols(C_GL_V, 256), g3,
                             tri, ones_t, indgl, hmasks_glk, hmasks, st_gl, mgl, c, nseg, seglen)
        o_ref[:, 256:512] = _head_rms_gate(res.pop(), ind64, prm(P_GL_NW), x_ref[:, C_GL_G:C_GL_G + 256])
        yield

    def ret_work(res):
        upper = (lane & (HEAD_V // 2)) != 0
        xq, xk = cols(C_RT_Q, 256), cols(C_RT_K, 256)

        def rot(x, s):
            sw = jnp.where(upper, pltpu.roll(x, HEAD_V // 2, 1), pltpu.roll(x, 256 - HEAD_V // 2, 1))
            return x * cos_ref[s, :] + sw * sin_ref[s, :]

        q_r, k_r = _by_rows(lambda s: (rot(xq[s], s), rot(xk[s], s) * HEAD_V ** -0.5), tl, SUBLANES)
        yield
        yield from _ret_tile(res, q_r, k_r, x_ref[:, C_RT_V:C_RT_V + 256], dm_ref[...], prm(P_RT_LG),
                             rtdec_ref[...], hmasks, st_rt, m256, nseg, seglen)
        ort = res.pop()
        ms = _headsum(ort * ort, ind64) * (1.0 / HEAD_V)
        o_ref[:, 768:1024] = ort * lax.rsqrt(ms + NORM_EPS) * _silu(x_ref[:, C_RT_G:C_RT_G + 256])
        yield

    work = [ret_work([]), hgrn_work([]), gla_work([])]

    def advance(n):
        for _ in range(n):
            while work:
                try:
                    next(work[0])
                    break
                except StopIteration:
                    work.pop(0)

    xrw = cols(C_RW, RW_COLS)
    first = lax.broadcasted_iota(jnp.int32, (SUBLANES, RW_COLS), 0) == 0
    mu = mu_ref[...]

    def mix_rows(s):
        rw = xrw[s]
        if s.start % seglen == 0:
            prev0 = carry[s.start // seglen]
        else:
            prev0 = xrw[s.start - 1:s.start]
        prev = jnp.where(first, jnp.broadcast_to(prev0, (SUBLANES, RW_COLS)), pltpu.roll(rw, 1, 0))
        mix = rw + (prev - rw) * mu
        k0 = pltpu.roll(mix[:, 256:640], 320, 1)[:, 0:256]
        v_ = pltpu.roll(mix[:, 512:896], 320, 1)[:, 0:256]
        kk0 = k0 * prm(P_KK)
        return (mix[:, 0:256], k0, v_, jnp.tanh(mix[:, 256:384]), mix[:, 768:896],
                _sigmoid(mix[:, 896:1024]), kk0, kk0 * kk0)

    r, k0, v, t_w, s_a, s_g, kk0, kk0sq = _by_rows(mix_rows, tl, SUBLANES)
    for seg in range(nseg):
        hi = (seg + 1) * seglen
        carry[seg] = xrw[hi - 1:hi]
    lw_pre = _dot1(t_w, w2_ref[...])
    a_pre = _dot1(s_a, a2_ref[...])
    g = _dot1(s_g, g2_ref[...])
    n2 = _headsum(kk0sq, ind64)

    def gates(s):
        lw = -math.exp(-0.5) * _sigmoid(prm(P_W0) + lw_pre[s])
        a = _sigmoid(prm(P_A0) + a_pre[s])
        kk = kk0[s] / jnp.maximum(jnp.sqrt(n2[s]), 1e-12)
        k = k0[s] * (1.0 + (a - 1.0) * prm(P_KA))
        return (lw, a * kk, kk, k, r[s] * k * prm(P_RK)) + _split3(lw)

    lw, alpha, kk, k, rkp, lw1, lw2, lw3 = _by_rows(gates, tl, BF16_ROWS)
    bw = _dotsr(trib, (lw1, lw2, lw3))
    bc = _dotsr(onesb, (lw1, lw2, lw3))

    def scaled(s):
        bws, bcs = bw[s], bc[s]
        e_neg = jnp.exp(-bws)
        e_rem = jnp.exp(bcs - bws)
        return (kk[s] * jnp.exp(bws - lw[s]), alpha[s] * e_neg, k[s] * e_neg, r[s] * jnp.exp(bws),
                alpha[s] * e_rem, k[s] * e_rem, jnp.exp(bcs))

    rw_kt, rw_ah, rw_kh, rw_rt, rw_ap, rw_kp, rw_dec = _by_rows(scaled, tl, SUBLANES)
    bonus = _headsum(rkp, ind64) * v

    kbar, u0, m_a, mkv = _rwkv_blocks(rw_kt, rw_ah, rw_kh, rw_rt, rw_ap, rw_kp, v, rw_dec,
                                      hmasks, m256, m_scr, c_scr, c_rw, lambda: advance(1))

    per_step = 6
    sts = [st_rw[seg] for seg in range(nseg)]
    outs = [[None] * nblk_seg for _ in range(nseg)]
    for i in range(nblk_seg):
        for seg in range(nseg):
            blk = seg * nblk_seg + i
            sth, stl = _split(sts[seg])
            kr = jnp.concatenate([kbar[blk], rw_rt[blk * c_rw:(blk + 1) * c_rw]], axis=0).astype(BF16)
            mb = m_scr[blk]
            res = _dg(jnp.concatenate([mb, kr], axis=0), sth, _NN)
            xr = res[256:]
            u = u0[blk] - xr[:c_rw]
            outs[seg][i] = xr[c_rw:] + _dot1(m_a[blk], _stack_heads(u, hmasks)) + mkv[blk]
            sts[seg] = (res[:256] + _dg(mb, stl, _NN)) + c_scr[blk]
        advance(per_step)
    while work:
        advance(1)
    for seg in range(nseg):
        st_rw[seg] = sts[seg]
    orw = _cat([o for seg_outs in outs for o in seg_outs])
    mean = _headsum(orw, ind64) * (1.0 / HEAD_V)
    xc = orw - mean
    var = _headsum(xc * xc, ind64) * (1.0 / HEAD_V)
    o_ref[:, 512:768] = (xc * lax.rsqrt(var + RW_LN_EPS) * prm(P_LNW) + prm(P_LNB) + bonus) * g

    @pl.when(j == nj - 1)
    def _():
        for seg in range(nseg):
            nhg_ref[seg] = _state_out(st_hg[seg])
            ngl_ref[seg] = _state_out(st_gl[seg])
            nrw_ref[seg] = _state_out(st_rw[seg])
            nrt_ref[seg] = _state_out(st_rt[seg])


def _mixer_constants(nseg, seglen, c_rw, log_gamma):
    tl = nseg * seglen
    t = np.arange(tl)
    same = (t[:, None] // c_rw) == (t[None, :] // c_rw)
    same_seg = (t[:, None] // seglen) == (t[None, :] // seglen)
    causal = (t[None, :] <= t[:, None]) & same_seg
    tri = causal.astype(np.float32)
    trib = (same & causal).astype(np.float32)
    onesb = same.astype(np.float32)
    kv = np.arange(256)
    ind64 = ((kv[:, None] // HEAD_V) == (kv[None, :] // HEAD_V)).astype(np.float32)
    kg = np.arange(GLA_KW)
    indgl = ((kg[:, None] // GLA_DK) == (kv[None, :] // HEAD_V)).astype(np.float32)
    diff = jnp.asarray((t[:, None] - t[None, :]).astype(np.float32))
    dm = jnp.concatenate([jnp.where(jnp.asarray(causal), jnp.exp(diff * log_gamma[h]), 0.0)
                          for h in range(N_HEAD)], axis=1)
    rtdec = jnp.broadcast_to(jnp.exp(float(seglen) * jnp.repeat(log_gamma, HEAD_V))[:, None], (256, 256))
    ones_t = np.ones((tl, 256), np.float32)
    return (jnp.asarray(tri, BF16), jnp.asarray(trib, BF16), jnp.asarray(onesb, BF16),
            jnp.asarray(ind64, BF16), jnp.asarray(indgl, BF16), jnp.asarray(ind64, F32),
            jnp.asarray(indgl, F32), dm, rtdec, jnp.asarray(ones_t, BF16))


def _mixer(proj, states, shift, cosf, sins, lp, c, c_rw, nseg, seglen, depth, prev_outs):
    ng, rows, _ = proj.shape
    tl = nseg * seglen
    nj = rows // tl
    has_state = states is not None
    g0 = lp['layer'] * ng
    consts = _mixer_constants(nseg, seglen, c_rw, lp['log_gamma'])

    def seq_spec(w):
        return pl.BlockSpec((None, tl, w), lambda g, j: (g, j, 0))

    def state_spec(hdk):
        return pl.BlockSpec((nseg, hdk, HEAD_V), lambda g, j: (g0 + g, 0, 0))

    def const_spec(a):
        nd = a.ndim
        return pl.BlockSpec(a.shape, lambda g, j: (0,) * nd)

    pos_spec = pl.BlockSpec((tl, 256), lambda g, j: (j, 0))
    params = (lp['p256'], lp['glba'], lp['mu'], lp['wa2'], lp['w2'], lp['a2'], lp['g2'])
    state_specs = [state_spec(256), state_spec(GLA_KW), state_spec(256), state_spec(256)]
    in_specs = [seq_spec(N_MIX)]
    args = [proj]
    if has_state:
        layer = lp['layer']
        in_specs += [pl.BlockSpec((None, nseg) + s.shape[2:], lambda g, j: (layer, g, 0, 0, 0)) for s in states]
        in_specs += [pl.BlockSpec((None, nseg, 1, RW_COLS), lambda g, j: (layer, g, 0, 0))]
        args += list(states) + [shift]
    in_specs += [pos_spec, pos_spec] + [const_spec(a) for a in params] + [const_spec(a) for a in consts]
    args += [cosf, sins, *params, *consts]
    aliases = {}
    if prev_outs is not None:
        aliases = {len(args) + i: 1 + i for i in range(len(prev_outs))}
        in_specs += [pl.BlockSpec(memory_space=pl.ANY)] * len(prev_outs)
        args += list(prev_outs)
    out_specs = [seq_spec(N_BRANCH * BRANCH_W)] + state_specs
    nseq = depth * ng * nseg
    out_shape = [jax.ShapeDtypeStruct((ng, rows, N_BRANCH * BRANCH_W), F32),
                 jax.ShapeDtypeStruct((nseq, 256, HEAD_V), F32), jax.ShapeDtypeStruct((nseq, GLA_KW, HEAD_V), F32),
                 jax.ShapeDtypeStruct((nseq, 256, HEAD_V), F32), jax.ShapeDtypeStruct((nseq, 256, HEAD_V), F32)]
    nblk = tl // c_rw
    scratch = [pltpu.VMEM((nseg, 256, 256), F32), pltpu.VMEM((nseg, GLA_KW, 256), F32),
               pltpu.VMEM((nseg, 256, 256), F32), pltpu.VMEM((nseg, 256, 256), F32),
               pltpu.VMEM((nseg, 1, RW_COLS), F32),
               pltpu.VMEM((nblk, 256, 256), BF16), pltpu.VMEM((nblk, 256, 256), F32)]
    return pl.pallas_call(
        functools.partial(_mixer_kernel, c=c, c_rw=c_rw, nseg=nseg, seglen=seglen, nj=nj,
                          has_state=has_state, n_alias=len(aliases)),
        grid=(ng, nj),
        in_specs=in_specs,
        out_specs=out_specs,
        out_shape=out_shape,
        scratch_shapes=scratch,
        input_output_aliases=aliases,
        compiler_params=pltpu.CompilerParams(dimension_semantics=("arbitrary", "arbitrary"),
                                             vmem_limit_bytes=VMEM_LIMIT_BYTES),
        name="mixer",
    )(*args)


def _tile_heads(v):
    return jnp.tile(v, N_HEAD)


def _layer_params(l, lower_bounds, w_in, hg_norm_w, gla_wa2, gla_ba, gla_norm_w, rw_mu, rw_w0, rw_w2,
                  rw_a0, rw_a2, rw_g2, rw_kk, rw_ka, rw_rk, rw_ln_w, rw_ln_b):
    wmix, wg = _wprep(w_in, l)
    log_gamma = jnp.log1p(-jnp.exp2(-5.0 - jnp.arange(N_HEAD, dtype=F32)))
    rows = [lower_bounds[l], _tile_heads(hg_norm_w[l]), _tile_heads(gla_norm_w[l]), rw_w0[l], rw_a0[l],
            rw_kk[l], rw_ka[l], rw_rk[l], rw_ln_w[l], rw_ln_b[l], jnp.repeat(log_gamma, HEAD_V)]
    p256 = jnp.concatenate([jnp.stack(rows), jnp.zeros((16 - len(rows), 256), F32)], axis=0)
    zeros64 = jnp.zeros((64, 256), F32)
    return {
        'wmix': wmix,
        'log_gamma': log_gamma,
        'wg': wg,
        'p256': p256,
        'glba': jnp.concatenate([gla_ba[l][None, :], jnp.zeros((7, GLA_KW), F32)], axis=0),
        'mu': rw_mu[l][None, :],
        'wa2': jnp.concatenate([gla_wa2[l], jnp.zeros((LANES - GLA_RANK, GLA_KW), F32)], axis=0).astype(BF16),
        'w2': jnp.concatenate([rw_w2[l], zeros64], axis=0).astype(BF16),
        'a2': jnp.concatenate([zeros64, rw_a2[l]], axis=0).astype(BF16),
        'g2': rw_g2[l].astype(BF16),
    }


def _rotary_tables(pos):
    half = HEAD_V // 2
    inv = ROPE_BASE ** (-jnp.arange(half, dtype=F32) / half)
    ang = pos[:, None] * inv[None, :]
    cos, sin = jnp.cos(ang), jnp.sin(ang)
    cosf = jnp.tile(jnp.concatenate([cos, cos], axis=1), (1, N_HEAD))
    sins = jnp.tile(jnp.concatenate([-sin, sin], axis=1), (1, N_HEAD))
    return cosf, sins


def _group_layer(x, states, shift, tables, lp, attn_nw, ffn_nw, final_nw, final, c, c_rw, nseg, seglen,
                 depth, prev_outs):
    nb, length, _ = x.shape
    xf = x.reshape(nb * length, D_MODEL)
    proj = _inproj(xf, attn_nw, lp['wmix'])
    o, *outs = _mixer(proj.reshape(nb // nseg, nseg * length, N_MIX), states, shift,
                      tables[0], tables[1], lp, c, c_rw, nseg, seglen, depth, prev_outs)
    x2 = _post(xf, o.reshape(nb * length, N_BRANCH * BRANCH_W), attn_nw, lp['wg'], lp['wb'], lp['wo'],
               ffn_nw, lp['wi'], lp['wfo'], final_nw, final, lp['layer'])
    new_shift = proj.reshape(nb, length, N_MIX)[:, length - 1, C_RW:C_RW + RW_COLS]
    return x2.reshape(nb, length, D_MODEL), outs, new_shift


def _states_out(outs, shifts, depth, nb):
    n_hg, n_gl, n_rw, n_rt = outs
    return (n_hg.reshape(depth, nb, N_HEAD, HEAD_V, HEAD_V), n_gl.reshape(depth, nb, N_HEAD, GLA_DK, HEAD_V),
            n_rw.reshape(depth, nb, N_HEAD, HEAD_V, HEAD_V), jnp.stack(shifts),
            n_rt.reshape(depth, nb, N_HEAD, HEAD_V, HEAD_V))


def kernel(x_prompt, x_sample, state_hgrn, state_gla, state_rwkv, state_rwkv_shift, state_ret,
           attn_norm_w, w_in, hg_lb_logits, hg_norm_w, gla_wa2, gla_ba, gla_norm_w,
           rw_mu, rw_w0, rw_w2, rw_a0, rw_a2, rw_g2, rw_kk, rw_ka, rw_rk, rw_ln_w, rw_ln_b,
           w_branch, w_out, ffn_norm_w, w_ffn_in, w_ffn_out, final_norm_w):
    depth = w_in.shape[0]
    lb_p = jax.nn.softmax(hg_lb_logits.astype(F32), axis=0)
    lower_bounds = jnp.cumsum(lb_p, axis=0) - lb_p[0:1]

    bp, lp_len, _ = x_prompt.shape
    bs, ls_len, _ = x_sample.shape
    past_len = 16384
    tab_p = _rotary_tables(jnp.arange(lp_len, dtype=F32))
    tab_s = _rotary_tables(float(past_len) + jnp.arange(ls_len, dtype=F32))
    tab_s = tuple(jnp.tile(t, (SAMPLE_SEQS_PER_TILE, 1)) for t in tab_s)
    final_nw = final_norm_w[None, :]
    dense = {'wb': w_branch.astype(BF16), 'wo': w_out.astype(BF16),
             'wi': w_ffn_in.astype(BF16), 'wfo': w_ffn_out.astype(BF16)}

    s_in = [state_hgrn, state_gla, state_rwkv, state_ret]
    s_shift = state_rwkv_shift.reshape(depth, bs, 1, RW_COLS)
    xp, xs = x_prompt, x_sample
    p_outs, s_outs, p_shifts, s_shifts = None, None, [], []
    for l in range(depth):
        lp = _layer_params(l, lower_bounds, w_in, hg_norm_w, gla_wa2, gla_ba, gla_norm_w, rw_mu, rw_w0,
                           rw_w2, rw_a0, rw_a2, rw_g2, rw_kk, rw_ka, rw_rk, rw_ln_w, rw_ln_b)
        lp.update(dense)
        lp['layer'] = l
        final = l == depth - 1
        anw, fnw = attn_norm_w[l][None, :], ffn_norm_w[l][None, :]
        xp, p_outs, sh = _group_layer(xp, None, None, tab_p, lp, anw, fnw, final_nw, final, c=8, c_rw=32,
                                      nseg=1, seglen=PROMPT_TILE, depth=depth, prev_outs=p_outs)
        p_shifts.append(sh)
        xs, s_outs, sh = _group_layer(xs, s_in, s_shift, tab_s, lp, anw, fnw, final_nw, final, c=ls_len,
                                      c_rw=ls_len, nseg=SAMPLE_SEQS_PER_TILE, seglen=ls_len, depth=depth,
                                      prev_outs=s_outs)
        s_shifts.append(sh)

    return (xp, xs) + _states_out(p_outs, p_shifts, depth, bp) + _states_out(s_outs, s_shifts, depth, bs)
```

```python
import functools
import math

import jax
import jax.numpy as jnp
import numpy as np
from jax import lax
from jax.experimental import pallas as pl
from jax.experimental.pallas import tpu as pltpu

F32 = jnp.float32
BF16 = jnp.bfloat16

D_MODEL = 1024
N_HEAD = 4
HEAD_V = 64
BRANCH_W = N_HEAD * HEAD_V
N_BRANCH = 4
GLA_DK = 32
GLA_KW = N_HEAD * GLA_DK
GLA_RANK = 16
GLA_TAU = 16.0
RW_COLS = 1024
NORM_EPS = 1e-6
RW_LN_EPS = 64e-5
MIN_FORGET = 1e-30
ROPE_BASE = 10000.0
LOG2E = 1.4426950408889634
D_FF = 2816
N_MIX = 3968
PROMPT_TILE = 128
SAMPLE_SEQS_PER_TILE = 8
N_GATE = N_BRANCH * D_MODEL

VMEM_LIMIT_BYTES = 56 * 1024 * 1024
LANES = 128
SUBLANES = 8
BF16_ROWS = 16

C_HG_Q, C_HG_F, C_HG_I, C_HG_G = 0, 256, 512, 768
C_GL_Q, C_GL_K, C_GL_V, C_GL_A, C_GL_G = 1024, 1152, 1280, 1536, 1664
C_RW = 1920
C_RT_Q, C_RT_K, C_RT_V, C_RT_G = 2944, 3200, 3456, 3712

P_LB, P_HG_NW, P_GL_NW, P_W0, P_A0, P_KK, P_KA, P_RK, P_LNW, P_LNB, P_RT_LG = range(11)

_NN = (((1,), (0,)), ((), ()))
_NT = (((1,), (1,)), ((), ()))
_TN = (((0,), (0,)), ((), ()))


def _dg(a, b, dn):
    return lax.dot_general(a, b, dn, preferred_element_type=F32)


def _split(x):
    hi = x.astype(BF16)
    lo = (x - hi.astype(F32)).astype(BF16)
    return hi, lo


def _split3(x):
    x1 = x.astype(BF16)
    r = x - x1.astype(F32)
    x2 = r.astype(BF16)
    x3 = (r - x2.astype(F32)).astype(BF16)
    return x1, x2, x3


def _dot1(a, b, dn=_NN):
    return _dg(a.astype(BF16), b.astype(BF16), dn)


def _dots(pieces, b_exact, dn=_NN):
    out = _dg(pieces[0], b_exact, dn)
    for p in pieces[1:]:
        out = out + _dg(p, b_exact, dn)
    return out


def _dotsr(a_exact, pieces, dn=_NN):
    out = _dg(a_exact, pieces[0], dn)
    for p in pieces[1:]:
        out = out + _dg(a_exact, p, dn)
    return out


def _headsum(x, ind):
    return _dg(x.astype(BF16), ind, _NN)


def _sigmoid(x):
    return jax.nn.sigmoid(x)


def _silu(x):
    return x * jax.nn.sigmoid(x)


def _rms(x, w):
    return x * lax.rsqrt(jnp.mean(x * x, axis=-1, keepdims=True) + NORM_EPS) * w


def _cat(parts, axis=0):
    return parts[0] if len(parts) == 1 else jnp.concatenate(parts, axis=axis)


def _by_rows(fn, rows, chunk, start=0):
    outs = [fn(slice(i, min(i + chunk, rows))) for i in range(start, rows, chunk)]
    if isinstance(outs[0], tuple):
        return tuple(_cat([o[j] for o in outs]) for j in range(len(outs[0])))
    return _cat(outs)


def _inproj_kernel(x_ref, nw_ref, w_ref, o_ref):
    h = _rms(x_ref[...], nw_ref[...])
    o_ref[...] = _dg(h.astype(BF16), w_ref[...], _NT)


def _merge_body(x, o_ref, nw_ref, wg_ref, wb_ref, wo_ref):
    h = _rms(x, nw_ref[...]).astype(BF16)
    merged = None
    for b in range(N_BRANCH):
        gl = _dg(h, wg_ref[b * D_MODEL:(b + 1) * D_MODEL, :], _NT)
        ob = o_ref[:, b * BRANCH_W:(b + 1) * BRANCH_W].astype(BF16)
        up = jnp.dot(ob, wb_ref[b], preferred_element_type=F32)
        t = _sigmoid(gl) * up
        merged = t if merged is None else merged + t
    return x + jnp.dot(merged.astype(BF16), wo_ref[...], preferred_element_type=F32)


FFN_CHUNK = 1408


def _ffn_body(x, nw_ref, wi_ref, wo_ref):
    h = _rms(x, nw_ref[...]).astype(BF16)
    acc = None
    for j in range(D_FF // FFN_CHUNK):
        lo = j * FFN_CHUNK
        g = jnp.dot(h, wi_ref[:, lo:lo + FFN_CHUNK], preferred_element_type=F32)
        u = jnp.dot(h, wi_ref[:, D_FF + lo:D_FF + lo + FFN_CHUNK], preferred_element_type=F32)
        a = (_silu(g) * u).astype(BF16)
        t = jnp.dot(a, wo_ref[lo:lo + FFN_CHUNK, :], preferred_element_type=F32)
        acc = t if acc is None else acc + t
    return x + acc


def _post_kernel(x_ref, o_ref, anw_ref, wg_ref, wb_ref, wo_ref, fnw_ref, wi_ref, wfo_ref, fw_ref, out_ref,
                 *, final):
    x1 = _merge_body(x_ref[...], o_ref, anw_ref, wg_ref, wb_ref, wo_ref)
    x2 = _ffn_body(x1, fnw_ref, wi_ref, wfo_ref)
    if final:
        x2 = _rms(x2, fw_ref[...])
    out_ref[...] = x2


def _dense_params():
    return pltpu.CompilerParams(dimension_semantics=("arbitrary",), vmem_limit_bytes=VMEM_LIMIT_BYTES)


def _const_spec(shape):
    nd = len(shape)
    return pl.BlockSpec(shape, lambda i: (0,) * nd)


def _token_tile(t):
    return 512 if t % 512 == 0 else t


GATE_COL0 = 3856


def _prep_in_weights(w_in, l):
    wt = jnp.swapaxes(w_in, 1, 2)[l]
    a1 = C_GL_A + GLA_RANK
    wmix_t = jnp.concatenate([wt[0:a1], jnp.zeros((LANES - GLA_RANK, D_MODEL), F32), wt[a1:GATE_COL0]], axis=0)
    return wmix_t.astype(BF16), wt[GATE_COL0:].astype(BF16)


def _inproj(x, nw, wmix):
    t = x.shape[0]
    tm = _token_tile(t)
    return pl.pallas_call(
        _inproj_kernel,
        grid=(t // tm,),
        in_specs=[pl.BlockSpec((tm, D_MODEL), lambda i: (i, 0)),
                  _const_spec((1, D_MODEL)), _const_spec((N_MIX, D_MODEL))],
        out_specs=pl.BlockSpec((tm, N_MIX), lambda i: (i, 0)),
        out_shape=jax.ShapeDtypeStruct((t, N_MIX), F32),
        compiler_params=_dense_params(),
        name="inproj",
    )(x, nw, wmix)


def _post(x, o, anw, wg, wb, wo, fnw, wi, wfo, fw, final, l):
    t = x.shape[0]
    tm = _token_tile(t)
    return pl.pallas_call(
        functools.partial(_post_kernel, final=final),
        grid=(t // tm,),
        in_specs=[pl.BlockSpec((tm, D_MODEL), lambda i: (i, 0)),
                  pl.BlockSpec((tm, N_BRANCH * BRANCH_W), lambda i: (i, 0)),
                  _const_spec((1, D_MODEL)), _const_spec((N_GATE, D_MODEL)),
                  pl.BlockSpec((None, N_BRANCH, BRANCH_W, D_MODEL), lambda i: (l, 0, 0, 0)),
                  pl.BlockSpec((None, D_MODEL, D_MODEL), lambda i: (l, 0, 0)),
                  _const_spec((1, D_MODEL)), pl.BlockSpec((None, D_MODEL, 2 * D_FF), lambda i: (l, 0, 0)),
                  pl.BlockSpec((None, D_FF, D_MODEL), lambda i: (l, 0, 0)), _const_spec((1, D_MODEL))],
        out_specs=pl.BlockSpec((tm, D_MODEL), lambda i: (i, 0)),
        out_shape=jax.ShapeDtypeStruct((t, D_MODEL), F32),
        compiler_params=_dense_params(),
        name="post",
    )(x, o, anw, wg, wb, wo, fnw, wi, wfo, fw)


def _stack_heads(x, hm):
    return jnp.concatenate([x * hm[h] for h in range(N_HEAD)], axis=0)


def _state_in(s, mask):
    return jnp.concatenate([s] * N_HEAD, axis=1) * mask


def _state_out(st):
    return (st[:, 0:HEAD_V] + st[:, HEAD_V:2 * HEAD_V]) + (st[:, 2 * HEAD_V:3 * HEAD_V] + st[:, 3 * HEAD_V:])


def _gla_tile(res, q, k, v, g3, tri, ones_t, ind, hm_k, hm_v, st_ref, mask, c, nseg, seglen):
    tl = q.shape[0]
    nblk = seglen // c
    b = _dotsr(tri, g3)
    yield
    rowi = lax.broadcasted_iota(jnp.int32, (c, q.shape[1]), 0)
    group = 4 * c
    accs, qes = [], []
    for g0 in range(0, tl, group):
        pieces, vrots = [], []
        for r0 in range(g0, g0 + group, c):
            qb, kb, vb = q[r0:r0 + c], k[r0:r0 + c], v[r0:r0 + c]
            bb = b[r0:r0 + c]
            b2 = bb * LOG2E
            qes.append(qb * jnp.exp2(b2))
            pieces.append(qb * kb)
            vr = [vb]
            for d in range(1, c):
                e = jnp.exp2(b2 - pltpu.roll(b2, d, 0))
                pieces.append(jnp.where(rowi >= d, qb * pltpu.roll(kb, d, 0) * e, 0.0))
                vr.append(pltpu.roll(vb, d, 0))
            vrots.append(vr)
        a = _dot1(jnp.concatenate(pieces, axis=0), ind)
        for i, vr in enumerate(vrots):
            acc = a[i * c * c:i * c * c + c] * vr[0]
            for d in range(1, c):
                acc = acc + a[(i * c + d) * c:(i * c + d + 1) * c] * vr[d]
            accs.append(acc)
        yield
    acc = _cat(accs)
    qe = _cat(qes)
    if nblk > 1:
        scores = []
        for seg in range(nseg):
            base = seg * seglen
            for i in range(1, nblk):
                lo = base + i * c
                r = b[lo - 1:lo, :]
                qi = q[lo:lo + c] * jnp.exp(b[lo:lo + c] - r)
                ki = _by_rows(lambda s: k[s] * jnp.exp(r - b[s]), lo, 2 * BF16_ROWS, base)
                scores.append(_dot1(_stack_heads(qi, hm_k), ki, _NT))
                if i % 3 == 0:
                    yield
        parts = []
        n = 0
        for seg in range(nseg):
            base = seg * seglen
            parts.append(jnp.zeros((c, N_HEAD * HEAD_V), F32))
            for i in range(1, nblk):
                r4 = _dot1(scores[n], v[base:base + i * c])
                n += 1
                oi = r4[0:c] * hm_v[0]
                for h in range(1, N_HEAD):
                    oi = oi + r4[h * c:(h + 1) * c] * hm_v[h]
                parts.append(oi)
                if i % 3 == 0:
                    yield
        acc = acc + jnp.concatenate(parts, axis=0)
    outs = []
    for seg in range(nseg):
        lo, hi = seg * seglen, (seg + 1) * seglen
        blast = b[hi - 1:hi, :]
        dcol = jnp.exp(_dots([p[lo:hi] for p in g3], ones_t[lo:hi], _TN))
        st = st_ref[seg]
        outs.append(acc[lo:hi] + _dot1(qe[lo:hi], st))
        ke = _by_rows(lambda s: k[s] * jnp.exp(blast - b[s]), hi, 2 * BF16_ROWS, lo)
        st_ref[seg] = st * dcol + _dot1(ke, v[lo:hi], _TN) * mask
        yield
    res.append(_cat(outs))
    yield


def _ret_tile(res, q, k, v, dm, lg, rtdec, hm, st_ref, mask, nseg, seglen):
    tau = (lax.broadcasted_iota(jnp.int32, q.shape, 0) & (seglen - 1)).astype(F32)
    a = _dot1(q, _stack_heads(k, hm), _NT) * dm
    yield
    intra = _dot1(a, _stack_heads(v, hm))
    qe = q * jnp.exp((tau + 1.0) * lg)
    ke = k * jnp.exp((float(seglen) - 1.0 - tau) * lg)
    yield
    outs = []
    for seg in range(nseg):
        lo, hi = seg * seglen, (seg + 1) * seglen
        st = st_ref[seg]
        outs.append(intra[lo:hi] + _dot1(qe[lo:hi], st))
        st_ref[seg] = st * rtdec + _dot1(ke[lo:hi], v[lo:hi], _TN) * mask
    res.append(_cat(outs))
    yield


def _head_rms_gate(o, ind64, nw, gate):
    ms = _headsum(o * o, ind64) * (1.0 / HEAD_V)
    return o * lax.rsqrt(ms + NORM_EPS) * nw * _silu(gate)


def _rwkv_blocks(kt, ah, kh, rt, ap, kp, v, dec, hmasks, m256, m_scr, c_scr, c, tick):
    nblk = kt.shape[0] // c
    blocks = range(nblk)
    r4 = lax.broadcasted_iota(jnp.int32, (c, N_HEAD * c), 0)
    s4 = lax.broadcasted_iota(jnp.int32, (c, N_HEAD * c), 1) & (c - 1)
    strict = s4 < r4
    incl = s4 <= r4
    rr = lax.broadcasted_iota(jnp.int32, (N_HEAD * c, N_HEAD * c), 0)
    cc = lax.broadcasted_iota(jnp.int32, (N_HEAD * c, N_HEAD * c), 1)
    sh = int(math.log2(c))
    bd4 = ((rr >> sh) == (cc >> sh)).astype(F32)
    eye4 = (rr == cc).astype(F32)
    eye256 = (lax.broadcasted_iota(jnp.int32, (256, 256), 0) == lax.broadcasted_iota(jnp.int32, (256, 256), 1))

    def blk(x, i):
        return x[i * c:(i + 1) * c]

    def stack(x):
        return _stack_heads(x, hmasks)

    def same(size):
        s = int(math.log2(size))
        return (rr >> s) == (cc >> s)

    kr = [jnp.concatenate([blk(kt, i), blk(rt, i)], axis=0).astype(BF16) for i in blocks]
    la = [_dot1(kr[i], stack(blk(ah, i)), _NT) for i in blocks]
    lk = [_dot1(kr[i], stack(blk(kh, i)), _NT) for i in blocks]
    tick()
    l_a = [jnp.where(strict, x[:c], 0.0) for x in la]
    m_a = [jnp.where(incl, x[c:], 0.0) for x in la]
    l_k = [jnp.where(strict, x[:c], 0.0) for x in lk]
    m_k = [jnp.where(incl, x[c:], 0.0) for x in lk]
    lbd = [jnp.concatenate([x] * N_HEAD, axis=0) * bd4 for x in l_a]
    tm = [eye4 - jnp.where(same(2), x, 0.0) for x in lbd]
    size = 2
    while size < c:
        cross = same(2 * size) & jnp.logical_not(same(size))
        xm = [_dot1(tm[i], jnp.where(cross, lbd[i], 0.0)) for i in blocks]
        tick()
        tm = [tm[i] - _dot1(xm[i], tm[i]) for i in blocks]
        tick()
        size *= 2
    ts = []
    for x in tm:
        t = x[0:c]
        for h in range(1, N_HEAD):
            t = t + x[h * c:(h + 1) * c]
        ts.append(t)
    vs = [stack(blk(v, i)).astype(BF16) for i in blocks]
    lkv = [_dot1(l_k[i], vs[i]) for i in blocks]
    mkv = [_dot1(m_k[i], vs[i]) for i in blocks]
    tick()
    kbar = [_dot1(ts[i], stack(blk(kt, i))) for i in blocks]
    u0 = [-_dot1(ts[i], stack(lkv[i])) for i in blocks]
    tick()
    for i in blocks:
        m = jnp.where(eye256, dec[i * c:i * c + 1], 0.0) - _dot1(blk(ap, i), kbar[i], _TN) * m256
        m_scr[i] = m.astype(BF16)
    tick()
    for i in blocks:
        c_scr[i] = _dot1(jnp.concatenate([blk(ap, i), blk(kp, i)], axis=0),
                         jnp.concatenate([u0[i], blk(v, i)], axis=0), _TN) * m256
    return kbar, u0, m_a, mkv


N_MIXER_SHARED_INPUTS = 19


def _mixer_kernel(*refs, c, c_rw, nseg, seglen, nj, has_state, n_alias):
    refs = list(refs)
    x_ref = refs.pop(0)
    if has_state:
        shg_ref, sgl_ref, srw_ref, srt_ref, shift_ref = refs[:5]
        refs = refs[5:]
    (cos_ref, sin_ref, p256_ref, glba_ref, mu_ref, wa2_ref, w2_ref, a2_ref, g2_ref,
     tri_ref, trib_ref, onesb_ref, ind64_ref, indgl_ref, m256_ref, mgl_ref, dm_ref, rtdec_ref,
     ones_ref) = refs[:N_MIXER_SHARED_INPUTS]
    (o_ref, nhg_ref, ngl_ref, nrw_ref, nrt_ref,
     st_hg, st_gl, st_rw, st_rt, carry, m_scr, c_scr) = refs[N_MIXER_SHARED_INPUTS + n_alias:]
    tl = nseg * seglen
    nblk_seg = seglen // c_rw
    j = pl.program_id(1)

    def prm(i):
        return p256_ref[i:i + 1, :]

    tri = tri_ref[...]
    trib = trib_ref[...]
    onesb = onesb_ref[...]
    ones_t = ones_ref[...]
    ind64 = ind64_ref[...]
    indgl = indgl_ref[...]
    m256 = m256_ref[...]
    mgl = mgl_ref[...]
    lane = lax.broadcasted_iota(jnp.int32, (1, 256), 1)
    hmasks = [((lane >> 6) == h).astype(F32) for h in range(N_HEAD)]
    lane_k = lax.broadcasted_iota(jnp.int32, (1, GLA_KW), 1)
    hmasks_glk = [((lane_k >> 5) == h).astype(F32) for h in range(N_HEAD)]

    @pl.when(j == 0)
    def _():
        for seg in range(nseg):
            if has_state:
                st_hg[seg] = _state_in(shg_ref[seg].reshape(256, HEAD_V), m256)
                st_gl[seg] = _state_in(sgl_ref[seg].reshape(GLA_KW, HEAD_V), mgl)
                st_rw[seg] = _state_in(srw_ref[seg].reshape(256, HEAD_V), m256)
                st_rt[seg] = _state_in(srt_ref[seg].reshape(256, HEAD_V), m256)
                carry[seg] = shift_ref[seg]
            else:
                st_hg[seg] = jnp.zeros((256, 256), F32)
                st_gl[seg] = jnp.zeros((GLA_KW, 256), F32)
                st_rw[seg] = jnp.zeros((256, 256), F32)
                st_rt[seg] = jnp.zeros((256, 256), F32)
                carry[seg] = jnp.zeros((1, RW_COLS), F32)

    def cols(c0, w):
        return x_ref.at[:, c0:c0 + w]

    def hgrn_work(res):
        lb = prm(P_LB)
        xq, xf = cols(C_HG_Q, 256), cols(C_HG_F, 256)

        def prep(s):
            forget = lb + (1.0 - lb) * _sigmoid(xf[s])
            return (_silu(xq[s]), 1.0 - forget) + _split3(jnp.log(jnp.maximum(forget, MIN_FORGET)))

        q, k, g1, g2_, g3_ = _by_rows(prep, tl, BF16_ROWS)
        yield
        yield from _gla_tile(res, q, k, cols(C_HG_I, 256), (g1, g2_, g3_), tri, ones_t, ind64,
                             hmasks, hmasks, st_hg, m256, c, nseg, seglen)
        o_ref[:, 0:256] = _head_rms_gate(res.pop(), ind64, prm(P_HG_NW), x_ref[:, C_HG_G:C_HG_G + 256])
        yield

    def gla_work(res):
        za = _dot1(x_ref[:, C_GL_A:C_GL_A + LANES], wa2_ref[...]) + glba_ref[0:1, :]

        def prep(s):
            z = za[s]
            return _split3((jnp.minimum(z, 0.0) - jnp.log1p(jnp.exp(-jnp.abs(z)))) / GLA_TAU)

        g3 = _by_rows(prep, tl, BF16_ROWS)
        yield
        yield from _gla_tile(res, x_ref[:, C_GL_Q:C_GL_Q + GLA_KW] * GLA_DK ** -0.5,
                             x_ref[:, C_GL_K:C_GL_K + GLA_KW], cols(C_GL_V, 256), g3,
                             tri, ones_t, indgl, hmasks_glk, hmasks, st_gl, mgl, c, nseg, seglen)
        o_ref[:, 256:512] = _head_rms_gate(res.pop(), ind64, prm(P_GL_NW), x_ref[:, C_GL_G:C_GL_G + 256])
        yield

    def ret_work(res):
        upper = (lane & (HEAD_V // 2)) != 0
        xq, xk = cols(C_RT_Q, 256), cols(C_RT_K, 256)

        def rot(x, s):
            sw = jnp.where(upper, pltpu.roll(x, HEAD_V // 2, 1), pltpu.roll(x, 256 - HEAD_V // 2, 1))
            return x * cos_ref[s, :] + sw * sin_ref[s, :]

        q_r, k_r = _by_rows(lambda s: (rot(xq[s], s), rot(xk[s], s) * HEAD_V ** -0.5), tl, SUBLANES)
        yield
        yield from _ret_tile(res, q_r, k_r, x_ref[:, C_RT_V:C_RT_V + 256], dm_ref[...], prm(P_RT_LG),
                             rtdec_ref[...], hmasks, st_rt, m256, nseg, seglen)
        ort = res.pop()
        ms = _headsum(ort * ort, ind64) * (1.0 / HEAD_V)
        o_ref[:, 768:1024] = ort * lax.rsqrt(ms + NORM_EPS) * _silu(x_ref[:, C_RT_G:C_RT_G + 256])
        yield

    work = [ret_work([]), hgrn_work([]), gla_work([])]

    def advance(n):
        for _ in range(n):
            while work:
                try:
                    next(work[0])
                    break
                except StopIteration:
                    work.pop(0)

    xrw = cols(C_RW, RW_COLS)
    first = lax.broadcasted_iota(jnp.int32, (SUBLANES, RW_COLS), 0) == 0
    mu = mu_ref[...]

    def mix_rows(s):
        rw = xrw[s]
        if s.start % seglen == 0:
            prev0 = carry[s.start // seglen]
        else:
            prev0 = xrw[s.start - 1:s.start]
        prev = jnp.where(first, jnp.broadcast_to(prev0, (SUBLANES, RW_COLS)), pltpu.roll(rw, 1, 0))
        mix = rw + (prev - rw) * mu
        k0 = pltpu.roll(mix[:, 256:640], 320, 1)[:, 0:256]
        v_ = pltpu.roll(mix[:, 512:896], 320, 1)[:, 0:256]
        kk0 = k0 * prm(P_KK)
        return (mix[:, 0:256], k0, v_, jnp.tanh(mix[:, 256:384]), mix[:, 768:896],
                _sigmoid(mix[:, 896:1024]), kk0, kk0 * kk0)

    r, k0, v, t_w, s_a, s_g, kk0, kk0sq = _by_rows(mix_rows, tl, SUBLANES)
    for seg in range(nseg):
        hi = (seg + 1) * seglen
        carry[seg] = xrw[hi - 1:hi]
    lw_pre = _dot1(t_w, w2_ref[...])
    a_pre = _dot1(s_a, a2_ref[...])
    g = _dot1(s_g, g2_ref[...])
    n2 = _headsum(kk0sq, ind64)

    def gates(s):
        lw = -math.exp(-0.5) * _sigmoid(prm(P_W0) + lw_pre[s])
        a = _sigmoid(prm(P_A0) + a_pre[s])
        kk = kk0[s] / jnp.maximum(jnp.sqrt(n2[s]), 1e-12)
        k = k0[s] * (1.0 + (a - 1.0) * prm(P_KA))
        return (lw, a * kk, kk, k, r[s] * k * prm(P_RK)) + _split3(lw)

    lw, alpha, kk, k, rkp, lw1, lw2, lw3 = _by_rows(gates, tl, BF16_ROWS)
    bw = _dotsr(trib, (lw1, lw2, lw3))
    bc = _dotsr(onesb, (lw1, lw2, lw3))

    def scaled(s):
        bws, bcs = bw[s], bc[s]
        e_neg = jnp.exp(-bws)
        e_rem = jnp.exp(bcs - bws)
        return (kk[s] * jnp.exp(bws - lw[s]), alpha[s] * e_neg, k[s] * e_neg, r[s] * jnp.exp(bws),
                alpha[s] * e_rem, k[s] * e_rem, jnp.exp(bcs))

    rw_kt, rw_ah, rw_kh, rw_rt, rw_ap, rw_kp, rw_dec = _by_rows(scaled, tl, SUBLANES)
    bonus = _headsum(rkp, ind64) * v

    kbar, u0, m_a, mkv = _rwkv_blocks(rw_kt, rw_ah, rw_kh, rw_rt, rw_ap, rw_kp, v, rw_dec,
                                      hmasks, m256, m_scr, c_scr, c_rw, lambda: advance(1))

    per_step = 6
    sts = [st_rw[seg] for seg in range(nseg)]
    outs = [[None] * nblk_seg for _ in range(nseg)]
    for i in range(nblk_seg):
        for seg in range(nseg):
            blk = seg * nblk_seg + i
            sth, stl = _split(sts[seg])
            kr = jnp.concatenate([kbar[blk], rw_rt[blk * c_rw:(blk + 1) * c_rw]], axis=0).astype(BF16)
            mb = m_scr[blk]
            res = _dg(jnp.concatenate([mb, kr], axis=0), sth, _NN)
            xr = res[256:]
            u = u0[blk] - xr[:c_rw]
            outs[seg][i] = xr[c_rw:] + _dot1(m_a[blk], _stack_heads(u, hmasks)) + mkv[blk]
            sts[seg] = (res[:256] + _dg(mb, stl, _NN)) + c_scr[blk]
        advance(per_step)
    while work:
        advance(1)
    for seg in range(nseg):
        st_rw[seg] = sts[seg]
    orw = _cat([o for seg_outs in outs for o in seg_outs])
    mean = _headsum(orw, ind64) * (1.0 / HEAD_V)
    xc = orw - mean
    var = _headsum(xc * xc, ind64) * (1.0 / HEAD_V)
    o_ref[:, 512:768] = (xc * lax.rsqrt(var + RW_LN_EPS) * prm(P_LNW) + prm(P_LNB) + bonus) * g

    @pl.when(j == nj - 1)
    def _():
        for seg in range(nseg):
            nhg_ref[seg] = _state_out(st_hg[seg])
            ngl_ref[seg] = _state_out(st_gl[seg])
            nrw_ref[seg] = _state_out(st_rw[seg])
            nrt_ref[seg] = _state_out(st_rt[seg])


def _mixer_constants(nseg, seglen, c_rw, log_gamma):
    tl = nseg * seglen
    t = np.arange(tl)
    same = (t[:, None] // c_rw) == (t[None, :] // c_rw)
    same_seg = (t[:, None] // seglen) == (t[None, :] // seglen)
    causal = (t[None, :] <= t[:, None]) & same_seg
    tri = causal.astype(np.float32)
    trib = (same & causal).astype(np.float32)
    onesb = same.astype(np.float32)
    kv = np.arange(256)
    ind64 = ((kv[:, None] // HEAD_V) == (kv[None, :] // HEAD_V)).astype(np.float32)
    kg = np.arange(GLA_KW)
    indgl = ((kg[:, None] // GLA_DK) == (kv[None, :] // HEAD_V)).astype(np.float32)
    diff = jnp.asarray((t[:, None] - t[None, :]).astype(np.float32))
    dm = jnp.concatenate([jnp.where(jnp.asarray(causal), jnp.exp(diff * log_gamma[h]), 0.0)
                          for h in range(N_HEAD)], axis=1)
    rtdec = jnp.broadcast_to(jnp.exp(float(seglen) * jnp.repeat(log_gamma, HEAD_V))[:, None], (256, 256))
    ones_t = np.ones((tl, 256), np.float32)
    return (jnp.asarray(tri, BF16), jnp.asarray(trib, BF16), jnp.asarray(onesb, BF16),
            jnp.asarray(ind64, BF16), jnp.asarray(indgl, BF16), jnp.asarray(ind64, F32),
            jnp.asarray(indgl, F32), dm, rtdec, jnp.asarray(ones_t, BF16))


def _mixer(proj, states, shift, cosf, sins, lp, c, c_rw, nseg, seglen, depth, prev_outs):
    ng, rows, _ = proj.shape
    tl = nseg * seglen
    nj = rows // tl
    has_state = states is not None
    g0 = lp['layer'] * ng
    consts = _mixer_constants(nseg, seglen, c_rw, lp['log_gamma'])

    def seq_spec(w):
        return pl.BlockSpec((None, tl, w), lambda g, j: (g, j, 0))

    def state_spec(hdk):
        return pl.BlockSpec((nseg, hdk, HEAD_V), lambda g, j: (g0 + g, 0, 0))

    def const_spec(a):
        nd = a.ndim
        return pl.BlockSpec(a.shape, lambda g, j: (0,) * nd)

    pos_spec = pl.BlockSpec((tl, 256), lambda g, j: (j, 0))
    params = (lp['p256'], lp['glba'], lp['mu'], lp['wa2'], lp['w2'], lp['a2'], lp['g2'])
    state_specs = [state_spec(256), state_spec(GLA_KW), state_spec(256), state_spec(256)]
    in_specs = [seq_spec(N_MIX)]
    args = [proj]
    if has_state:
        layer = lp['layer']
        in_specs += [pl.BlockSpec((None, nseg) + s.shape[2:], lambda g, j: (layer, g, 0, 0, 0)) for s in states]
        in_specs += [pl.BlockSpec((None, nseg, 1, RW_COLS), lambda g, j: (layer, g, 0, 0))]
        args += list(states) + [shift]
    in_specs += [pos_spec, pos_spec] + [const_spec(a) for a in params] + [const_spec(a) for a in consts]
    args += [cosf, sins, *params, *consts]
    aliases = {len(args) + i: 1 + i for i in range(len(prev_outs))}
    in_specs += [pl.BlockSpec(memory_space=pl.ANY)] * len(prev_outs)
    args += list(prev_outs)
    out_specs = [seq_spec(N_BRANCH * BRANCH_W)] + state_specs
    nseq = depth * ng * nseg
    out_shape = [jax.ShapeDtypeStruct((ng, rows, N_BRANCH * BRANCH_W), F32),
                 jax.ShapeDtypeStruct((nseq, 256, HEAD_V), F32), jax.ShapeDtypeStruct((nseq, GLA_KW, HEAD_V), F32),
                 jax.ShapeDtypeStruct((nseq, 256, HEAD_V), F32), jax.ShapeDtypeStruct((nseq, 256, HEAD_V), F32)]
    nblk = tl // c_rw
    scratch = [pltpu.VMEM((nseg, 256, 256), F32), pltpu.VMEM((nseg, GLA_KW, 256), F32),
               pltpu.VMEM((nseg, 256, 256), F32), pltpu.VMEM((nseg, 256, 256), F32),
               pltpu.VMEM((nseg, 1, RW_COLS), F32),
               pltpu.VMEM((nblk, 256, 256), BF16), pltpu.VMEM((nblk, 256, 256), F32)]
    return pl.pallas_call(
        functools.partial(_mixer_kernel, c=c, c_rw=c_rw, nseg=nseg, seglen=seglen, nj=nj,
                          has_state=has_state, n_alias=len(aliases)),
        grid=(ng, nj),
        in_specs=in_specs,
        out_specs=out_specs,
        out_shape=out_shape,
        scratch_shapes=scratch,
        input_output_aliases=aliases,
        compiler_params=pltpu.CompilerParams(dimension_semantics=("arbitrary", "arbitrary"),
                                             vmem_limit_bytes=VMEM_LIMIT_BYTES),
        name="mixer",
    )(*args)


def _tile_heads(v):
    return jnp.tile(v, N_HEAD)


def _layer_params(l, lower_bounds, w_in, hg_norm_w, gla_wa2, gla_ba, gla_norm_w, rw_mu, rw_w0, rw_w2,
                  rw_a0, rw_a2, rw_g2, rw_kk, rw_ka, rw_rk, rw_ln_w, rw_ln_b):
    wmix, wg = _prep_in_weights(w_in, l)
    log_gamma = jnp.log1p(-jnp.exp2(-5.0 - jnp.arange(N_HEAD, dtype=F32)))
    rows = [lower_bounds[l], _tile_heads(hg_norm_w[l]), _tile_heads(gla_norm_w[l]), rw_w0[l], rw_a0[l],
            rw_kk[l], rw_ka[l], rw_rk[l], rw_ln_w[l], rw_ln_b[l], jnp.repeat(log_gamma, HEAD_V)]
    p256 = jnp.concatenate([jnp.stack(rows), jnp.zeros((16 - len(rows), 256), F32)], axis=0)
    zeros64 = jnp.zeros((64, 256), F32)
    return {
        'wmix': wmix,
        'log_gamma': log_gamma,
        'wg': wg,
        'p256': p256,
        'glba': jnp.concatenate([gla_ba[l][None, :], jnp.zeros((7, GLA_KW), F32)], axis=0),
        'mu': rw_mu[l][None, :],
        'wa2': jnp.concatenate([gla_wa2[l], jnp.zeros((LANES - GLA_RANK, GLA_KW), F32)], axis=0).astype(BF16),
        'w2': jnp.concatenate([rw_w2[l], zeros64], axis=0).astype(BF16),
        'a2': jnp.concatenate([zeros64, rw_a2[l]], axis=0).astype(BF16),
        'g2': rw_g2[l].astype(BF16),
    }


def _rotary_tables(pos):
    half = HEAD_V // 2
    inv = ROPE_BASE ** (-jnp.arange(half, dtype=F32) / half)
    ang = pos[:, None] * inv[None, :]
    cos, sin = jnp.cos(ang), jnp.sin(ang)
    cosf = jnp.tile(jnp.concatenate([cos, cos], axis=1), (1, N_HEAD))
    sins = jnp.tile(jnp.concatenate([-sin, sin], axis=1), (1, N_HEAD))
    return cosf, sins


def _group_layer(x, states, shift, tables, lp, attn_nw, ffn_nw, final_nw, final, c, c_rw, nseg, seglen,
                 depth, prev_outs):
    nb, length, _ = x.shape
    xf = x.reshape(nb * length, D_MODEL)
    proj = _inproj(xf, attn_nw, lp['wmix'])
    o, *outs = _mixer(proj.reshape(nb // nseg, nseg * length, N_MIX), states, shift,
                      tables[0], tables[1], lp, c, c_rw, nseg, seglen, depth, prev_outs)
    x2 = _post(xf, o.reshape(nb * length, N_BRANCH * BRANCH_W), attn_nw, lp['wg'], lp['wb'], lp['wo'],
               ffn_nw, lp['wi'], lp['wfo'], final_nw, final, lp['layer'])
    new_shift = proj.reshape(nb, length, N_MIX)[:, length - 1, C_RW:C_RW + RW_COLS]
    return x2.reshape(nb, length, D_MODEL), outs, new_shift


def _states_out(outs, shifts, depth, nb):
    n_hg, n_gl, n_rw, n_rt = outs
    return (n_hg.reshape(depth, nb, N_HEAD, HEAD_V, HEAD_V), n_gl.reshape(depth, nb, N_HEAD, GLA_DK, HEAD_V),
            n_rw.reshape(depth, nb, N_HEAD, HEAD_V, HEAD_V), jnp.stack(shifts),
            n_rt.reshape(depth, nb, N_HEAD, HEAD_V, HEAD_V))


def kernel(x_prompt, x_sample, state_hgrn, state_gla, state_rwkv, state_rwkv_shift, state_ret,
           attn_norm_w, w_in, hg_lb_logits, hg_norm_w, gla_wa2, gla_ba, gla_norm_w,
           rw_mu, rw_w0, rw_w2, rw_a0, rw_a2, rw_g2, rw_kk, rw_ka, rw_rk, rw_ln_w, rw_ln_b,
           w_branch, w_out, ffn_norm_w, w_ffn_in, w_ffn_out, final_norm_w):
    depth = w_in.shape[0]
    lb_p = jax.nn.softmax(hg_lb_logits.astype(F32), axis=0)
    lower_bounds = jnp.cumsum(lb_p, axis=0) - lb_p[0:1]

    bp, lp_len, _ = x_prompt.shape
    bs, ls_len, _ = x_sample.shape
    past_len = 16384
    tab_p = _rotary_tables(jnp.arange(lp_len, dtype=F32))
    tab_s = _rotary_tables(float(past_len) + jnp.arange(ls_len, dtype=F32))
    tab_s = tuple(jnp.tile(t, (SAMPLE_SEQS_PER_TILE, 1)) for t in tab_s)
    final_nw = final_norm_w[None, :]
    dense = {'wb': w_branch.astype(BF16), 'wo': w_out.astype(BF16),
             'wi': w_ffn_in.astype(BF16), 'wfo': w_ffn_out.astype(BF16)}

    s_in = [state_hgrn, state_gla, state_rwkv, state_ret]
    s_shift = state_rwkv_shift.reshape(depth, bs, 1, RW_COLS)
    xp, xs = x_prompt, x_sample
    def empty_states(nb):
        return [jnp.zeros((depth * nb, hdk, HEAD_V), F32) for hdk in (256, GLA_KW, 256, 256)]

    p_outs, s_outs, p_shifts, s_shifts = empty_states(bp), empty_states(bs), [], []
    for l in range(depth):
        lp = _layer_params(l, lower_bounds, w_in, hg_norm_w, gla_wa2, gla_ba, gla_norm_w, rw_mu, rw_w0,
                           rw_w2, rw_a0, rw_a2, rw_g2, rw_kk, rw_ka, rw_rk, rw_ln_w, rw_ln_b)
        lp.update(dense)
        lp['layer'] = l
        final = l == depth - 1
        anw, fnw = attn_norm_w[l][None, :], ffn_norm_w[l][None, :]
        xp, p_outs, sh = _group_layer(xp, None, None, tab_p, lp, anw, fnw, final_nw, final, c=8, c_rw=32,
                                      nseg=1, seglen=PROMPT_TILE, depth=depth, prev_outs=p_outs)
        p_shifts.append(sh)
        xs, s_outs, sh = _group_layer(xs, s_in, s_shift, tab_s, lp, anw, fnw, final_nw, final, c=ls_len,
                                      c_rw=ls_len, nseg=SAMPLE_SEQS_PER_TILE, seglen=ls_len, depth=depth,
                                      prev_outs=s_outs)
        s_shifts.append(sh)

    return (xp, xs) + _states_out(p_outs, p_shifts, depth, bp) + _states_out(s_outs, s_shifts, depth, bs)
```

```python
import functools
import math

import jax
import jax.numpy as jnp
import numpy as np
from jax import lax
from jax.experimental import pallas as pl
from jax.experimental.pallas import tpu as pltpu

F32 = jnp.float32
BF16 = jnp.bfloat16

D_MODEL = 1024
N_HEAD = 4
HEAD_V = 64
BRANCH_W = N_HEAD * HEAD_V
N_BRANCH = 4
GLA_DK = 32
GLA_KW = N_HEAD * GLA_DK
GLA_RANK = 16
GLA_TAU = 16.0
RW_COLS = 1024
NORM_EPS = 1e-6
RW_LN_EPS = 64e-5
MIN_FORGET = 1e-30
ROPE_BASE = 10000.0
LOG2E = 1.4426950408889634
D_FF = 2816
N_MIX = 3968
PROMPT_TILE = 128
SAMPLE_SEQS_PER_TILE = 8
N_GATE = N_BRANCH * D_MODEL

VMEM_LIMIT_BYTES = 56 * 1024 * 1024
LANES = 128
SUBLANES = 8
BF16_ROWS = 16

C_HG_Q, C_HG_F, C_HG_I, C_HG_G = 0, 256, 512, 768
C_GL_Q, C_GL_K, C_GL_V, C_GL_A, C_GL_G = 1024, 1152, 1280, 1536, 1664
C_RW = 1920
C_RT_Q, C_RT_K, C_RT_V, C_RT_G = 2944, 3200, 3456, 3712

P_LB, P_HG_NW, P_GL_NW, P_W0, P_A0, P_KK, P_KA, P_RK, P_LNW, P_LNB, P_RT_LG = range(11)

_NN = (((1,), (0,)), ((), ()))
_NT = (((1,), (1,)), ((), ()))
_TN = (((0,), (0,)), ((), ()))


def _dg(a, b, dn):
    return lax.dot_general(a, b, dn, preferred_element_type=F32)


def _split(x):
    hi = x.astype(BF16)
    lo = (x - hi.astype(F32)).astype(BF16)
    return hi, lo


def _split3(x):
    x1 = x.astype(BF16)
    r = x - x1.astype(F32)
    x2 = r.astype(BF16)
    x3 = (r - x2.astype(F32)).astype(BF16)
    return x1, x2, x3


def _dot1(a, b, dn=_NN):
    return _dg(a.astype(BF16), b.astype(BF16), dn)


def _dots(pieces, b_exact, dn=_NN):
    out = _dg(pieces[0], b_exact, dn)
    for p in pieces[1:]:
        out = out + _dg(p, b_exact, dn)
    return out


def _dotsr(a_exact, pieces, dn=_NN):
    out = _dg(a_exact, pieces[0], dn)
    for p in pieces[1:]:
        out = out + _dg(a_exact, p, dn)
    return out


def _headsum(x, ind):
    return _dg(x.astype(BF16), ind, _NN)


def _sigmoid(x):
    return jax.nn.sigmoid(x)


def _silu(x):
    return x * jax.nn.sigmoid(x)


def _rms(x, w):
    return x * lax.rsqrt(jnp.mean(x * x, axis=-1, keepdims=True) + NORM_EPS) * w


def _cat(parts, axis=0):
    return parts[0] if len(parts) == 1 else jnp.concatenate(parts, axis=axis)


def _by_rows(fn, rows, chunk, start=0):
    outs = [fn(slice(i, min(i + chunk, rows))) for i in range(start, rows, chunk)]
    if isinstance(outs[0], tuple):
        return tuple(_cat([o[j] for o in outs]) for j in range(len(outs[0])))
    return _cat(outs)


def _inproj_kernel(x_ref, nw_ref, w_ref, o_ref):
    h = _rms(x_ref[...], nw_ref[...]).astype(BF16)
    pad0, pad1 = C_GL_A + GLA_RANK, C_GL_A + LANES
    o_ref[:, 0:pad1] = _dg(h, w_ref[0:pad1, :], _NT)
    o_ref[:, pad1:N_MIX] = _dg(h, w_ref[pad0:GATE_COL0, :], _NT)


def _merge_body(x, o_ref, nw_ref, wg_ref, wb_ref, wo_ref):
    h = _rms(x, nw_ref[...]).astype(BF16)
    merged = None
    for b in range(N_BRANCH):
        gl = _dg(h, wg_ref[b * D_MODEL:(b + 1) * D_MODEL, :], _NT)
        ob = o_ref[:, b * BRANCH_W:(b + 1) * BRANCH_W].astype(BF16)
        up = jnp.dot(ob, wb_ref[b], preferred_element_type=F32)
        t = _sigmoid(gl) * up
        merged = t if merged is None else merged + t
    return x + jnp.dot(merged.astype(BF16), wo_ref[...], preferred_element_type=F32)


FFN_CHUNK = 1408


def _ffn_body(x, nw_ref, wi_ref, wo_ref):
    h = _rms(x, nw_ref[...]).astype(BF16)
    acc = None
    for j in range(D_FF // FFN_CHUNK):
        lo = j * FFN_CHUNK
        g = jnp.dot(h, wi_ref[:, lo:lo + FFN_CHUNK], preferred_element_type=F32)
        u = jnp.dot(h, wi_ref[:, D_FF + lo:D_FF + lo + FFN_CHUNK], preferred_element_type=F32)
        a = (_silu(g) * u).astype(BF16)
        t = jnp.dot(a, wo_ref[lo:lo + FFN_CHUNK, :], preferred_element_type=F32)
        acc = t if acc is None else acc + t
    return x + acc


def _post_kernel(x_ref, o_ref, anw_ref, wg_ref, wb_ref, wo_ref, fnw_ref, wi_ref, wfo_ref, fw_ref, out_ref,
                 *, final):
    x1 = _merge_body(x_ref[...], o_ref, anw_ref, wg_ref, wb_ref, wo_ref)
    x2 = _ffn_body(x1, fnw_ref, wi_ref, wfo_ref)
    if final:
        x2 = _rms(x2, fw_ref[...])
    out_ref[...] = x2


def _dense_params():
    return pltpu.CompilerParams(dimension_semantics=("arbitrary",), vmem_limit_bytes=VMEM_LIMIT_BYTES)


def _const_spec(shape):
    nd = len(shape)
    return pl.BlockSpec(shape, lambda i: (0,) * nd)


def _token_tile(t):
    return 512 if t % 512 == 0 else t


GATE_COL0 = 3856


def _prep_in_weights(w_in):
    wt = jnp.swapaxes(w_in, 1, 2)
    return wt[:, :GATE_COL0].astype(BF16), wt[:, GATE_COL0:].astype(BF16)


def _inproj(x, nw, wmix, l):
    t = x.shape[0]
    tm = _token_tile(t)
    return pl.pallas_call(
        _inproj_kernel,
        grid=(t // tm,),
        in_specs=[pl.BlockSpec((tm, D_MODEL), lambda i: (i, 0)),
                  _const_spec((1, D_MODEL)), pl.BlockSpec((None, GATE_COL0, D_MODEL), lambda i: (l, 0, 0))],
        out_specs=pl.BlockSpec((tm, N_MIX), lambda i: (i, 0)),
        out_shape=jax.ShapeDtypeStruct((t, N_MIX), F32),
        compiler_params=_dense_params(),
        name="inproj",
    )(x, nw, wmix)


def _post(x, o, anw, wg, wb, wo, fnw, wi, wfo, fw, final, l):
    t = x.shape[0]
    tm = _token_tile(t)
    return pl.pallas_call(
        functools.partial(_post_kernel, final=final),
        grid=(t // tm,),
        in_specs=[pl.BlockSpec((tm, D_MODEL), lambda i: (i, 0)),
                  pl.BlockSpec((tm, N_BRANCH * BRANCH_W), lambda i: (i, 0)),
                  _const_spec((1, D_MODEL)), pl.BlockSpec((None, N_GATE, D_MODEL), lambda i: (l, 0, 0)),
                  pl.BlockSpec((None, N_BRANCH, BRANCH_W, D_MODEL), lambda i: (l, 0, 0, 0)),
                  pl.BlockSpec((None, D_MODEL, D_MODEL), lambda i: (l, 0, 0)),
                  _const_spec((1, D_MODEL)), pl.BlockSpec((None, D_MODEL, 2 * D_FF), lambda i: (l, 0, 0)),
                  pl.BlockSpec((None, D_FF, D_MODEL), lambda i: (l, 0, 0)), _const_spec((1, D_MODEL))],
        out_specs=pl.BlockSpec((tm, D_MODEL), lambda i: (i, 0)),
        out_shape=jax.ShapeDtypeStruct((t, D_MODEL), F32),
        compiler_params=_dense_params(),
        name="post",
    )(x, o, anw, wg, wb, wo, fnw, wi, wfo, fw)


def _stack_heads(x, hm):
    return jnp.concatenate([x * hm[h] for h in range(N_HEAD)], axis=0)


def _state_in(s, mask):
    return jnp.concatenate([s] * N_HEAD, axis=1) * mask


def _state_out(st):
    return (st[:, 0:HEAD_V] + st[:, HEAD_V:2 * HEAD_V]) + (st[:, 2 * HEAD_V:3 * HEAD_V] + st[:, 3 * HEAD_V:])


def _gla_tile(res, q, k, v, g3, tri, ones_t, ind, hm_k, hm_v, st_ref, mask, c, nseg, seglen):
    tl = q.shape[0]
    nblk = seglen // c
    b = _dotsr(tri, g3)
    yield
    rowi = lax.broadcasted_iota(jnp.int32, (c, q.shape[1]), 0)
    group = 4 * c
    accs, qes = [], []
    for g0 in range(0, tl, group):
        pieces, vrots = [], []
        for r0 in range(g0, g0 + group, c):
            qb, kb, vb = q[r0:r0 + c], k[r0:r0 + c], v[r0:r0 + c]
            bb = b[r0:r0 + c]
            b2 = bb * LOG2E
            qes.append(qb * jnp.exp2(b2))
            pieces.append(qb * kb)
            vr = [vb]
            for d in range(1, c):
                e = jnp.exp2(b2 - pltpu.roll(b2, d, 0))
                pieces.append(jnp.where(rowi >= d, qb * pltpu.roll(kb, d, 0) * e, 0.0))
                vr.append(pltpu.roll(vb, d, 0))
            vrots.append(vr)
        a = _dot1(jnp.concatenate(pieces, axis=0), ind)
        for i, vr in enumerate(vrots):
            acc = a[i * c * c:i * c * c + c] * vr[0]
            for d in range(1, c):
                acc = acc + a[(i * c + d) * c:(i * c + d + 1) * c] * vr[d]
            accs.append(acc)
        yield
    acc = _cat(accs)
    qe = _cat(qes)
    if nblk > 1:
        scores = []
        for seg in range(nseg):
            base = seg * seglen
            for i in range(1, nblk):
                lo = base + i * c
                r = b[lo - 1:lo, :]
                qi = q[lo:lo + c] * jnp.exp(b[lo:lo + c] - r)
                ki = _by_rows(lambda s: k[s] * jnp.exp(r - b[s]), lo, 2 * BF16_ROWS, base)
                scores.append(_dot1(_stack_heads(qi, hm_k), ki, _NT))
                if i % 3 == 0:
                    yield
        parts = []
        n = 0
        for seg in range(nseg):
            base = seg * seglen
            parts.append(jnp.zeros((c, N_HEAD * HEAD_V), F32))
            for i in range(1, nblk):
                r4 = _dot1(scores[n], v[base:base + i * c])
                n += 1
                oi = r4[0:c] * hm_v[0]
                for h in range(1, N_HEAD):
                    oi = oi + r4[h * c:(h + 1) * c] * hm_v[h]
                parts.append(oi)
                if i % 3 == 0:
                    yield
        acc = acc + jnp.concatenate(parts, axis=0)
    outs = []
    for seg in range(nseg):
        lo, hi = seg * seglen, (seg + 1) * seglen
        blast = b[hi - 1:hi, :]
        dcol = jnp.exp(_dots([p[lo:hi] for p in g3], ones_t[lo:hi], _TN))
        st = st_ref[seg]
        outs.append(acc[lo:hi] + _dot1(qe[lo:hi], st))
        ke = _by_rows(lambda s: k[s] * jnp.exp(blast - b[s]), hi, 2 * BF16_ROWS, lo)
        st_ref[seg] = st * dcol + _dot1(ke, v[lo:hi], _TN) * mask
        yield
    res.append(_cat(outs))
    yield


def _ret_tile(res, q, k, v, dm, lg, rtdec, hm, st_ref, mask, nseg, seglen):
    tau = (lax.broadcasted_iota(jnp.int32, q.shape, 0) & (seglen - 1)).astype(F32)
    a = _dot1(q, _stack_heads(k, hm), _NT) * dm
    yield
    intra = _dot1(a, _stack_heads(v, hm))
    qe = q * jnp.exp((tau + 1.0) * lg)
    ke = k * jnp.exp((float(seglen) - 1.0 - tau) * lg)
    yield
    outs = []
    for seg in range(nseg):
        lo, hi = seg * seglen, (seg + 1) * seglen
        st = st_ref[seg]
        outs.append(intra[lo:hi] + _dot1(qe[lo:hi], st))
        st_ref[seg] = st * rtdec + _dot1(ke[lo:hi], v[lo:hi], _TN) * mask
    res.append(_cat(outs))
    yield


def _head_rms_gate(o, ind64, nw, gate):
    ms = _headsum(o * o, ind64) * (1.0 / HEAD_V)
    return o * lax.rsqrt(ms + NORM_EPS) * nw * _silu(gate)


def _rwkv_blocks(kt, ah, kh, rt, ap, kp, v, dec, hmasks, m256, m_scr, c_scr, c, tick):
    nblk = kt.shape[0] // c
    blocks = range(nblk)
    r4 = lax.broadcasted_iota(jnp.int32, (c, N_HEAD * c), 0)
    s4 = lax.broadcasted_iota(jnp.int32, (c, N_HEAD * c), 1) & (c - 1)
    strict = s4 < r4
    incl = s4 <= r4
    rr = lax.broadcasted_iota(jnp.int32, (N_HEAD * c, N_HEAD * c), 0)
    cc = lax.broadcasted_iota(jnp.int32, (N_HEAD * c, N_HEAD * c), 1)
    sh = int(math.log2(c))
    bd4 = ((rr >> sh) == (cc >> sh)).astype(F32)
    eye4 = (rr == cc).astype(F32)
    eye256 = (lax.broadcasted_iota(jnp.int32, (256, 256), 0) == lax.broadcasted_iota(jnp.int32, (256, 256), 1))

    def blk(x, i):
        return x[i * c:(i + 1) * c]

    def stack(x):
        return _stack_heads(x, hmasks)

    def same(size):
        s = int(math.log2(size))
        return (rr >> s) == (cc >> s)

    kr = [jnp.concatenate([blk(kt, i), blk(rt, i)], axis=0).astype(BF16) for i in blocks]
    la = [_dot1(kr[i], stack(blk(ah, i)), _NT) for i in blocks]
    lk = [_dot1(kr[i], stack(blk(kh, i)), _NT) for i in blocks]
    tick()
    l_a = [jnp.where(strict, x[:c], 0.0) for x in la]
    m_a = [jnp.where(incl, x[c:], 0.0) for x in la]
    l_k = [jnp.where(strict, x[:c], 0.0) for x in lk]
    m_k = [jnp.where(incl, x[c:], 0.0) for x in lk]
    lbd = [jnp.concatenate([x] * N_HEAD, axis=0) * bd4 for x in l_a]
    tm = [eye4 - jnp.where(same(2), x, 0.0) for x in lbd]
    size = 2
    while size < c:
        cross = same(2 * size) & jnp.logical_not(same(size))
        xm = [_dot1(tm[i], jnp.where(cross, lbd[i], 0.0)) for i in blocks]
        tick()
        tm = [tm[i] - _dot1(xm[i], tm[i]) for i in blocks]
        tick()
        size *= 2
    ts = []
    for x in tm:
        t = x[0:c]
        for h in range(1, N_HEAD):
            t = t + x[h * c:(h + 1) * c]
        ts.append(t)
    vs = [stack(blk(v, i)).astype(BF16) for i in blocks]
    lkv = [_dot1(l_k[i], vs[i]) for i in blocks]
    mkv = [_dot1(m_k[i], vs[i]) for i in blocks]
    tick()
    kbar = [_dot1(ts[i], stack(blk(kt, i))) for i in blocks]
    u0 = [-_dot1(ts[i], stack(lkv[i])) for i in blocks]
    tick()
    for i in blocks:
        m = jnp.where(eye256, dec[i * c:i * c + 1], 0.0) - _dot1(blk(ap, i), kbar[i], _TN) * m256
        m_scr[i] = m.astype(BF16)
    tick()
    for i in blocks:
        c_scr[i] = _dot1(jnp.concatenate([blk(ap, i), blk(kp, i)], axis=0),
                         jnp.concatenate([u0[i], blk(v, i)], axis=0), _TN) * m256
    return kbar, u0, m_a, mkv


N_MIXER_SHARED_INPUTS = 19


def _mixer_kernel(*refs, c, c_rw, nseg, seglen, nj, has_state, n_alias):
    refs = list(refs)
    x_ref = refs.pop(0)
    if has_state:
        shg_ref, sgl_ref, srw_ref, srt_ref, shift_ref = refs[:5]
        refs = refs[5:]
    (cos_ref, sin_ref, p256_ref, glba_ref, mu_ref, wa2_ref, w2_ref, a2_ref, g2_ref,
     tri_ref, trib_ref, onesb_ref, ind64_ref, indgl_ref, m256_ref, mgl_ref, dm_ref, rtdec_ref,
     ones_ref) = refs[:N_MIXER_SHARED_INPUTS]
    (o_ref, nhg_ref, ngl_ref, nrw_ref, nrt_ref,
     st_hg, st_gl, st_rw, st_rt, carry, m_scr, c_scr) = refs[N_MIXER_SHARED_INPUTS + n_alias:]
    tl = nseg * seglen
    nblk_seg = seglen // c_rw
    j = pl.program_id(1)

    def prm(i):
        return p256_ref[i:i + 1, :]

    tri = tri_ref[...]
    trib = trib_ref[...]
    onesb = onesb_ref[...]
    ones_t = ones_ref[...]
    ind64 = ind64_ref[...]
    indgl = indgl_ref[...]
    m256 = m256_ref[...]
    mgl = mgl_ref[...]
    lane = lax.broadcasted_iota(jnp.int32, (1, 256), 1)
    hmasks = [((lane >> 6) == h).astype(F32) for h in range(N_HEAD)]
    lane_k = lax.broadcasted_iota(jnp.int32, (1, GLA_KW), 1)
    hmasks_glk = [((lane_k >> 5) == h).astype(F32) for h in range(N_HEAD)]

    @pl.when(j == 0)
    def _():
        for seg in range(nseg):
            if has_state:
                st_hg[seg] = _state_in(shg_ref[seg].reshape(256, HEAD_V), m256)
                st_gl[seg] = _state_in(sgl_ref[seg].reshape(GLA_KW, HEAD_V), mgl)
                st_rw[seg] = _state_in(srw_ref[seg].reshape(256, HEAD_V), m256)
                st_rt[seg] = _state_in(srt_ref[seg].reshape(256, HEAD_V), m256)
                carry[seg] = shift_ref[seg]
            else:
                st_hg[seg] = jnp.zeros((256, 256), F32)
                st_gl[seg] = jnp.zeros((GLA_KW, 256), F32)
                st_rw[seg] = jnp.zeros((256, 256), F32)
                st_rt[seg] = jnp.zeros((256, 256), F32)
                carry[seg] = jnp.zeros((1, RW_COLS), F32)

    def cols(c0, w):
        return x_ref.at[:, c0:c0 + w]

    def hgrn_work(res):
        lb = prm(P_LB)
        xq, xf = cols(C_HG_Q, 256), cols(C_HG_F, 256)

        def prep(s):
            forget = lb + (1.0 - lb) * _sigmoid(xf[s])
            return (_silu(xq[s]), 1.0 - forget) + _split3(jnp.log(jnp.maximum(forget, MIN_FORGET)))

        q, k, g1, g2_, g3_ = _by_rows(prep, tl, BF16_ROWS)
        yield
        yield from _gla_tile(res, q, k, cols(C_HG_I, 256), (g1, g2_, g3_), tri, ones_t, ind64,
                             hmasks, hmasks, st_hg, m256, c, nseg, seglen)
        o_ref[:, 0:256] = _head_rms_gate(res.pop(), ind64, prm(P_HG_NW), x_ref[:, C_HG_G:C_HG_G + 256])
        yield

    def gla_work(res):
        za = _dot1(x_ref[:, C_GL_A:C_GL_A + LANES], wa2_ref[...]) + glba_ref[0:1, :]

        def prep(s):
            z = za[s]
            return _split3((jnp.minimum(z, 0.0) - jnp.log1p(jnp.exp(-jnp.abs(z)))) / GLA_TAU)

        g3 = _by_rows(prep, tl, BF16_ROWS)
        yield
        yield from _gla_tile(res, x_ref[:, C_GL_Q:C_GL_Q + GLA_KW] * GLA_DK ** -0.5,
                             x_ref[:, C_GL_K:C_GL_K + GLA_KW], cols(C_GL_V, 256), g3,
                             tri, ones_t, indgl, hmasks_glk, hmasks, st_gl, mgl, c, nseg, seglen)
        o_ref[:, 256:512] = _head_rms_gate(res.pop(), ind64, prm(P_GL_NW), x_ref[:, C_GL_G:C_GL_G + 256])
        yield

    def ret_work(res):
        upper = (lane & (HEAD_V // 2)) != 0
        xq, xk = cols(C_RT_Q, 256), cols(C_RT_K, 256)

        def rot(x, s):
            sw = jnp.where(upper, pltpu.roll(x, HEAD_V // 2, 1), pltpu.roll(x, 256 - HEAD_V // 2, 1))
            return x * cos_ref[s, :] + sw * sin_ref[s, :]

        q_r, k_r = _by_rows(lambda s: (rot(xq[s], s), rot(xk[s], s) * HEAD_V ** -0.5), tl, SUBLANES)
        yield
        yield from _ret_tile(res, q_r, k_r, x_ref[:, C_RT_V:C_RT_V + 256], dm_ref[...], prm(P_RT_LG),
                             rtdec_ref[...], hmasks, st_rt, m256, nseg, seglen)
        ort = res.pop()
        ms = _headsum(ort * ort, ind64) * (1.0 / HEAD_V)
        o_ref[:, 768:1024] = ort * lax.rsqrt(ms + NORM_EPS) * _silu(x_ref[:, C_RT_G:C_RT_G + 256])
        yield

    work = [ret_work([]), hgrn_work([]), gla_work([])]

    def advance(n):
        for _ in range(n):
            while work:
                try:
                    next(work[0])
                    break
                except StopIteration:
                    work.pop(0)

    xrw = cols(C_RW, RW_COLS)
    first = lax.broadcasted_iota(jnp.int32, (SUBLANES, RW_COLS), 0) == 0
    mu = mu_ref[...]

    def mix_rows(s):
        rw = xrw[s]
        if s.start % seglen == 0:
            prev0 = carry[s.start // seglen]
        else:
            prev0 = xrw[s.start - 1:s.start]
        prev = jnp.where(first, jnp.broadcast_to(prev0, (SUBLANES, RW_COLS)), pltpu.roll(rw, 1, 0))
        mix = rw + (prev - rw) * mu
        k0 = pltpu.roll(mix[:, 256:640], 320, 1)[:, 0:256]
        v_ = pltpu.roll(mix[:, 512:896], 320, 1)[:, 0:256]
        kk0 = k0 * prm(P_KK)
        return (mix[:, 0:256], k0, v_, jnp.tanh(mix[:, 256:384]), mix[:, 768:896],
                _sigmoid(mix[:, 896:1024]), kk0, kk0 * kk0)

    r, k0, v, t_w, s_a, s_g, kk0, kk0sq = _by_rows(mix_rows, tl, SUBLANES)
    for seg in range(nseg):
        hi = (seg + 1) * seglen
        carry[seg] = xrw[hi - 1:hi]
    lw_pre = _dot1(t_w, w2_ref[...])
    a_pre = _dot1(s_a, a2_ref[...])
    g = _dot1(s_g, g2_ref[...])
    n2 = _headsum(kk0sq, ind64)

    def gates(s):
        lw = -math.exp(-0.5) * _sigmoid(prm(P_W0) + lw_pre[s])
        a = _sigmoid(prm(P_A0) + a_pre[s])
        kk = kk0[s] / jnp.maximum(jnp.sqrt(n2[s]), 1e-12)
        k = k0[s] * (1.0 + (a - 1.0) * prm(P_KA))
        return (lw, a * kk, kk, k, r[s] * k * prm(P_RK)) + _split3(lw)

    lw, alpha, kk, k, rkp, lw1, lw2, lw3 = _by_rows(gates, tl, BF16_ROWS)
    bw = _dotsr(trib, (lw1, lw2, lw3))
    bc = _dotsr(onesb, (lw1, lw2, lw3))

    def scaled(s):
        bws, bcs = bw[s], bc[s]
        e_neg = jnp.exp(-bws)
        e_rem = jnp.exp(bcs - bws)
        return (kk[s] * jnp.exp(bws - lw[s]), alpha[s] * e_neg, k[s] * e_neg, r[s] * jnp.exp(bws),
                alpha[s] * e_rem, k[s] * e_rem, jnp.exp(bcs))

    rw_kt, rw_ah, rw_kh, rw_rt, rw_ap, rw_kp, rw_dec = _by_rows(scaled, tl, SUBLANES)
    bonus = _headsum(rkp, ind64) * v

    kbar, u0, m_a, mkv = _rwkv_blocks(rw_kt, rw_ah, rw_kh, rw_rt, rw_ap, rw_kp, v, rw_dec,
                                      hmasks, m256, m_scr, c_scr, c_rw, lambda: advance(1))

    per_step = 6
    sts = [st_rw[seg] for seg in range(nseg)]
    outs = [[None] * nblk_seg for _ in range(nseg)]
    for i in range(nblk_seg):
        for seg in range(nseg):
            blk = seg * nblk_seg + i
            sth, stl = _split(sts[seg])
            kr = jnp.concatenate([kbar[blk], rw_rt[blk * c_rw:(blk + 1) * c_rw]], axis=0).astype(BF16)
            mb = m_scr[blk]
            res = _dg(jnp.concatenate([mb, kr], axis=0), sth, _NN)
            xr = res[256:]
            u = u0[blk] - xr[:c_rw]
            outs[seg][i] = xr[c_rw:] + _dot1(m_a[blk], _stack_heads(u, hmasks)) + mkv[blk]
            sts[seg] = (res[:256] + _dg(mb, stl, _NN)) + c_scr[blk]
        advance(per_step)
    while work:
        advance(1)
    for seg in range(nseg):
        st_rw[seg] = sts[seg]
    orw = _cat([o for seg_outs in outs for o in seg_outs])
    mean = _headsum(orw, ind64) * (1.0 / HEAD_V)
    xc = orw - mean
    var = _headsum(xc * xc, ind64) * (1.0 / HEAD_V)
    o_ref[:, 512:768] = (xc * lax.rsqrt(var + RW_LN_EPS) * prm(P_LNW) + prm(P_LNB) + bonus) * g

    @pl.when(j == nj - 1)
    def _():
        for seg in range(nseg):
            nhg_ref[seg] = _state_out(st_hg[seg]).reshape(N_HEAD, HEAD_V, HEAD_V)
            ngl_ref[seg] = _state_out(st_gl[seg]).reshape(N_HEAD, GLA_DK, HEAD_V)
            nrw_ref[seg] = _state_out(st_rw[seg]).reshape(N_HEAD, HEAD_V, HEAD_V)
            nrt_ref[seg] = _state_out(st_rt[seg]).reshape(N_HEAD, HEAD_V, HEAD_V)


def _mixer_constants(nseg, seglen, c_rw, log_gamma):
    tl = nseg * seglen
    t = np.arange(tl)
    same = (t[:, None] // c_rw) == (t[None, :] // c_rw)
    same_seg = (t[:, None] // seglen) == (t[None, :] // seglen)
    causal = (t[None, :] <= t[:, None]) & same_seg
    tri = causal.astype(np.float32)
    trib = (same & causal).astype(np.float32)
    onesb = same.astype(np.float32)
    kv = np.arange(256)
    ind64 = ((kv[:, None] // HEAD_V) == (kv[None, :] // HEAD_V)).astype(np.float32)
    kg = np.arange(GLA_KW)
    indgl = ((kg[:, None] // GLA_DK) == (kv[None, :] // HEAD_V)).astype(np.float32)
    diff = jnp.asarray((t[:, None] - t[None, :]).astype(np.float32))
    dm = jnp.concatenate([jnp.where(jnp.asarray(causal), jnp.exp(diff * log_gamma[h]), 0.0)
                          for h in range(N_HEAD)], axis=1)
    rtdec = jnp.broadcast_to(jnp.exp(float(seglen) * jnp.repeat(log_gamma, HEAD_V))[:, None], (256, 256))
    ones_t = np.ones((tl, 256), np.float32)
    return (jnp.asarray(tri, BF16), jnp.asarray(trib, BF16), jnp.asarray(onesb, BF16),
            jnp.asarray(ind64, BF16), jnp.asarray(indgl, BF16), jnp.asarray(ind64, F32),
            jnp.asarray(indgl, F32), dm, rtdec, jnp.asarray(ones_t, BF16))


def _mixer(proj, states, shift, cosf, sins, lp, c, c_rw, nseg, seglen, read_state):
    ng, rows, _ = proj.shape
    tl = nseg * seglen
    nj = rows // tl
    layer = lp['layer']
    consts = _mixer_constants(nseg, seglen, c_rw, lp['log_gamma'])

    def seq_spec(w):
        return pl.BlockSpec((None, tl, w), lambda g, j: (g, j, 0))

    def const_spec(a):
        nd = a.ndim
        return pl.BlockSpec(a.shape, lambda g, j: (0,) * nd)

    pos_spec = pl.BlockSpec((tl, 256), lambda g, j: (j, 0))
    params = (lp['p256'], lp['glba'], lp['mu'], lp['wa2'], lp['w2'], lp['a2'], lp['g2'])
    state_specs = [pl.BlockSpec((None, nseg) + s.shape[2:], lambda g, j: (layer, g, 0, 0, 0)) for s in states]
    in_specs = [seq_spec(N_MIX)]
    args = [proj]
    if read_state:
        in_specs += state_specs + [pl.BlockSpec((None, nseg, 1, RW_COLS), lambda g, j: (layer, g, 0, 0))]
        args += list(states) + [shift]
        aliases = {1 + i: 1 + i for i in range(len(states))}
    in_specs += [pos_spec, pos_spec] + [const_spec(a) for a in params] + [const_spec(a) for a in consts]
    args += [cosf, sins, *params, *consts]
    if not read_state:
        aliases = {len(args) + i: 1 + i for i in range(len(states))}
        in_specs += [pl.BlockSpec(memory_space=pl.ANY)] * len(states)
        args += list(states)
    out_specs = [seq_spec(N_BRANCH * BRANCH_W)] + state_specs
    out_shape = ([jax.ShapeDtypeStruct((ng, rows, N_BRANCH * BRANCH_W), F32)]
                 + [jax.ShapeDtypeStruct(s.shape, F32) for s in states])
    nblk = tl // c_rw
    scratch = [pltpu.VMEM((nseg, 256, 256), F32), pltpu.VMEM((nseg, GLA_KW, 256), F32),
               pltpu.VMEM((nseg, 256, 256), F32), pltpu.VMEM((nseg, 256, 256), F32),
               pltpu.VMEM((nseg, 1, RW_COLS), F32),
               pltpu.VMEM((nblk, 256, 256), BF16), pltpu.VMEM((nblk, 256, 256), F32)]
    return pl.pallas_call(
        functools.partial(_mixer_kernel, c=c, c_rw=c_rw, nseg=nseg, seglen=seglen, nj=nj,
                          has_state=read_state, n_alias=0 if read_state else len(states)),
        grid=(ng, nj),
        in_specs=in_specs,
        out_specs=out_specs,
        out_shape=out_shape,
        scratch_shapes=scratch,
        input_output_aliases=aliases,
        compiler_params=pltpu.CompilerParams(dimension_semantics=("arbitrary", "arbitrary"),
                                             vmem_limit_bytes=VMEM_LIMIT_BYTES),
        name="mixer",
    )(*args)


def _tile_heads(v):
    return jnp.tile(v, N_HEAD)


def _layer_params(l, lower_bounds, hg_norm_w, gla_wa2, gla_ba, gla_norm_w, rw_mu, rw_w0, rw_w2,
                  rw_a0, rw_a2, rw_g2, rw_kk, rw_ka, rw_rk, rw_ln_w, rw_ln_b):
    log_gamma = jnp.log1p(-jnp.exp2(-5.0 - jnp.arange(N_HEAD, dtype=F32)))
    rows = [lower_bounds[l], _tile_heads(hg_norm_w[l]), _tile_heads(gla_norm_w[l]), rw_w0[l], rw_a0[l],
            rw_kk[l], rw_ka[l], rw_rk[l], rw_ln_w[l], rw_ln_b[l], jnp.repeat(log_gamma, HEAD_V)]
    p256 = jnp.concatenate([jnp.stack(rows), jnp.zeros((16 - len(rows), 256), F32)], axis=0)
    zeros64 = jnp.zeros((64, 256), F32)
    return {
        'log_gamma': log_gamma,
        'p256': p256,
        'glba': jnp.concatenate([gla_ba[l][None, :], jnp.zeros((7, GLA_KW), F32)], axis=0),
        'mu': rw_mu[l][None, :],
        'wa2': jnp.concatenate([gla_wa2[l], jnp.zeros((LANES - GLA_RANK, GLA_KW), F32)], axis=0).astype(BF16),
        'w2': jnp.concatenate([rw_w2[l], zeros64], axis=0).astype(BF16),
        'a2': jnp.concatenate([zeros64, rw_a2[l]], axis=0).astype(BF16),
        'g2': rw_g2[l].astype(BF16),
    }


def _rotary_tables(pos):
    half = HEAD_V // 2
    inv = ROPE_BASE ** (-jnp.arange(half, dtype=F32) / half)
    ang = pos[:, None] * inv[None, :]
    cos, sin = jnp.cos(ang), jnp.sin(ang)
    cosf = jnp.tile(jnp.concatenate([cos, cos], axis=1), (1, N_HEAD))
    sins = jnp.tile(jnp.concatenate([-sin, sin], axis=1), (1, N_HEAD))
    return cosf, sins


def _group_layer(x, states, shift, tables, lp, attn_nw, ffn_nw, final_nw, final, c, c_rw, nseg, seglen,
                 read_state):
    nb, length, _ = x.shape
    xf = x.reshape(nb * length, D_MODEL)
    proj = _inproj(xf, attn_nw, lp['wmix'], lp['layer'])
    o, *outs = _mixer(proj.reshape(nb // nseg, nseg * length, N_MIX), states, shift,
                      tables[0], tables[1], lp, c, c_rw, nseg, seglen, read_state)
    x2 = _post(xf, o.reshape(nb * length, N_BRANCH * BRANCH_W), attn_nw, lp['wg'], lp['wb'], lp['wo'],
               ffn_nw, lp['wi'], lp['wfo'], final_nw, final, lp['layer'])
    new_shift = proj.reshape(nb, length, N_MIX)[:, length - 1, C_RW:C_RW + RW_COLS]
    return x2.reshape(nb, length, D_MODEL), outs, new_shift


def _states_out(outs, shifts):
    n_hg, n_gl, n_rw, n_rt = outs
    return (n_hg, n_gl, n_rw, jnp.stack(shifts), n_rt)


def kernel(x_prompt, x_sample, state_hgrn, state_gla, state_rwkv, state_rwkv_shift, state_ret,
           attn_norm_w, w_in, hg_lb_logits, hg_norm_w, gla_wa2, gla_ba, gla_norm_w,
           rw_mu, rw_w0, rw_w2, rw_a0, rw_a2, rw_g2, rw_kk, rw_ka, rw_rk, rw_ln_w, rw_ln_b,
           w_branch, w_out, ffn_norm_w, w_ffn_in, w_ffn_out, final_norm_w):
    depth = w_in.shape[0]
    lb_p = jax.nn.softmax(hg_lb_logits.astype(F32), axis=0)
    lower_bounds = jnp.cumsum(lb_p, axis=0) - lb_p[0:1]

    bp, lp_len, _ = x_prompt.shape
    bs, ls_len, _ = x_sample.shape
    past_len = 16384
    tab_p = _rotary_tables(jnp.arange(lp_len, dtype=F32))
    tab_s = _rotary_tables(float(past_len) + jnp.arange(ls_len, dtype=F32))
    tab_s = tuple(jnp.tile(t, (SAMPLE_SEQS_PER_TILE, 1)) for t in tab_s)
    final_nw = final_norm_w[None, :]
    wmix, wg = _prep_in_weights(w_in)
    dense = {'wmix': wmix, 'wg': wg, 'wb': w_branch.astype(BF16), 'wo': w_out.astype(BF16),
             'wi': w_ffn_in.astype(BF16), 'wfo': w_ffn_out.astype(BF16)}

    s_states = [state_hgrn, state_gla, state_rwkv, state_ret]
    p_states = [jnp.zeros((depth, bp) + s.shape[2:], F32) for s in s_states]
    s_shift = state_rwkv_shift.reshape(depth, bs, 1, RW_COLS)
    xp, xs = x_prompt, x_sample
    p_shifts, s_shifts = [], []
    for l in range(depth):
        lp = _layer_params(l, lower_bounds, hg_norm_w, gla_wa2, gla_ba, gla_norm_w, rw_mu, rw_w0,
                           rw_w2, rw_a0, rw_a2, rw_g2, rw_kk, rw_ka, rw_rk, rw_ln_w, rw_ln_b)
        lp.update(dense)
        lp['layer'] = l
        final = l == depth - 1
        anw, fnw = attn_norm_w[l][None, :], ffn_norm_w[l][None, :]
        xp, p_states, sh = _group_layer(xp, p_states, None, tab_p, lp, anw, fnw, final_nw, final, c=8,
                                        c_rw=32, nseg=1, seglen=PROMPT_TILE, read_state=False)
        p_shifts.append(sh)
        xs, s_states, sh = _group_layer(xs, s_states, s_shift, tab_s, lp, anw, fnw, final_nw, final,
                                        c=ls_len, c_rw=ls_len, nseg=SAMPLE_SEQS_PER_TILE, seglen=ls_len,
                                        read_state=True)
        s_shifts.append(sh)

    return (xp, xs) + _states_out(p_states, p_shifts) + _states_out(s_states, s_shifts)
```

```python
import functools
import math

import jax
import jax.numpy as jnp
import numpy as np
from jax import lax
from jax.experimental import pallas as pl
from jax.experimental.pallas import tpu as pltpu

F32 = jnp.float32
BF16 = jnp.bfloat16

D_MODEL = 1024
N_HEAD = 4
HEAD_V = 64
BRANCH_W = N_HEAD * HEAD_V
N_BRANCH = 4
GLA_DK = 32
GLA_KW = N_HEAD * GLA_DK
GLA_RANK = 16
GLA_TAU = 16.0
RW_COLS = 1024
NORM_EPS = 1e-6
RW_LN_EPS = 64e-5
MIN_FORGET = 1e-30
ROPE_BASE = 10000.0
LOG2E = 1.4426950408889634
D_FF = 2816
N_MIX = 3968
PROMPT_TILE = 128
SAMPLE_SEQS_PER_TILE = 8
N_GATE = N_BRANCH * D_MODEL

VMEM_LIMIT_BYTES = 56 * 1024 * 1024
LANES = 128
SUBLANES = 8
BF16_ROWS = 16

C_HG_Q, C_HG_F, C_HG_I, C_HG_G = 0, 256, 512, 768
C_GL_Q, C_GL_K, C_GL_V, C_GL_A, C_GL_G = 1024, 1152, 1280, 1536, 1664
C_RW = 1920
C_RT_Q, C_RT_K, C_RT_V, C_RT_G = 2944, 3200, 3456, 3712

P_LB, P_HG_NW, P_GL_NW, P_W0, P_A0, P_KK, P_KA, P_RK, P_LNW, P_LNB, P_RT_LG = range(11)

_NN = (((1,), (0,)), ((), ()))
_NT = (((1,), (1,)), ((), ()))
_TN = (((0,), (0,)), ((), ()))


def _dg(a, b, dn):
    return lax.dot_general(a, b, dn, preferred_element_type=F32)


def _split(x):
    hi = x.astype(BF16)
    lo = (x - hi.astype(F32)).astype(BF16)
    return hi, lo


def _split3(x):
    x1 = x.astype(BF16)
    r = x - x1.astype(F32)
    x2 = r.astype(BF16)
    x3 = (r - x2.astype(F32)).astype(BF16)
    return x1, x2, x3


def _dot1(a, b, dn=_NN):
    return _dg(a.astype(BF16), b.astype(BF16), dn)


def _dots(pieces, b_exact, dn=_NN):
    out = _dg(pieces[0], b_exact, dn)
    for p in pieces[1:]:
        out = out + _dg(p, b_exact, dn)
    return out


def _dotsr(a_exact, pieces, dn=_NN):
    out = _dg(a_exact, pieces[0], dn)
    for p in pieces[1:]:
        out = out + _dg(a_exact, p, dn)
    return out


def _headsum(x, ind):
    return _dg(x.astype(BF16), ind, _NN)


def _sigmoid(x):
    return jax.nn.sigmoid(x)


def _silu(x):
    return x * jax.nn.sigmoid(x)


def _rms(x, w):
    return x * lax.rsqrt(jnp.mean(x * x, axis=-1, keepdims=True) + NORM_EPS) * w


def _cat(parts, axis=0):
    return parts[0] if len(parts) == 1 else jnp.concatenate(parts, axis=axis)


def _by_rows(fn, rows, chunk, start=0):
    outs = [fn(slice(i, min(i + chunk, rows))) for i in range(start, rows, chunk)]
    if isinstance(outs[0], tuple):
        return tuple(_cat([o[j] for o in outs]) for j in range(len(outs[0])))
    return _cat(outs)


def _inproj_kernel(x_ref, nw_ref, w_ref, o_ref):
    h = _rms(x_ref[...], nw_ref[...]).astype(BF16)
    pad0, pad1 = C_GL_A + GLA_RANK, C_GL_A + LANES
    o_ref[:, 0:pad1] = _dg(h, w_ref[0:pad1, :], _NT)
    o_ref[:, pad1:N_MIX] = _dg(h, w_ref[pad0:GATE_COL0, :], _NT)


def _merge_body(x, o_ref, nw_ref, wg_ref, wb_ref, wo_ref):
    h = _rms(x, nw_ref[...]).astype(BF16)
    merged = None
    for b in range(N_BRANCH):
        gl = _dg(h, wg_ref[b * D_MODEL:(b + 1) * D_MODEL, :], _NT)
        ob = o_ref[:, b * BRANCH_W:(b + 1) * BRANCH_W].astype(BF16)
        up = jnp.dot(ob, wb_ref[b], preferred_element_type=F32)
        t = _sigmoid(gl) * up
        merged = t if merged is None else merged + t
    return x + jnp.dot(merged.astype(BF16), wo_ref[...], preferred_element_type=F32)


FFN_CHUNK = 1408


def _ffn_body(x, nw_ref, wi_ref, wo_ref):
    h = _rms(x, nw_ref[...]).astype(BF16)
    acc = None
    for j in range(D_FF // FFN_CHUNK):
        lo = j * FFN_CHUNK
        g = jnp.dot(h, wi_ref[:, lo:lo + FFN_CHUNK], preferred_element_type=F32)
        u = jnp.dot(h, wi_ref[:, D_FF + lo:D_FF + lo + FFN_CHUNK], preferred_element_type=F32)
        a = (_silu(g) * u).astype(BF16)
        t = jnp.dot(a, wo_ref[lo:lo + FFN_CHUNK, :], preferred_element_type=F32)
        acc = t if acc is None else acc + t
    return x + acc


def _post_kernel(x_ref, o_ref, anw_ref, wg_ref, wb_ref, wo_ref, fnw_ref, wi_ref, wfo_ref, fw_ref, out_ref,
                 *, final):
    x1 = _merge_body(x_ref[...], o_ref, anw_ref, wg_ref, wb_ref, wo_ref)
    x2 = _ffn_body(x1, fnw_ref, wi_ref, wfo_ref)
    if final:
        x2 = _rms(x2, fw_ref[...])
    out_ref[...] = x2


def _dense_params():
    return pltpu.CompilerParams(dimension_semantics=("arbitrary",), vmem_limit_bytes=VMEM_LIMIT_BYTES)


def _const_spec(shape):
    nd = len(shape)
    return pl.BlockSpec(shape, lambda i: (0,) * nd)


def _token_tile(t):
    return 512 if t % 512 == 0 else t


GATE_COL0 = 3856


def _prep_in_weights(w_in):
    wt = jnp.swapaxes(w_in, 1, 2).astype(BF16)
    return wt, wt[:, GATE_COL0:]


def _inproj(x, nw, wmix, l):
    t = x.shape[0]
    tm = _token_tile(t)
    return pl.pallas_call(
        _inproj_kernel,
        grid=(t // tm,),
        in_specs=[pl.BlockSpec((tm, D_MODEL), lambda i: (i, 0)),
                  _const_spec((1, D_MODEL)), pl.BlockSpec((None, GATE_COL0, D_MODEL), lambda i: (l, 0, 0))],
        out_specs=pl.BlockSpec((tm, N_MIX), lambda i: (i, 0)),
        out_shape=jax.ShapeDtypeStruct((t, N_MIX), F32),
        compiler_params=_dense_params(),
        name="inproj",
    )(x, nw, wmix)


def _post(x, o, anw, wg, wb, wo, fnw, wi, wfo, fw, final, l):
    t = x.shape[0]
    tm = _token_tile(t)
    return pl.pallas_call(
        functools.partial(_post_kernel, final=final),
        grid=(t // tm,),
        in_specs=[pl.BlockSpec((tm, D_MODEL), lambda i: (i, 0)),
                  pl.BlockSpec((tm, N_BRANCH * BRANCH_W), lambda i: (i, 0)),
                  _const_spec((1, D_MODEL)), pl.BlockSpec((None, N_GATE, D_MODEL), lambda i: (l, 0, 0)),
                  pl.BlockSpec((None, N_BRANCH, BRANCH_W, D_MODEL), lambda i: (l, 0, 0, 0)),
                  pl.BlockSpec((None, D_MODEL, D_MODEL), lambda i: (l, 0, 0)),
                  _const_spec((1, D_MODEL)), pl.BlockSpec((None, D_MODEL, 2 * D_FF), lambda i: (l, 0, 0)),
                  pl.BlockSpec((None, D_FF, D_MODEL), lambda i: (l, 0, 0)), _const_spec((1, D_MODEL))],
        out_specs=pl.BlockSpec((tm, D_MODEL), lambda i: (i, 0)),
        out_shape=jax.ShapeDtypeStruct((t, D_MODEL), F32),
        compiler_params=_dense_params(),
        name="post",
    )(x, o, anw, wg, wb, wo, fnw, wi, wfo, fw)


def _stack_heads(x, hm):
    if x.shape[0] % BF16_ROWS == 0:
        x = x.astype(BF16)
        return jnp.concatenate([x * hm[h].astype(BF16) for h in range(N_HEAD)], axis=0)
    return jnp.concatenate([x * hm[h] for h in range(N_HEAD)], axis=0)


def _state_in(s, mask):
    return jnp.concatenate([s] * N_HEAD, axis=1) * mask


def _state_out(st):
    return (st[:, 0:HEAD_V] + st[:, HEAD_V:2 * HEAD_V]) + (st[:, 2 * HEAD_V:3 * HEAD_V] + st[:, 3 * HEAD_V:])


def _gla_tile(res, q, k, v, g3, tri, ones_t, ind, hm_k, hm_v, st_ref, mask, c, nseg, seglen):
    tl = q.shape[0]
    nblk = seglen // c
    b = _dotsr(tri, g3)
    yield
    rowi = lax.broadcasted_iota(jnp.int32, (c, q.shape[1]), 0)
    group = 4 * c
    accs, qes = [], []
    for g0 in range(0, tl, group):
        pieces, vrots = [], []
        for r0 in range(g0, g0 + group, c):
            qb, kb, vb = q[r0:r0 + c], k[r0:r0 + c], v[r0:r0 + c]
            bb = b[r0:r0 + c]
            b2 = bb * LOG2E
            qes.append(qb * jnp.exp2(b2))
            pieces.append(qb * kb)
            vr = [vb]
            for d in range(1, c):
                e = jnp.exp2(b2 - pltpu.roll(b2, d, 0))
                pieces.append(jnp.where(rowi >= d, qb * pltpu.roll(kb, d, 0) * e, 0.0))
                vr.append(pltpu.roll(vb, d, 0))
            vrots.append(vr)
        a = _dot1(jnp.concatenate(pieces, axis=0), ind)
        for i, vr in enumerate(vrots):
            acc = a[i * c * c:i * c * c + c] * vr[0]
            for d in range(1, c):
                acc = acc + a[(i * c + d) * c:(i * c + d + 1) * c] * vr[d]
            accs.append(acc)
        yield
    acc = _cat(accs)
    qe = _cat(qes)
    if nblk > 1:
        scores = []
        for seg in range(nseg):
            base = seg * seglen
            for i in range(1, nblk):
                lo = base + i * c
                r = b[lo - 1:lo, :]
                qi = q[lo:lo + c] * jnp.exp(b[lo:lo + c] - r)
                ki = _by_rows(lambda s: k[s] * jnp.exp(r - b[s]), lo, 2 * BF16_ROWS, base)
                scores.append(_dot1(_stack_heads(qi, hm_k), ki, _NT))
                if i % 3 == 0:
                    yield
        parts = []
        n = 0
        for seg in range(nseg):
            base = seg * seglen
            parts.append(jnp.zeros((c, N_HEAD * HEAD_V), F32))
            for i in range(1, nblk):
                r4 = _dot1(scores[n], v[base:base + i * c])
                n += 1
                oi = r4[0:c] * hm_v[0]
                for h in range(1, N_HEAD):
                    oi = oi + r4[h * c:(h + 1) * c] * hm_v[h]
                parts.append(oi)
                if i % 3 == 0:
                    yield
        acc = acc + jnp.concatenate(parts, axis=0)
    outs = []
    for seg in range(nseg):
        lo, hi = seg * seglen, (seg + 1) * seglen
        blast = b[hi - 1:hi, :]
        dcol = jnp.exp(_dots([p[lo:hi] for p in g3], ones_t[lo:hi], _TN))
        st = st_ref[seg]
        outs.append(acc[lo:hi] + _dot1(qe[lo:hi], st))
        ke = _by_rows(lambda s: k[s] * jnp.exp(blast - b[s]), hi, 2 * BF16_ROWS, lo)
        st_ref[seg] = st * dcol + _dot1(ke, v[lo:hi], _TN) * mask
        yield
    res.append(_cat(outs))
    yield


def _ret_tile(res, q, k, v, dm, lg, rtdec, hm, st_ref, mask, nseg, seglen):
    tau = (lax.broadcasted_iota(jnp.int32, q.shape, 0) & (seglen - 1)).astype(F32)
    a = _dot1(q, _stack_heads(k, hm), _NT) * dm
    yield
    intra = _dot1(a, _stack_heads(v, hm))
    qe = q * jnp.exp((tau + 1.0) * lg)
    ke = k * jnp.exp((float(seglen) - 1.0 - tau) * lg)
    yield
    outs = []
    for seg in range(nseg):
        lo, hi = seg * seglen, (seg + 1) * seglen
        st = st_ref[seg]
        outs.append(intra[lo:hi] + _dot1(qe[lo:hi], st))
        st_ref[seg] = st * rtdec + _dot1(ke[lo:hi], v[lo:hi], _TN) * mask
    res.append(_cat(outs))
    yield


def _head_rms_gate(o, ind64, nw, gate):
    ms = _headsum(o * o, ind64) * (1.0 / HEAD_V)
    return o * lax.rsqrt(ms + NORM_EPS) * nw * _silu(gate)


def _rwkv_blocks(kt, ah, kh, rt, ap, kp, v, dec, hmasks, m256, m_scr, c_scr, c, tick):
    nblk = kt.shape[0] // c
    blocks = range(nblk)
    r4 = lax.broadcasted_iota(jnp.int32, (c, N_HEAD * c), 0)
    s4 = lax.broadcasted_iota(jnp.int32, (c, N_HEAD * c), 1) & (c - 1)
    strict = s4 < r4
    incl = s4 <= r4
    rr = lax.broadcasted_iota(jnp.int32, (N_HEAD * c, N_HEAD * c), 0)
    cc = lax.broadcasted_iota(jnp.int32, (N_HEAD * c, N_HEAD * c), 1)
    sh = int(math.log2(c))
    bd4 = ((rr >> sh) == (cc >> sh)).astype(F32)
    eye4 = (rr == cc).astype(F32)
    eye256 = (lax.broadcasted_iota(jnp.int32, (256, 256), 0) == lax.broadcasted_iota(jnp.int32, (256, 256), 1))

    def blk(x, i):
        return x[i * c:(i + 1) * c]

    def stack(x):
        return _stack_heads(x, hmasks)

    def same(size):
        s = int(math.log2(size))
        return (rr >> s) == (cc >> s)

    kr = [jnp.concatenate([blk(kt, i), blk(rt, i)], axis=0).astype(BF16) for i in blocks]
    la = [_dot1(kr[i], stack(blk(ah, i)), _NT) for i in blocks]
    lk = [_dot1(kr[i], stack(blk(kh, i)), _NT) for i in blocks]
    tick()
    l_a = [jnp.where(strict, x[:c], 0.0) for x in la]
    m_a = [jnp.where(incl, x[c:], 0.0) for x in la]
    l_k = [jnp.where(strict, x[:c], 0.0) for x in lk]
    m_k = [jnp.where(incl, x[c:], 0.0) for x in lk]
    lbd = [jnp.concatenate([x] * N_HEAD, axis=0) * bd4 for x in l_a]
    tm = [eye4 - jnp.where(same(2), x, 0.0) for x in lbd]
    size = 2
    while size < c:
        cross = same(2 * size) & jnp.logical_not(same(size))
        xm = [_dot1(tm[i], jnp.where(cross, lbd[i], 0.0)) for i in blocks]
        tick()
        tm = [tm[i] - _dot1(xm[i], tm[i]) for i in blocks]
        tick()
        size *= 2
    ts = []
    for x in tm:
        t = x[0:c]
        for h in range(1, N_HEAD):
            t = t + x[h * c:(h + 1) * c]
        ts.append(t)
    vs = [stack(blk(v, i)).astype(BF16) for i in blocks]
    lkv = [_dot1(l_k[i], vs[i]) for i in blocks]
    mkv = [_dot1(m_k[i], vs[i]) for i in blocks]
    tick()
    kbar = [_dot1(ts[i], stack(blk(kt, i))) for i in blocks]
    u0 = [-_dot1(ts[i], stack(lkv[i])) for i in blocks]
    tick()
    for i in blocks:
        m = jnp.where(eye256, dec[i * c:i * c + 1], 0.0) - _dot1(blk(ap, i), kbar[i], _TN) * m256
        m_scr[i] = m.astype(BF16)
    tick()
    for i in blocks:
        c_scr[i] = _dot1(jnp.concatenate([blk(ap, i), blk(kp, i)], axis=0),
                         jnp.concatenate([u0[i], blk(v, i)], axis=0), _TN) * m256
    return kbar, u0, m_a, mkv


N_MIXER_SHARED_INPUTS = 19


def _mixer_kernel(*refs, c, c_rw, nseg, seglen, nj, has_state, n_alias):
    refs = list(refs)
    x_ref = refs.pop(0)
    if has_state:
        shg_ref, sgl_ref, srw_ref, srt_ref, shift_ref = refs[:5]
        refs = refs[5:]
    (cos_ref, sin_ref, p256_ref, glba_ref, mu_ref, wa2_ref, w2_ref, a2_ref, g2_ref,
     tri_ref, trib_ref, onesb_ref, ind64_ref, indgl_ref, m256_ref, mgl_ref, dm_ref, rtdec_ref,
     ones_ref) = refs[:N_MIXER_SHARED_INPUTS]
    (o_ref, nhg_ref, ngl_ref, nrw_ref, nrt_ref,
     st_hg, st_gl, st_rw, st_rt, carry, m_scr, c_scr) = refs[N_MIXER_SHARED_INPUTS + n_alias:]
    tl = nseg * seglen
    nblk_seg = seglen // c_rw
    j = pl.program_id(1)

    def prm(i):
        return p256_ref[i:i + 1, :]

    tri = tri_ref[...]
    trib = trib_ref[...]
    onesb = onesb_ref[...]
    ones_t = ones_ref[...]
    ind64 = ind64_ref[...]
    indgl = indgl_ref[...]
    m256 = m256_ref[...]
    mgl = mgl_ref[...]
    lane = lax.broadcasted_iota(jnp.int32, (1, 256), 1)
    hmasks = [((lane >> 6) == h).astype(F32) for h in range(N_HEAD)]
    lane_k = lax.broadcasted_iota(jnp.int32, (1, GLA_KW), 1)
    hmasks_glk = [((lane_k >> 5) == h).astype(F32) for h in range(N_HEAD)]

    @pl.when(j == 0)
    def _():
        for seg in range(nseg):
            if has_state:
                st_hg[seg] = _state_in(shg_ref[seg].reshape(256, HEAD_V), m256)
                st_gl[seg] = _state_in(sgl_ref[seg].reshape(GLA_KW, HEAD_V), mgl)
                st_rw[seg] = _state_in(srw_ref[seg].reshape(256, HEAD_V), m256)
                st_rt[seg] = _state_in(srt_ref[seg].reshape(256, HEAD_V), m256)
                carry[seg] = shift_ref[seg]
            else:
                st_hg[seg] = jnp.zeros((256, 256), F32)
                st_gl[seg] = jnp.zeros((GLA_KW, 256), F32)
                st_rw[seg] = jnp.zeros((256, 256), F32)
                st_rt[seg] = jnp.zeros((256, 256), F32)
                carry[seg] = jnp.zeros((1, RW_COLS), F32)

    def cols(c0, w):
        return x_ref.at[:, c0:c0 + w]

    def hgrn_work(res):
        lb = prm(P_LB)
        xq, xf = cols(C_HG_Q, 256), cols(C_HG_F, 256)

        def prep(s):
            forget = lb + (1.0 - lb) * _sigmoid(xf[s])
            return (_silu(xq[s]), 1.0 - forget) + _split3(jnp.log(jnp.maximum(forget, MIN_FORGET)))

        q, k, g1, g2_, g3_ = _by_rows(prep, tl, BF16_ROWS)
        yield
        yield from _gla_tile(res, q, k, cols(C_HG_I, 256), (g1, g2_, g3_), tri, ones_t, ind64,
                             hmasks, hmasks, st_hg, m256, c, nseg, seglen)
        o_ref[:, 0:256] = _head_rms_gate(res.pop(), ind64, prm(P_HG_NW), x_ref[:, C_HG_G:C_HG_G + 256])
        yield

    def gla_work(res):
        za = _dot1(x_ref[:, C_GL_A:C_GL_A + LANES], wa2_ref[...]) + glba_ref[0:1, :]

        def prep(s):
            z = za[s]
            return _split3((jnp.minimum(z, 0.0) - jnp.log1p(jnp.exp(-jnp.abs(z)))) / GLA_TAU)

        g3 = _by_rows(prep, tl, BF16_ROWS)
        yield
        yield from _gla_tile(res, x_ref[:, C_GL_Q:C_GL_Q + GLA_KW] * GLA_DK ** -0.5,
                             x_ref[:, C_GL_K:C_GL_K + GLA_KW], cols(C_GL_V, 256), g3,
                             tri, ones_t, indgl, hmasks_glk, hmasks, st_gl, mgl, c, nseg, seglen)
        o_ref[:, 256:512] = _head_rms_gate(res.pop(), ind64, prm(P_GL_NW), x_ref[:, C_GL_G:C_GL_G + 256])
        yield

    def ret_work(res):
        upper = (lane & (HEAD_V // 2)) != 0
        xq, xk = cols(C_RT_Q, 256), cols(C_RT_K, 256)

        def rot(x, s):
            sw = jnp.where(upper, pltpu.roll(x, HEAD_V // 2, 1), pltpu.roll(x, 256 - HEAD_V // 2, 1))
            return x * cos_ref[s, :] + sw * sin_ref[s, :]

        q_r, k_r = _by_rows(lambda s: (rot(xq[s], s), rot(xk[s], s) * HEAD_V ** -0.5), tl, SUBLANES)
        yield
        yield from _ret_tile(res, q_r, k_r, x_ref[:, C_RT_V:C_RT_V + 256], dm_ref[...], prm(P_RT_LG),
                             rtdec_ref[...], hmasks, st_rt, m256, nseg, seglen)
        ort = res.pop()
        ms = _headsum(ort * ort, ind64) * (1.0 / HEAD_V)
        o_ref[:, 768:1024] = ort * lax.rsqrt(ms + NORM_EPS) * _silu(x_ref[:, C_RT_G:C_RT_G + 256])
        yield

    work = [ret_work([]), hgrn_work([]), gla_work([])]

    def advance(n):
        for _ in range(n):
            while work:
                try:
                    next(work[0])
                    break
                except StopIteration:
                    work.pop(0)

    xrw = cols(C_RW, RW_COLS)
    first = lax.broadcasted_iota(jnp.int32, (SUBLANES, RW_COLS), 0) == 0
    mu = mu_ref[...]

    def mix_rows(s):
        rw = xrw[s]
        if s.start % seglen == 0:
            prev0 = carry[s.start // seglen]
        else:
            prev0 = xrw[s.start - 1:s.start]
        prev = jnp.where(first, jnp.broadcast_to(prev0, (SUBLANES, RW_COLS)), pltpu.roll(rw, 1, 0))
        mix = rw + (prev - rw) * mu
        k0 = pltpu.roll(mix[:, 256:640], 320, 1)[:, 0:256]
        v_ = pltpu.roll(mix[:, 512:896], 320, 1)[:, 0:256]
        kk0 = k0 * prm(P_KK)
        return (mix[:, 0:256], k0, v_, jnp.tanh(mix[:, 256:384]), mix[:, 768:896],
                _sigmoid(mix[:, 896:1024]), kk0, kk0 * kk0)

    r, k0, v, t_w, s_a, s_g, kk0, kk0sq = _by_rows(mix_rows, tl, SUBLANES)
    for seg in range(nseg):
        hi = (seg + 1) * seglen
        carry[seg] = xrw[hi - 1:hi]
    lw_pre = _dot1(t_w, w2_ref[...])
    a_pre = _dot1(s_a, a2_ref[...])
    g = _dot1(s_g, g2_ref[...])
    n2 = _headsum(kk0sq, ind64)

    def gates(s):
        lw = -math.exp(-0.5) * _sigmoid(prm(P_W0) + lw_pre[s])
        a = _sigmoid(prm(P_A0) + a_pre[s])
        kk = kk0[s] / jnp.maximum(jnp.sqrt(n2[s]), 1e-12)
        k = k0[s] * (1.0 + (a - 1.0) * prm(P_KA))
        return (lw, a * kk, kk, k, r[s] * k * prm(P_RK)) + _split3(lw)

    lw, alpha, kk, k, rkp, lw1, lw2, lw3 = _by_rows(gates, tl, BF16_ROWS)
    bw = _dotsr(trib, (lw1, lw2, lw3))
    bc = _dotsr(onesb, (lw1, lw2, lw3))

    def scaled(s):
        bws, bcs = bw[s], bc[s]
        e_neg = jnp.exp(-bws)
        e_rem = jnp.exp(bcs - bws)
        return (kk[s] * jnp.exp(bws - lw[s]), alpha[s] * e_neg, k[s] * e_neg, r[s] * jnp.exp(bws),
                alpha[s] * e_rem, k[s] * e_rem, jnp.exp(bcs))

    rw_kt, rw_ah, rw_kh, rw_rt, rw_ap, rw_kp, rw_dec = _by_rows(scaled, tl, SUBLANES)
    bonus = _headsum(rkp, ind64) * v

    kbar, u0, m_a, mkv = _rwkv_blocks(rw_kt, rw_ah, rw_kh, rw_rt, rw_ap, rw_kp, v, rw_dec,
                                      hmasks, m256, m_scr, c_scr, c_rw, lambda: advance(1))

    per_step = 6
    sts = [st_rw[seg] for seg in range(nseg)]
    outs = [[None] * nblk_seg for _ in range(nseg)]
    for i in range(nblk_seg):
        for seg in range(nseg):
            blk = seg * nblk_seg + i
            sth, stl = _split(sts[seg])
            kr = jnp.concatenate([kbar[blk], rw_rt[blk * c_rw:(blk + 1) * c_rw]], axis=0).astype(BF16)
            mb = m_scr[blk]
            res = _dg(jnp.concatenate([mb, kr], axis=0), sth, _NN)
            xr = res[256:]
            u = u0[blk] - xr[:c_rw]
            outs[seg][i] = xr[c_rw:] + _dot1(m_a[blk], _stack_heads(u, hmasks)) + mkv[blk]
            sts[seg] = (res[:256] + _dg(mb, stl, _NN)) + c_scr[blk]
        advance(per_step)
    while work:
        advance(1)
    for seg in range(nseg):
        st_rw[seg] = sts[seg]
    orw = _cat([o for seg_outs in outs for o in seg_outs])
    mean = _headsum(orw, ind64) * (1.0 / HEAD_V)
    xc = orw - mean
    var = _headsum(xc * xc, ind64) * (1.0 / HEAD_V)
    o_ref[:, 512:768] = (xc * lax.rsqrt(var + RW_LN_EPS) * prm(P_LNW) + prm(P_LNB) + bonus) * g

    @pl.when(j == nj - 1)
    def _():
        for seg in range(nseg):
            nhg_ref[seg] = _state_out(st_hg[seg]).reshape(N_HEAD, HEAD_V, HEAD_V)
            ngl_ref[seg] = _state_out(st_gl[seg]).reshape(N_HEAD, GLA_DK, HEAD_V)
            nrw_ref[seg] = _state_out(st_rw[seg]).reshape(N_HEAD, HEAD_V, HEAD_V)
            nrt_ref[seg] = _state_out(st_rt[seg]).reshape(N_HEAD, HEAD_V, HEAD_V)


def _mixer_constants(nseg, seglen, c_rw, log_gamma):
    tl = nseg * seglen
    t = np.arange(tl)
    same = (t[:, None] // c_rw) == (t[None, :] // c_rw)
    same_seg = (t[:, None] // seglen) == (t[None, :] // seglen)
    causal = (t[None, :] <= t[:, None]) & same_seg
    tri = causal.astype(np.float32)
    trib = (same & causal).astype(np.float32)
    onesb = same.astype(np.float32)
    kv = np.arange(256)
    ind64 = ((kv[:, None] // HEAD_V) == (kv[None, :] // HEAD_V)).astype(np.float32)
    kg = np.arange(GLA_KW)
    indgl = ((kg[:, None] // GLA_DK) == (kv[None, :] // HEAD_V)).astype(np.float32)
    diff = jnp.asarray((t[:, None] - t[None, :]).astype(np.float32))
    dm = jnp.concatenate([jnp.where(jnp.asarray(causal), jnp.exp(diff * log_gamma[h]), 0.0)
                          for h in range(N_HEAD)], axis=1)
    rtdec = jnp.broadcast_to(jnp.exp(float(seglen) * jnp.repeat(log_gamma, HEAD_V))[:, None], (256, 256))
    ones_t = np.ones((tl, 256), np.float32)
    return (jnp.asarray(tri, BF16), jnp.asarray(trib, BF16), jnp.asarray(onesb, BF16),
            jnp.asarray(ind64, BF16), jnp.asarray(indgl, BF16), jnp.asarray(ind64, F32),
            jnp.asarray(indgl, F32), dm, rtdec, jnp.asarray(ones_t, BF16))


def _mixer(proj, states, shift, cosf, sins, lp, c, c_rw, nseg, seglen, read_state):
    ng, rows, _ = proj.shape
    tl = nseg * seglen
    nj = rows // tl
    layer = lp['layer']
    consts = _mixer_constants(nseg, seglen, c_rw, lp['log_gamma'])

    def seq_spec(w):
        return pl.BlockSpec((None, tl, w), lambda g, j: (g, j, 0))

    def const_spec(a):
        nd = a.ndim
        return pl.BlockSpec(a.shape, lambda g, j: (0,) * nd)

    pos_spec = pl.BlockSpec((tl, 256), lambda g, j: (j, 0))
    params = (lp['p256'], lp['glba'], lp['mu'], lp['wa2'], lp['w2'], lp['a2'], lp['g2'])
    state_specs = [pl.BlockSpec((None, nseg) + s.shape[2:], lambda g, j: (layer, g, 0, 0, 0)) for s in states]
    in_specs = [seq_spec(N_MIX)]
    args = [proj]
    if read_state:
        in_specs += state_specs + [pl.BlockSpec((None, nseg, 1, RW_COLS), lambda g, j: (layer, g, 0, 0))]
        args += list(states) + [shift]
        aliases = {1 + i: 1 + i for i in range(len(states))}
    in_specs += [pos_spec, pos_spec] + [const_spec(a) for a in params] + [const_spec(a) for a in consts]
    args += [cosf, sins, *params, *consts]
    if not read_state:
        aliases = {len(args) + i: 1 + i for i in range(len(states))}
        in_specs += [pl.BlockSpec(memory_space=pl.ANY)] * len(states)
        args += list(states)
    out_specs = [seq_spec(N_BRANCH * BRANCH_W)] + state_specs
    out_shape = ([jax.ShapeDtypeStruct((ng, rows, N_BRANCH * BRANCH_W), F32)]
                 + [jax.ShapeDtypeStruct(s.shape, F32) for s in states])
    nblk = tl // c_rw
    scratch = [pltpu.VMEM((nseg, 256, 256), F32), pltpu.VMEM((nseg, GLA_KW, 256), F32),
               pltpu.VMEM((nseg, 256, 256), F32), pltpu.VMEM((nseg, 256, 256), F32),
               pltpu.VMEM((nseg, 1, RW_COLS), F32),
               pltpu.VMEM((nblk, 256, 256), BF16), pltpu.VMEM((nblk, 256, 256), F32)]
    return pl.pallas_call(
        functools.partial(_mixer_kernel, c=c, c_rw=c_rw, nseg=nseg, seglen=seglen, nj=nj,
                          has_state=read_state, n_alias=0 if read_state else len(states)),
        grid=(ng, nj),
        in_specs=in_specs,
        out_specs=out_specs,
        out_shape=out_shape,
        scratch_shapes=scratch,
        input_output_aliases=aliases,
        compiler_params=pltpu.CompilerParams(dimension_semantics=("arbitrary", "arbitrary"),
                                             vmem_limit_bytes=VMEM_LIMIT_BYTES),
        name="mixer",
    )(*args)


def _tile_heads(v):
    return jnp.tile(v, N_HEAD)


def _layer_params(l, lower_bounds, hg_norm_w, gla_wa2, gla_ba, gla_norm_w, rw_mu, rw_w0, rw_w2,
                  rw_a0, rw_a2, rw_g2, rw_kk, rw_ka, rw_rk, rw_ln_w, rw_ln_b):
    log_gamma = jnp.log1p(-jnp.exp2(-5.0 - jnp.arange(N_HEAD, dtype=F32)))
    rows = [lower_bounds[l], _tile_heads(hg_norm_w[l]), _tile_heads(gla_norm_w[l]), rw_w0[l], rw_a0[l],
            rw_kk[l], rw_ka[l], rw_rk[l], rw_ln_w[l], rw_ln_b[l], jnp.repeat(log_gamma, HEAD_V)]
    p256 = jnp.concatenate([jnp.stack(rows), jnp.zeros((16 - len(rows), 256), F32)], axis=0)
    zeros64 = jnp.zeros((64, 256), F32)
    return {
        'log_gamma': log_gamma,
        'p256': p256,
        'glba': jnp.concatenate([gla_ba[l][None, :], jnp.zeros((7, GLA_KW), F32)], axis=0),
        'mu': rw_mu[l][None, :],
        'wa2': jnp.concatenate([gla_wa2[l], jnp.zeros((LANES - GLA_RANK, GLA_KW), F32)], axis=0).astype(BF16),
        'w2': jnp.concatenate([rw_w2[l], zeros64], axis=0).astype(BF16),
        'a2': jnp.concatenate([zeros64, rw_a2[l]], axis=0).astype(BF16),
        'g2': rw_g2[l].astype(BF16),
    }


def _rotary_tables(pos):
    half = HEAD_V // 2
    inv = ROPE_BASE ** (-jnp.arange(half, dtype=F32) / half)
    ang = pos[:, None] * inv[None, :]
    cos, sin = jnp.cos(ang), jnp.sin(ang)
    cosf = jnp.tile(jnp.concatenate([cos, cos], axis=1), (1, N_HEAD))
    sins = jnp.tile(jnp.concatenate([-sin, sin], axis=1), (1, N_HEAD))
    return cosf, sins


def _group_layer(x, states, shift, tables, lp, attn_nw, ffn_nw, final_nw, final, c, c_rw, nseg, seglen,
                 read_state):
    nb, length, _ = x.shape
    xf = x.reshape(nb * length, D_MODEL)
    proj = _inproj(xf, attn_nw, lp['wmix'], lp['layer'])
    o, *outs = _mixer(proj.reshape(nb // nseg, nseg * length, N_MIX), states, shift,
                      tables[0], tables[1], lp, c, c_rw, nseg, seglen, read_state)
    x2 = _post(xf, o.reshape(nb * length, N_BRANCH * BRANCH_W), attn_nw, lp['wg'], lp['wb'], lp['wo'],
               ffn_nw, lp['wi'], lp['wfo'], final_nw, final, lp['layer'])
    new_shift = proj.reshape(nb, length, N_MIX)[:, length - 1, C_RW:C_RW + RW_COLS]
    return x2.reshape(nb, length, D_MODEL), outs, new_shift


def _states_out(outs, shifts):
    n_hg, n_gl, n_rw, n_rt = outs
    return (n_hg, n_gl, n_rw, jnp.stack(shifts), n_rt)


def kernel(x_prompt, x_sample, state_hgrn, state_gla, state_rwkv, state_rwkv_shift, state_ret,
           attn_norm_w, w_in, hg_lb_logits, hg_norm_w, gla_wa2, gla_ba, gla_norm_w,
           rw_mu, rw_w0, rw_w2, rw_a0, rw_a2, rw_g2, rw_kk, rw_ka, rw_rk, rw_ln_w, rw_ln_b,
           w_branch, w_out, ffn_norm_w, w_ffn_in, w_ffn_out, final_norm_w):
    depth = w_in.shape[0]
    lb_p = jax.nn.softmax(hg_lb_logits.astype(F32), axis=0)
    lower_bounds = jnp.cumsum(lb_p, axis=0) - lb_p[0:1]

    bp, lp_len, _ = x_prompt.shape
    bs, ls_len, _ = x_sample.shape
    past_len = 16384
    tab_p = _rotary_tables(jnp.arange(lp_len, dtype=F32))
    tab_s = _rotary_tables(float(past_len) + jnp.arange(ls_len, dtype=F32))
    tab_s = tuple(jnp.tile(t, (SAMPLE_SEQS_PER_TILE, 1)) for t in tab_s)
    final_nw = final_norm_w[None, :]
    wmix, wg = _prep_in_weights(w_in)
    dense = {'wmix': wmix, 'wg': wg, 'wb': w_branch.astype(BF16), 'wo': w_out.astype(BF16),
             'wi': w_ffn_in.astype(BF16), 'wfo': w_ffn_out.astype(BF16)}

    s_states = [state_hgrn, state_gla, state_rwkv, state_ret]
    p_states = [jnp.zeros((depth, bp) + s.shape[2:], F32) for s in s_states]
    s_shift = state_rwkv_shift.reshape(depth, bs, 1, RW_COLS)
    xp, xs = x_prompt, x_sample
    p_shifts, s_shifts = [], []
    for l in range(depth):
        lp = _layer_params(l, lower_bounds, hg_norm_w, gla_wa2, gla_ba, gla_norm_w, rw_mu, rw_w0,
                           rw_w2, rw_a0, rw_a2, rw_g2, rw_kk, rw_ka, rw_rk, rw_ln_w, rw_ln_b)
        lp.update(dense)
        lp['layer'] = l
        final = l == depth - 1
        anw, fnw = attn_norm_w[l][None, :], ffn_norm_w[l][None, :]
        xp, p_states, sh = _group_layer(xp, p_states, None, tab_p, lp, anw, fnw, final_nw, final, c=8,
                                        c_rw=32, nseg=1, seglen=PROMPT_TILE, read_state=False)
        p_shifts.append(sh)
        xs, s_states, sh = _group_layer(xs, s_states, s_shift, tab_s, lp, anw, fnw, final_nw, final,
                                        c=ls_len, c_rw=ls_len, nseg=SAMPLE_SEQS_PER_TILE, seglen=ls_len,
                                        read_state=True)
        s_shifts.append(sh)

    return (xp, xs) + _states_out(p_states, p_shifts) + _states_out(s_states, s_shifts)
```

```python
import functools
import math

import jax
import jax.numpy as jnp
import numpy as np
from jax import lax
from jax.experimental import pallas as pl
from jax.experimental.pallas import tpu as pltpu

F32 = jnp.float32
BF16 = jnp.bfloat16

D_MODEL = 1024
N_HEAD = 4
HEAD_V = 64
BRANCH_W = N_HEAD * HEAD_V
N_BRANCH = 4
GLA_DK = 32
GLA_KW = N_HEAD * GLA_DK
GLA_RANK = 16
GLA_TAU = 16.0
RW_COLS = 1024
NORM_EPS = 1e-6
RW_LN_EPS = 64e-5
MIN_FORGET = 1e-30
ROPE_BASE = 10000.0
PAST_LEN = 16384
LOG2E = 1.4426950408889634
D_FF = 2816
N_MIX = 3968
PROMPT_TILE = 128
SAMPLE_SEQS_PER_TILE = 8
N_GATE = N_BRANCH * D_MODEL

VMEM_LIMIT_BYTES = 56 * 1024 * 1024
LANES = 128
SUBLANES = 8
BF16_ROWS = 16

C_HG_Q, C_HG_F, C_HG_I, C_HG_G = 0, 256, 512, 768
C_GL_Q, C_GL_K, C_GL_V, C_GL_A, C_GL_G = 1024, 1152, 1280, 1536, 1664
C_RW = 1920
C_RT_Q, C_RT_K, C_RT_V, C_RT_G = 2944, 3200, 3456, 3712

P_LB, P_HG_NW, P_GL_NW, P_W0, P_A0, P_KK, P_KA, P_RK, P_LNW, P_LNB, P_RT_LG = range(11)

_NN = (((1,), (0,)), ((), ()))
_NT = (((1,), (1,)), ((), ()))
_TN = (((0,), (0,)), ((), ()))


def _dg(a, b, dn):
    return lax.dot_general(a, b, dn, preferred_element_type=F32)


def _split(x):
    hi = x.astype(BF16)
    lo = (x - hi.astype(F32)).astype(BF16)
    return hi, lo


def _split3(x):
    x1 = x.astype(BF16)
    r = x - x1.astype(F32)
    x2 = r.astype(BF16)
    x3 = (r - x2.astype(F32)).astype(BF16)
    return x1, x2, x3


def _dot1(a, b, dn=_NN):
    return _dg(a.astype(BF16), b.astype(BF16), dn)


def _dots(pieces, b_exact, dn=_NN):
    out = _dg(pieces[0], b_exact, dn)
    for p in pieces[1:]:
        out = out + _dg(p, b_exact, dn)
    return out


def _dotsr(a_exact, pieces, dn=_NN):
    out = _dg(a_exact, pieces[0], dn)
    for p in pieces[1:]:
        out = out + _dg(a_exact, p, dn)
    return out


def _headsum(x, ind):
    return _dg(x.astype(BF16), ind, _NN)


def _sigmoid(x):
    return jax.nn.sigmoid(x)


def _silu(x):
    return x * jax.nn.sigmoid(x)


def _rms(x, w):
    return x * lax.rsqrt(jnp.mean(x * x, axis=-1, keepdims=True) + NORM_EPS) * w


def _cat(parts, axis=0):
    return parts[0] if len(parts) == 1 else jnp.concatenate(parts, axis=axis)


def _by_rows(fn, rows, chunk, start=0):
    outs = [fn(slice(i, min(i + chunk, rows))) for i in range(start, rows, chunk)]
    if isinstance(outs[0], tuple):
        return tuple(_cat([o[j] for o in outs]) for j in range(len(outs[0])))
    return _cat(outs)


def _inproj_kernel(x_ref, nw_ref, w_ref, o_ref):
    h = _rms(x_ref[...], nw_ref[...]).astype(BF16)
    pad0, pad1 = C_GL_A + GLA_RANK, C_GL_A + LANES
    o_ref[:, 0:pad1] = _dg(h, w_ref[0:pad1, :], _NT)
    o_ref[:, pad1:N_MIX] = _dg(h, w_ref[pad0:GATE_COL0, :], _NT)


def _merge_body(x, o_ref, nw_ref, wg_ref, wb_ref, wo_ref):
    h = _rms(x, nw_ref[...]).astype(BF16)
    merged = None
    for b in range(N_BRANCH):
        gl = _dg(h, wg_ref[b * D_MODEL:(b + 1) * D_MODEL, :], _NT)
        ob = o_ref[:, b * BRANCH_W:(b + 1) * BRANCH_W].astype(BF16)
        up = jnp.dot(ob, wb_ref[b], preferred_element_type=F32)
        t = _sigmoid(gl) * up
        merged = t if merged is None else merged + t
    return x + jnp.dot(merged.astype(BF16), wo_ref[...], preferred_element_type=F32)


FFN_CHUNK = 1408


def _ffn_body(x, nw_ref, wi_ref, wo_ref):
    h = _rms(x, nw_ref[...]).astype(BF16)
    acc = None
    for j in range(D_FF // FFN_CHUNK):
        lo = j * FFN_CHUNK
        g = jnp.dot(h, wi_ref[:, lo:lo + FFN_CHUNK], preferred_element_type=F32)
        u = jnp.dot(h, wi_ref[:, D_FF + lo:D_FF + lo + FFN_CHUNK], preferred_element_type=F32)
        a = (_silu(g) * u).astype(BF16)
        t = jnp.dot(a, wo_ref[lo:lo + FFN_CHUNK, :], preferred_element_type=F32)
        acc = t if acc is None else acc + t
    return x + acc


def _post_kernel(x_ref, o_ref, anw_ref, wg_ref, wb_ref, wo_ref, fnw_ref, wi_ref, wfo_ref, fw_ref, out_ref,
                 *, final):
    x1 = _merge_body(x_ref[...], o_ref, anw_ref, wg_ref, wb_ref, wo_ref)
    x2 = _ffn_body(x1, fnw_ref, wi_ref, wfo_ref)
    if final:
        x2 = _rms(x2, fw_ref[...])
    out_ref[...] = x2


def _dense_params():
    return pltpu.CompilerParams(dimension_semantics=("arbitrary",), vmem_limit_bytes=VMEM_LIMIT_BYTES)


def _const_spec(shape):
    nd = len(shape)
    return pl.BlockSpec(shape, lambda i: (0,) * nd)


def _token_tile(t):
    return 512 if t % 512 == 0 else t


GATE_COL0 = 3856
INPROJ_TILE = 1024


def _prep_in_weights(w_in):
    wt = jnp.swapaxes(w_in, 1, 2).astype(BF16)
    return wt, wt[:, GATE_COL0:]


def _inproj(x, nw, wmix, l):
    t = x.shape[0]
    tm = INPROJ_TILE if t % INPROJ_TILE == 0 and t > INPROJ_TILE else _token_tile(t)
    return pl.pallas_call(
        _inproj_kernel,
        grid=(t // tm,),
        in_specs=[pl.BlockSpec((tm, D_MODEL), lambda i: (i, 0)),
                  _const_spec((1, D_MODEL)), pl.BlockSpec((None, GATE_COL0, D_MODEL), lambda i: (l, 0, 0))],
        out_specs=pl.BlockSpec((tm, N_MIX), lambda i: (i, 0)),
        out_shape=jax.ShapeDtypeStruct((t, N_MIX), F32),
        compiler_params=_dense_params(),
        name="inproj",
    )(x, nw, wmix)


def _post(x, o, anw, wg, wb, wo, fnw, wi, wfo, fw, final, l):
    t = x.shape[0]
    tm = _token_tile(t)
    return pl.pallas_call(
        functools.partial(_post_kernel, final=final),
        grid=(t // tm,),
        in_specs=[pl.BlockSpec((tm, D_MODEL), lambda i: (i, 0)),
                  pl.BlockSpec((tm, N_BRANCH * BRANCH_W), lambda i: (i, 0)),
                  _const_spec((1, D_MODEL)), pl.BlockSpec((None, N_GATE, D_MODEL), lambda i: (l, 0, 0)),
                  pl.BlockSpec((None, N_BRANCH, BRANCH_W, D_MODEL), lambda i: (l, 0, 0, 0)),
                  pl.BlockSpec((None, D_MODEL, D_MODEL), lambda i: (l, 0, 0)),
                  _const_spec((1, D_MODEL)), pl.BlockSpec((None, D_MODEL, 2 * D_FF), lambda i: (l, 0, 0)),
                  pl.BlockSpec((None, D_FF, D_MODEL), lambda i: (l, 0, 0)), _const_spec((1, D_MODEL))],
        out_specs=pl.BlockSpec((tm, D_MODEL), lambda i: (i, 0)),
        out_shape=jax.ShapeDtypeStruct((t, D_MODEL), F32),
        compiler_params=_dense_params(),
        name="post",
    )(x, o, anw, wg, wb, wo, fnw, wi, wfo, fw)


def _stack_heads(x, hm):
    if x.shape[0] % BF16_ROWS == 0:
        x = x.astype(BF16)
        return jnp.concatenate([x * hm[h].astype(BF16) for h in range(N_HEAD)], axis=0)
    return jnp.concatenate([x * hm[h] for h in range(N_HEAD)], axis=0)


def _state_in(s, mask):
    return jnp.concatenate([s] * N_HEAD, axis=1) * mask


def _state_out(st):
    return (st[:, 0:HEAD_V] + st[:, HEAD_V:2 * HEAD_V]) + (st[:, 2 * HEAD_V:3 * HEAD_V] + st[:, 3 * HEAD_V:])


def _gla_tile(res, q, k, v, g3, tri, ones_t, ind, hm_k, hm_v, st_ref, mask, c, nseg, seglen):
    tl = q.shape[0]
    nblk = seglen // c
    b = _dotsr(tri, g3)
    yield
    rowi = lax.broadcasted_iota(jnp.int32, (c, q.shape[1]), 0)
    group = 4 * c
    accs, qes = [], []
    for g0 in range(0, tl, group):
        pieces, vrots = [], []
        for r0 in range(g0, g0 + group, c):
            qb, kb, vb = q[r0:r0 + c], k[r0:r0 + c], v[r0:r0 + c]
            bb = b[r0:r0 + c]
            b2 = bb * LOG2E
            qes.append(qb * jnp.exp2(b2))
            pieces.append(qb * kb)
            vr = [vb]
            for d in range(1, c):
                e = jnp.exp2(b2 - pltpu.roll(b2, d, 0))
                pieces.append(jnp.where(rowi >= d, qb * pltpu.roll(kb, d, 0) * e, 0.0))
                vr.append(pltpu.roll(vb, d, 0))
            vrots.append(vr)
        a = _dot1(jnp.concatenate(pieces, axis=0), ind)
        for i, vr in enumerate(vrots):
            acc = a[i * c * c:i * c * c + c] * vr[0]
            for d in range(1, c):
                acc = acc + a[(i * c + d) * c:(i * c + d + 1) * c] * vr[d]
            accs.append(acc)
        yield
    acc = _cat(accs)
    qe = _cat(qes)
    if nblk > 1:
        scores = []
        for seg in range(nseg):
            base = seg * seglen
            for i in range(1, nblk):
                lo = base + i * c
                r = b[lo - 1:lo, :]
                qi = q[lo:lo + c] * jnp.exp(b[lo:lo + c] - r)
                ki = _by_rows(lambda s: k[s] * jnp.exp(r - b[s]), lo, 2 * BF16_ROWS, base)
                scores.append(_dot1(_stack_heads(qi, hm_k), ki, _NT))
                if i % 3 == 0:
                    yield
        parts = []
        n = 0
        for seg in range(nseg):
            base = seg * seglen
            parts.append(jnp.zeros((c, N_HEAD * HEAD_V), F32))
            for i in range(1, nblk):
                r4 = _dot1(scores[n], v[base:base + i * c])
                n += 1
                oi = r4[0:c] * hm_v[0]
                for h in range(1, N_HEAD):
                    oi = oi + r4[h * c:(h + 1) * c] * hm_v[h]
                parts.append(oi)
                if i % 3 == 0:
                    yield
        acc = acc + jnp.concatenate(parts, axis=0)
    outs = []
    for seg in range(nseg):
        lo, hi = seg * seglen, (seg + 1) * seglen
        blast = b[hi - 1:hi, :]
        dcol = jnp.exp(_dots([p[lo:hi] for p in g3], ones_t[lo:hi], _TN))
        st = st_ref[seg]
        outs.append(acc[lo:hi] + _dot1(qe[lo:hi], st))
        ke = _by_rows(lambda s: k[s] * jnp.exp(blast - b[s]), hi, 2 * BF16_ROWS, lo)
        st_ref[seg] = st * dcol + _dot1(ke, v[lo:hi], _TN) * mask
        yield
    res.append(_cat(outs))
    yield


def _ret_tile(res, q, k, v, dm, lg, rtdec, hm, st_ref, mask, nseg, seglen):
    tau = (lax.broadcasted_iota(jnp.int32, q.shape, 0) & (seglen - 1)).astype(F32)
    a = _dot1(q, _stack_heads(k, hm), _NT) * dm
    yield
    intra = _dot1(a, _stack_heads(v, hm))
    qe = q * jnp.exp((tau + 1.0) * lg)
    ke = k * jnp.exp((float(seglen) - 1.0 - tau) * lg)
    yield
    outs = []
    for seg in range(nseg):
        lo, hi = seg * seglen, (seg + 1) * seglen
        st = st_ref[seg]
        outs.append(intra[lo:hi] + _dot1(qe[lo:hi], st))
        st_ref[seg] = st * rtdec + _dot1(ke[lo:hi], v[lo:hi], _TN) * mask
    res.append(_cat(outs))
    yield


def _head_rms_gate(o, ind64, nw, gate):
    ms = _headsum(o * o, ind64) * (1.0 / HEAD_V)
    return o * lax.rsqrt(ms + NORM_EPS) * nw * _silu(gate)


def _rwkv_blocks(kt, ah, kh, rt, ap, kp, v, dec, hmasks, m256, m_scr, c_scr, c, tick):
    nblk = kt.shape[0] // c
    blocks = range(nblk)
    r4 = lax.broadcasted_iota(jnp.int32, (c, N_HEAD * c), 0)
    s4 = lax.broadcasted_iota(jnp.int32, (c, N_HEAD * c), 1) & (c - 1)
    strict = s4 < r4
    incl = s4 <= r4
    rr = lax.broadcasted_iota(jnp.int32, (N_HEAD * c, N_HEAD * c), 0)
    cc = lax.broadcasted_iota(jnp.int32, (N_HEAD * c, N_HEAD * c), 1)
    sh = int(math.log2(c))
    bd4 = ((rr >> sh) == (cc >> sh)).astype(F32)
    eye4 = (rr == cc).astype(F32)
    eye256 = (lax.broadcasted_iota(jnp.int32, (256, 256), 0) == lax.broadcasted_iota(jnp.int32, (256, 256), 1))

    def blk(x, i):
        return x[i * c:(i + 1) * c]

    def stack(x):
        return _stack_heads(x, hmasks)

    def same(size):
        s = int(math.log2(size))
        return (rr >> s) == (cc >> s)

    kr = [jnp.concatenate([blk(kt, i), blk(rt, i)], axis=0).astype(BF16) for i in blocks]
    la = [_dot1(kr[i], stack(blk(ah, i)), _NT) for i in blocks]
    lk = [_dot1(kr[i], stack(blk(kh, i)), _NT) for i in blocks]
    tick()
    l_a = [jnp.where(strict, x[:c], 0.0) for x in la]
    m_a = [jnp.where(incl, x[c:], 0.0) for x in la]
    l_k = [jnp.where(strict, x[:c], 0.0) for x in lk]
    m_k = [jnp.where(incl, x[c:], 0.0) for x in lk]
    lbd = [jnp.concatenate([x] * N_HEAD, axis=0) * bd4 for x in l_a]
    tm = [eye4 - jnp.where(same(2), x, 0.0) for x in lbd]
    size = 2
    while size < c:
        cross = same(2 * size) & jnp.logical_not(same(size))
        xm = [_dot1(tm[i], jnp.where(cross, lbd[i], 0.0)) for i in blocks]
        tick()
        tm = [tm[i] - _dot1(xm[i], tm[i]) for i in blocks]
        tick()
        size *= 2
    ts = []
    for x in tm:
        t = x[0:c]
        for h in range(1, N_HEAD):
            t = t + x[h * c:(h + 1) * c]
        ts.append(t)
    vs = [stack(blk(v, i)).astype(BF16) for i in blocks]
    lkv = [_dot1(l_k[i], vs[i]) for i in blocks]
    mkv = [_dot1(m_k[i], vs[i]) for i in blocks]
    tick()
    kbar = [_dot1(ts[i], stack(blk(kt, i))) for i in blocks]
    u0 = [-_dot1(ts[i], stack(lkv[i])) for i in blocks]
    tick()
    for i in blocks:
        m = jnp.where(eye256, dec[i * c:i * c + 1], 0.0) - _dot1(blk(ap, i), kbar[i], _TN) * m256
        m_scr[i] = m.astype(BF16)
    tick()
    for i in blocks:
        c_scr[i] = _dot1(jnp.concatenate([blk(ap, i), blk(kp, i)], axis=0),
                         jnp.concatenate([u0[i], blk(v, i)], axis=0), _TN) * m256
    return kbar, u0, m_a, mkv


N_MIXER_SHARED_INPUTS = 19


def _mixer_kernel(*refs, c, c_rw, nseg, seglen, nj, has_state, n_alias):
    refs = list(refs)
    x_ref = refs.pop(0)
    if has_state:
        shg_ref, sgl_ref, srw_ref, srt_ref, shift_ref = refs[:5]
        refs = refs[5:]
    (cos_ref, sin_ref, p256_ref, glba_ref, mu_ref, wa2_ref, w2_ref, a2_ref, g2_ref,
     tri_ref, trib_ref, onesb_ref, ind64_ref, indgl_ref, m256_ref, mgl_ref, dm_ref, rtdec_ref,
     ones_ref) = refs[:N_MIXER_SHARED_INPUTS]
    (o_ref, nhg_ref, ngl_ref, nrw_ref, nrt_ref,
     st_hg, st_gl, st_rw, st_rt, carry, m_scr, c_scr) = refs[N_MIXER_SHARED_INPUTS + n_alias:]
    tl = nseg * seglen
    nblk_seg = seglen // c_rw
    j = pl.program_id(1)

    def prm(i):
        return p256_ref[i:i + 1, :]

    tri = tri_ref[...]
    trib = trib_ref[...]
    onesb = onesb_ref[...]
    ones_t = ones_ref[...]
    ind64 = ind64_ref[...]
    indgl = indgl_ref[...]
    m256 = m256_ref[...]
    mgl = mgl_ref[...]
    lane = lax.broadcasted_iota(jnp.int32, (1, 256), 1)
    hmasks = [((lane >> 6) == h).astype(F32) for h in range(N_HEAD)]
    lane_k = lax.broadcasted_iota(jnp.int32, (1, GLA_KW), 1)
    hmasks_glk = [((lane_k >> 5) == h).astype(F32) for h in range(N_HEAD)]

    @pl.when(j == 0)
    def _():
        for seg in range(nseg):
            if has_state:
                st_hg[seg] = _state_in(shg_ref[seg].reshape(256, HEAD_V), m256)
                st_gl[seg] = _state_in(sgl_ref[seg].reshape(GLA_KW, HEAD_V), mgl)
                st_rw[seg] = _state_in(srw_ref[seg].reshape(256, HEAD_V), m256)
                st_rt[seg] = _state_in(srt_ref[seg].reshape(256, HEAD_V), m256)
                carry[seg] = shift_ref[seg]
            else:
                st_hg[seg] = jnp.zeros((256, 256), F32)
                st_gl[seg] = jnp.zeros((GLA_KW, 256), F32)
                st_rw[seg] = jnp.zeros((256, 256), F32)
                st_rt[seg] = jnp.zeros((256, 256), F32)
                carry[seg] = jnp.zeros((1, RW_COLS), F32)

    def cols(c0, w):
        return x_ref.at[:, c0:c0 + w]

    def hgrn_work(res):
        lb = prm(P_LB)
        xq, xf = cols(C_HG_Q, 256), cols(C_HG_F, 256)

        def prep(s):
            forget = lb + (1.0 - lb) * _sigmoid(xf[s])
            return (_silu(xq[s]), 1.0 - forget) + _split3(jnp.log(jnp.maximum(forget, MIN_FORGET)))

        q, k, g1, g2_, g3_ = _by_rows(prep, tl, BF16_ROWS)
        yield
        yield from _gla_tile(res, q, k, cols(C_HG_I, 256), (g1, g2_, g3_), tri, ones_t, ind64,
                             hmasks, hmasks, st_hg, m256, c, nseg, seglen)
        o_ref[:, 0:256] = _head_rms_gate(res.pop(), ind64, prm(P_HG_NW), x_ref[:, C_HG_G:C_HG_G + 256])
        yield

    def gla_work(res):
        za = _dot1(x_ref[:, C_GL_A:C_GL_A + LANES], wa2_ref[...]) + glba_ref[0:1, :]

        def prep(s):
            z = za[s]
            return _split3((jnp.minimum(z, 0.0) - jnp.log1p(jnp.exp(-jnp.abs(z)))) / GLA_TAU)

        g3 = _by_rows(prep, tl, BF16_ROWS)
        yield
        yield from _gla_tile(res, x_ref[:, C_GL_Q:C_GL_Q + GLA_KW] * GLA_DK ** -0.5,
                             x_ref[:, C_GL_K:C_GL_K + GLA_KW], cols(C_GL_V, 256), g3,
                             tri, ones_t, indgl, hmasks_glk, hmasks, st_gl, mgl, c, nseg, seglen)
        o_ref[:, 256:512] = _head_rms_gate(res.pop(), ind64, prm(P_GL_NW), x_ref[:, C_GL_G:C_GL_G + 256])
        yield

    def ret_work(res):
        upper = (lane & (HEAD_V // 2)) != 0
        xq, xk = cols(C_RT_Q, 256), cols(C_RT_K, 256)

        def rot(x, s):
            sw = jnp.where(upper, pltpu.roll(x, HEAD_V // 2, 1), pltpu.roll(x, 256 - HEAD_V // 2, 1))
            return x * cos_ref[s, :] + sw * sin_ref[s, :]

        q_r, k_r = _by_rows(lambda s: (rot(xq[s], s), rot(xk[s], s) * HEAD_V ** -0.5), tl, SUBLANES)
        yield
        yield from _ret_tile(res, q_r, k_r, x_ref[:, C_RT_V:C_RT_V + 256], dm_ref[...], prm(P_RT_LG),
                             rtdec_ref[...], hmasks, st_rt, m256, nseg, seglen)
        ort = res.pop()
        ms = _headsum(ort * ort, ind64) * (1.0 / HEAD_V)
        o_ref[:, 768:1024] = ort * lax.rsqrt(ms + NORM_EPS) * _silu(x_ref[:, C_RT_G:C_RT_G + 256])
        yield

    work = [ret_work([]), hgrn_work([]), gla_work([])]

    def advance(n):
        for _ in range(n):
            while work:
                try:
                    next(work[0])
                    break
                except StopIteration:
                    work.pop(0)

    xrw = cols(C_RW, RW_COLS)
    first = lax.broadcasted_iota(jnp.int32, (SUBLANES, RW_COLS), 0) == 0
    mu = mu_ref[...]

    def mix_rows(s):
        rw = xrw[s]
        if s.start % seglen == 0:
            prev0 = carry[s.start // seglen]
        else:
            prev0 = xrw[s.start - 1:s.start]
        prev = jnp.where(first, jnp.broadcast_to(prev0, (SUBLANES, RW_COLS)), pltpu.roll(rw, 1, 0))
        mix = rw + (prev - rw) * mu
        k0 = pltpu.roll(mix[:, 256:640], 320, 1)[:, 0:256]
        v_ = pltpu.roll(mix[:, 512:896], 320, 1)[:, 0:256]
        kk0 = k0 * prm(P_KK)
        return (mix[:, 0:256], k0, v_, jnp.tanh(mix[:, 256:384]), mix[:, 768:896],
                _sigmoid(mix[:, 896:1024]), kk0, kk0 * kk0)

    r, k0, v, t_w, s_a, s_g, kk0, kk0sq = _by_rows(mix_rows, tl, SUBLANES)
    for seg in range(nseg):
        hi = (seg + 1) * seglen
        carry[seg] = xrw[hi - 1:hi]
    lw_pre = _dot1(t_w, w2_ref[...])
    a_pre = _dot1(s_a, a2_ref[...])
    g = _dot1(s_g, g2_ref[...])
    n2 = _headsum(kk0sq, ind64)

    def gates(s):
        lw = -math.exp(-0.5) * _sigmoid(prm(P_W0) + lw_pre[s])
        a = _sigmoid(prm(P_A0) + a_pre[s])
        kk = kk0[s] / jnp.maximum(jnp.sqrt(n2[s]), 1e-12)
        k = k0[s] * (1.0 + (a - 1.0) * prm(P_KA))
        return (lw, a * kk, kk, k, r[s] * k * prm(P_RK)) + _split3(lw)

    lw, alpha, kk, k, rkp, lw1, lw2, lw3 = _by_rows(gates, tl, BF16_ROWS)
    bw = _dotsr(trib, (lw1, lw2, lw3))
    bc = _dotsr(onesb, (lw1, lw2, lw3))

    def scaled(s):
        bws, bcs = bw[s], bc[s]
        e_neg = jnp.exp(-bws)
        e_rem = jnp.exp(bcs - bws)
        return (kk[s] * jnp.exp(bws - lw[s]), alpha[s] * e_neg, k[s] * e_neg, r[s] * jnp.exp(bws),
                alpha[s] * e_rem, k[s] * e_rem, jnp.exp(bcs))

    rw_kt, rw_ah, rw_kh, rw_rt, rw_ap, rw_kp, rw_dec = _by_rows(scaled, tl, SUBLANES)
    bonus = _headsum(rkp, ind64) * v

    kbar, u0, m_a, mkv = _rwkv_blocks(rw_kt, rw_ah, rw_kh, rw_rt, rw_ap, rw_kp, v, rw_dec,
                                      hmasks, m256, m_scr, c_scr, c_rw, lambda: advance(1))

    per_step = 6
    sts = [st_rw[seg] for seg in range(nseg)]
    outs = [[None] * nblk_seg for _ in range(nseg)]
    for i in range(nblk_seg):
        for seg in range(nseg):
            blk = seg * nblk_seg + i
            sth, stl = _split(sts[seg])
            kr = jnp.concatenate([kbar[blk], rw_rt[blk * c_rw:(blk + 1) * c_rw]], axis=0).astype(BF16)
            mb = m_scr[blk]
            res = _dg(jnp.concatenate([mb, kr], axis=0), sth, _NN)
            xr = res[256:]
            u = u0[blk] - xr[:c_rw]
            outs[seg][i] = xr[c_rw:] + _dot1(m_a[blk], _stack_heads(u, hmasks)) + mkv[blk]
            sts[seg] = (res[:256] + _dg(mb, stl, _NN)) + c_scr[blk]
        advance(per_step)
    while work:
        advance(1)
    for seg in range(nseg):
        st_rw[seg] = sts[seg]
    orw = _cat([o for seg_outs in outs for o in seg_outs])
    mean = _headsum(orw, ind64) * (1.0 / HEAD_V)
    xc = orw - mean
    var = _headsum(xc * xc, ind64) * (1.0 / HEAD_V)
    o_ref[:, 512:768] = (xc * lax.rsqrt(var + RW_LN_EPS) * prm(P_LNW) + prm(P_LNB) + bonus) * g

    @pl.when(j == nj - 1)
    def _():
        for seg in range(nseg):
            nhg_ref[seg] = _state_out(st_hg[seg]).reshape(N_HEAD, HEAD_V, HEAD_V)
            ngl_ref[seg] = _state_out(st_gl[seg]).reshape(N_HEAD, GLA_DK, HEAD_V)
            nrw_ref[seg] = _state_out(st_rw[seg]).reshape(N_HEAD, HEAD_V, HEAD_V)
            nrt_ref[seg] = _state_out(st_rt[seg]).reshape(N_HEAD, HEAD_V, HEAD_V)


def _mixer_constants(nseg, seglen, c_rw, log_gamma):
    tl = nseg * seglen
    t = np.arange(tl)
    same = (t[:, None] // c_rw) == (t[None, :] // c_rw)
    same_seg = (t[:, None] // seglen) == (t[None, :] // seglen)
    causal = (t[None, :] <= t[:, None]) & same_seg
    tri = causal.astype(np.float32)
    trib = (same & causal).astype(np.float32)
    onesb = same.astype(np.float32)
    kv = np.arange(256)
    ind64 = ((kv[:, None] // HEAD_V) == (kv[None, :] // HEAD_V)).astype(np.float32)
    kg = np.arange(GLA_KW)
    indgl = ((kg[:, None] // GLA_DK) == (kv[None, :] // HEAD_V)).astype(np.float32)
    diff = jnp.asarray((t[:, None] - t[None, :]).astype(np.float32))
    dm = jnp.concatenate([jnp.where(jnp.asarray(causal), jnp.exp(diff * log_gamma[h]), 0.0)
                          for h in range(N_HEAD)], axis=1)
    rtdec = jnp.broadcast_to(jnp.exp(float(seglen) * jnp.repeat(log_gamma, HEAD_V))[:, None], (256, 256))
    ones_t = np.ones((tl, 256), np.float32)
    return (jnp.asarray(tri, BF16), jnp.asarray(trib, BF16), jnp.asarray(onesb, BF16),
            jnp.asarray(ind64, BF16), jnp.asarray(indgl, BF16), jnp.asarray(ind64, F32),
            jnp.asarray(indgl, F32), dm, rtdec, jnp.asarray(ones_t, BF16))


def _mixer(proj, states, shift, cosf, sins, lp, c, c_rw, nseg, seglen, read_state):
    ng, rows, _ = proj.shape
    tl = nseg * seglen
    nj = rows // tl
    layer = lp['layer']
    consts = _mixer_constants(nseg, seglen, c_rw, lp['log_gamma'])

    def seq_spec(w):
        return pl.BlockSpec((None, tl, w), lambda g, j: (g, j, 0))

    def const_spec(a):
        nd = a.ndim
        return pl.BlockSpec(a.shape, lambda g, j: (0,) * nd)

    pos_spec = pl.BlockSpec((tl, 256), lambda g, j: (j, 0))
    params = (lp['p256'], lp['glba'], lp['mu'], lp['wa2'], lp['w2'], lp['a2'], lp['g2'])
    state_specs = [pl.BlockSpec((None, nseg) + s.shape[2:], lambda g, j: (layer, g, 0, 0, 0)) for s in states]
    in_specs = [seq_spec(N_MIX)]
    args = [proj]
    if read_state:
        in_specs += state_specs + [pl.BlockSpec((None, nseg, 1, RW_COLS), lambda g, j: (layer, g, 0, 0))]
        args += list(states) + [shift]
        aliases = {1 + i: 1 + i for i in range(len(states))}
    in_specs += [pos_spec, pos_spec] + [const_spec(a) for a in params] + [const_spec(a) for a in consts]
    args += [cosf, sins, *params, *consts]
    if not read_state:
        aliases = {len(args) + i: 1 + i for i in range(len(states))}
        in_specs += [pl.BlockSpec(memory_space=pl.ANY)] * len(states)
        args += list(states)
    out_specs = [seq_spec(N_BRANCH * BRANCH_W)] + state_specs
    out_shape = ([jax.ShapeDtypeStruct((ng, rows, N_BRANCH * BRANCH_W), F32)]
                 + [jax.ShapeDtypeStruct(s.shape, F32) for s in states])
    nblk = tl // c_rw
    scratch = [pltpu.VMEM((nseg, 256, 256), F32), pltpu.VMEM((nseg, GLA_KW, 256), F32),
               pltpu.VMEM((nseg, 256, 256), F32), pltpu.VMEM((nseg, 256, 256), F32),
               pltpu.VMEM((nseg, 1, RW_COLS), F32),
               pltpu.VMEM((nblk, 256, 256), BF16), pltpu.VMEM((nblk, 256, 256), F32)]
    return pl.pallas_call(
        functools.partial(_mixer_kernel, c=c, c_rw=c_rw, nseg=nseg, seglen=seglen, nj=nj,
                          has_state=read_state, n_alias=0 if read_state else len(states)),
        grid=(ng, nj),
        in_specs=in_specs,
        out_specs=out_specs,
        out_shape=out_shape,
        scratch_shapes=scratch,
        input_output_aliases=aliases,
        compiler_params=pltpu.CompilerParams(dimension_semantics=("arbitrary", "arbitrary"),
                                             vmem_limit_bytes=VMEM_LIMIT_BYTES),
        name="mixer",
    )(*args)


def _tile_heads(v):
    return jnp.tile(v, N_HEAD)


def _layer_params(l, lower_bounds, hg_norm_w, gla_wa2, gla_ba, gla_norm_w, rw_mu, rw_w0, rw_w2,
                  rw_a0, rw_a2, rw_g2, rw_kk, rw_ka, rw_rk, rw_ln_w, rw_ln_b):
    log_gamma = jnp.log1p(-jnp.exp2(-5.0 - jnp.arange(N_HEAD, dtype=F32)))
    rows = [lower_bounds[l], _tile_heads(hg_norm_w[l]), _tile_heads(gla_norm_w[l]), rw_w0[l], rw_a0[l],
            rw_kk[l], rw_ka[l], rw_rk[l], rw_ln_w[l], rw_ln_b[l], jnp.repeat(log_gamma, HEAD_V)]
    p256 = jnp.concatenate([jnp.stack(rows), jnp.zeros((16 - len(rows), 256), F32)], axis=0)
    zeros64 = jnp.zeros((64, 256), F32)
    return {
        'log_gamma': log_gamma,
        'p256': p256,
        'glba': jnp.concatenate([gla_ba[l][None, :], jnp.zeros((7, GLA_KW), F32)], axis=0),
        'mu': rw_mu[l][None, :],
        'wa2': jnp.concatenate([gla_wa2[l], jnp.zeros((LANES - GLA_RANK, GLA_KW), F32)], axis=0).astype(BF16),
        'w2': jnp.concatenate([rw_w2[l], zeros64], axis=0).astype(BF16),
        'a2': jnp.concatenate([zeros64, rw_a2[l]], axis=0).astype(BF16),
        'g2': rw_g2[l].astype(BF16),
    }


def _rotary_tables(pos):
    half = HEAD_V // 2
    inv = ROPE_BASE ** (-jnp.arange(half, dtype=F32) / half)
    ang = pos[:, None] * inv[None, :]
    cos, sin = jnp.cos(ang), jnp.sin(ang)
    cosf = jnp.tile(jnp.concatenate([cos, cos], axis=1), (1, N_HEAD))
    sins = jnp.tile(jnp.concatenate([-sin, sin], axis=1), (1, N_HEAD))
    return cosf, sins


def _group_layer(x, states, shift, tables, lp, attn_nw, ffn_nw, final_nw, final, c, c_rw, nseg, seglen,
                 read_state):
    nb, length, _ = x.shape
    xf = x.reshape(nb * length, D_MODEL)
    proj = _inproj(xf, attn_nw, lp['wmix'], lp['layer'])
    o, *outs = _mixer(proj.reshape(nb // nseg, nseg * length, N_MIX), states, shift,
                      tables[0], tables[1], lp, c, c_rw, nseg, seglen, read_state)
    x2 = _post(xf, o.reshape(nb * length, N_BRANCH * BRANCH_W), attn_nw, lp['wg'], lp['wb'], lp['wo'],
               ffn_nw, lp['wi'], lp['wfo'], final_nw, final, lp['layer'])
    new_shift = proj.reshape(nb, length, N_MIX)[:, length - 1, C_RW:C_RW + RW_COLS]
    return x2.reshape(nb, length, D_MODEL), outs, new_shift


def _states_out(outs, shifts):
    n_hg, n_gl, n_rw, n_rt = outs
    return (n_hg, n_gl, n_rw, jnp.stack(shifts), n_rt)


def kernel(x_prompt, x_sample, state_hgrn, state_gla, state_rwkv, state_rwkv_shift, state_ret,
           attn_norm_w, w_in, hg_lb_logits, hg_norm_w, gla_wa2, gla_ba, gla_norm_w,
           rw_mu, rw_w0, rw_w2, rw_a0, rw_a2, rw_g2, rw_kk, rw_ka, rw_rk, rw_ln_w, rw_ln_b,
           w_branch, w_out, ffn_norm_w, w_ffn_in, w_ffn_out, final_norm_w):
    depth = w_in.shape[0]
    lb_p = jax.nn.softmax(hg_lb_logits.astype(F32), axis=0)
    lower_bounds = jnp.cumsum(lb_p, axis=0) - lb_p[0:1]

    bp, lp_len, _ = x_prompt.shape
    bs, ls_len, _ = x_sample.shape
    tab_p = _rotary_tables(jnp.arange(lp_len, dtype=F32))
    tab_s = _rotary_tables(float(PAST_LEN) + jnp.arange(ls_len, dtype=F32))
    tab_s = tuple(jnp.tile(t, (SAMPLE_SEQS_PER_TILE, 1)) for t in tab_s)
    final_nw = final_norm_w[None, :]
    wmix, wg = _prep_in_weights(w_in)
    dense = {'wmix': wmix, 'wg': wg, 'wb': w_branch.astype(BF16), 'wo': w_out.astype(BF16),
             'wi': w_ffn_in.astype(BF16), 'wfo': w_ffn_out.astype(BF16)}

    s_states = [state_hgrn, state_gla, state_rwkv, state_ret]
    p_states = [jnp.zeros((depth, bp) + s.shape[2:], F32) for s in s_states]
    s_shift = state_rwkv_shift.reshape(depth, bs, 1, RW_COLS)
    xp, xs = x_prompt, x_sample
    p_shifts, s_shifts = [], []
    for l in range(depth):
        lp = _layer_params(l, lower_bounds, hg_norm_w, gla_wa2, gla_ba, gla_norm_w, rw_mu, rw_w0,
                           rw_w2, rw_a0, rw_a2, rw_g2, rw_kk, rw_ka, rw_rk, rw_ln_w, rw_ln_b)
        lp.update(dense)
        lp['layer'] = l
        final = l == depth - 1
        anw, fnw = attn_norm_w[l][None, :], ffn_norm_w[l][None, :]
        xp, p_states, sh = _group_layer(xp, p_states, None, tab_p, lp, anw, fnw, final_nw, final, c=8,
                                        c_rw=32, nseg=1, seglen=PROMPT_TILE, read_state=False)
        p_shifts.append(sh)
        xs, s_states, sh = _group_layer(xs, s_states, s_shift, tab_s, lp, anw, fnw, final_nw, final,
                                        c=ls_len, c_rw=ls_len, nseg=SAMPLE_SEQS_PER_TILE, seglen=ls_len,
                                        read_state=True)
        s_shifts.append(sh)

    return (xp, xs) + _states_out(p_states, p_shifts) + _states_out(s_states, s_shifts)
```

```python
import functools
import math

import jax
import jax.numpy as jnp
import numpy as np
from jax import lax
from jax.experimental import pallas as pl
from jax.experimental.pallas import tpu as pltpu

F32 = jnp.float32
BF16 = jnp.bfloat16

D_MODEL = 1024
N_HEAD = 4
HEAD_V = 64
BRANCH_W = N_HEAD * HEAD_V
N_BRANCH = 4
GLA_DK = 32
GLA_KW = N_HEAD * GLA_DK
GLA_RANK = 16
GLA_TAU = 16.0
RW_COLS = 1024
NORM_EPS = 1e-6
RW_LN_EPS = 64e-5
MIN_FORGET = 1e-30
ROPE_BASE = 10000.0
PAST_LEN = 16384
LOG2E = 1.4426950408889634
D_FF = 2816
N_MIX = 3968
PROMPT_TILE = 256
SAMPLE_SEQS_PER_TILE = 8
N_GATE = N_BRANCH * D_MODEL

VMEM_LIMIT_BYTES = 56 * 1024 * 1024
LANES = 128
SUBLANES = 8
BF16_ROWS = 16

C_HG_Q, C_HG_F, C_HG_I, C_HG_G = 0, 256, 512, 768
C_GL_Q, C_GL_K, C_GL_V, C_GL_A, C_GL_G = 1024, 1152, 1280, 1536, 1664
C_RW = 1920
C_RT_Q, C_RT_K, C_RT_V, C_RT_G = 2944, 3200, 3456, 3712

P_LB, P_HG_NW, P_GL_NW, P_W0, P_A0, P_KK, P_KA, P_RK, P_LNW, P_LNB, P_RT_LG = range(11)

_NN = (((1,), (0,)), ((), ()))
_NT = (((1,), (1,)), ((), ()))
_TN = (((0,), (0,)), ((), ()))


def _dg(a, b, dn):
    return lax.dot_general(a, b, dn, preferred_element_type=F32)


def _split(x):
    hi = x.astype(BF16)
    lo = (x - hi.astype(F32)).astype(BF16)
    return hi, lo


def _split3(x):
    x1 = x.astype(BF16)
    r = x - x1.astype(F32)
    x2 = r.astype(BF16)
    x3 = (r - x2.astype(F32)).astype(BF16)
    return x1, x2, x3


def _dot1(a, b, dn=_NN):
    return _dg(a.astype(BF16), b.astype(BF16), dn)


def _dots(pieces, b_exact, dn=_NN):
    out = _dg(pieces[0], b_exact, dn)
    for p in pieces[1:]:
        out = out + _dg(p, b_exact, dn)
    return out


def _dotsr(a_exact, pieces, dn=_NN):
    out = _dg(a_exact, pieces[0], dn)
    for p in pieces[1:]:
        out = out + _dg(a_exact, p, dn)
    return out


def _headsum(x, ind):
    return _dg(x.astype(BF16), ind, _NN)


def _sigmoid(x):
    return jax.nn.sigmoid(x)


def _silu(x):
    return x * jax.nn.sigmoid(x)


def _rms(x, w):
    return x * lax.rsqrt(jnp.mean(x * x, axis=-1, keepdims=True) + NORM_EPS) * w


def _cat(parts, axis=0):
    return parts[0] if len(parts) == 1 else jnp.concatenate(parts, axis=axis)


def _by_rows(fn, rows, chunk, start=0):
    outs = [fn(slice(i, min(i + chunk, rows))) for i in range(start, rows, chunk)]
    if isinstance(outs[0], tuple):
        return tuple(_cat([o[j] for o in outs]) for j in range(len(outs[0])))
    return _cat(outs)


def _inproj_kernel(x_ref, nw_ref, w_ref, o_ref):
    h = _rms(x_ref[...], nw_ref[...]).astype(BF16)
    pad0, pad1 = C_GL_A + GLA_RANK, C_GL_A + LANES
    o_ref[:, 0:pad1] = _dg(h, w_ref[0:pad1, :], _NT)
    o_ref[:, pad1:N_MIX] = _dg(h, w_ref[pad0:GATE_COL0, :], _NT)


def _merge_body(x, o_ref, nw_ref, wg_ref, wb_ref, wo_ref):
    h = _rms(x, nw_ref[...]).astype(BF16)
    merged = None
    for b in range(N_BRANCH):
        gl = _dg(h, wg_ref[b * D_MODEL:(b + 1) * D_MODEL, :], _NT)
        ob = o_ref[:, b * BRANCH_W:(b + 1) * BRANCH_W].astype(BF16)
        up = jnp.dot(ob, wb_ref[b], preferred_element_type=F32)
        t = _sigmoid(gl) * up
        merged = t if merged is None else merged + t
    return x + jnp.dot(merged.astype(BF16), wo_ref[...], preferred_element_type=F32)


FFN_CHUNK = 1408


def _ffn_body(x, nw_ref, wi_ref, wo_ref):
    h = _rms(x, nw_ref[...]).astype(BF16)
    acc = None
    for j in range(D_FF // FFN_CHUNK):
        lo = j * FFN_CHUNK
        g = jnp.dot(h, wi_ref[:, lo:lo + FFN_CHUNK], preferred_element_type=F32)
        u = jnp.dot(h, wi_ref[:, D_FF + lo:D_FF + lo + FFN_CHUNK], preferred_element_type=F32)
        a = (_silu(g) * u).astype(BF16)
        t = jnp.dot(a, wo_ref[lo:lo + FFN_CHUNK, :], preferred_element_type=F32)
        acc = t if acc is None else acc + t
    return x + acc


def _post_kernel(x_ref, o_ref, anw_ref, wg_ref, wb_ref, wo_ref, fnw_ref, wi_ref, wfo_ref, fw_ref, out_ref,
                 *, final):
    x1 = _merge_body(x_ref[...], o_ref, anw_ref, wg_ref, wb_ref, wo_ref)
    x2 = _ffn_body(x1, fnw_ref, wi_ref, wfo_ref)
    if final:
        x2 = _rms(x2, fw_ref[...])
    out_ref[...] = x2


def _dense_params():
    return pltpu.CompilerParams(dimension_semantics=("arbitrary",), vmem_limit_bytes=VMEM_LIMIT_BYTES)


def _const_spec(shape):
    nd = len(shape)
    return pl.BlockSpec(shape, lambda i: (0,) * nd)


def _token_tile(t):
    return 512 if t % 512 == 0 else t


GATE_COL0 = 3856
INPROJ_TILE = 1024


def _prep_in_weights(w_in):
    wt = jnp.swapaxes(w_in, 1, 2).astype(BF16)
    return wt, wt[:, GATE_COL0:]


def _inproj(x, nw, wmix, l):
    t = x.shape[0]
    tm = INPROJ_TILE if t % INPROJ_TILE == 0 and t > INPROJ_TILE else _token_tile(t)
    return pl.pallas_call(
        _inproj_kernel,
        grid=(t // tm,),
        in_specs=[pl.BlockSpec((tm, D_MODEL), lambda i: (i, 0)),
                  _const_spec((1, D_MODEL)), pl.BlockSpec((None, GATE_COL0, D_MODEL), lambda i: (l, 0, 0))],
        out_specs=pl.BlockSpec((tm, N_MIX), lambda i: (i, 0)),
        out_shape=jax.ShapeDtypeStruct((t, N_MIX), F32),
        compiler_params=_dense_params(),
        name="inproj",
    )(x, nw, wmix)


def _post(x, o, anw, wg, wb, wo, fnw, wi, wfo, fw, final, l):
    t = x.shape[0]
    tm = _token_tile(t)
    return pl.pallas_call(
        functools.partial(_post_kernel, final=final),
        grid=(t // tm,),
        in_specs=[pl.BlockSpec((tm, D_MODEL), lambda i: (i, 0)),
                  pl.BlockSpec((tm, N_BRANCH * BRANCH_W), lambda i: (i, 0)),
                  _const_spec((1, D_MODEL)), pl.BlockSpec((None, N_GATE, D_MODEL), lambda i: (l, 0, 0)),
                  pl.BlockSpec((None, N_BRANCH, BRANCH_W, D_MODEL), lambda i: (l, 0, 0, 0)),
                  pl.BlockSpec((None, D_MODEL, D_MODEL), lambda i: (l, 0, 0)),
                  _const_spec((1, D_MODEL)), pl.BlockSpec((None, D_MODEL, 2 * D_FF), lambda i: (l, 0, 0)),
                  pl.BlockSpec((None, D_FF, D_MODEL), lambda i: (l, 0, 0)), _const_spec((1, D_MODEL))],
        out_specs=pl.BlockSpec((tm, D_MODEL), lambda i: (i, 0)),
        out_shape=jax.ShapeDtypeStruct((t, D_MODEL), F32),
        compiler_params=_dense_params(),
        name="post",
    )(x, o, anw, wg, wb, wo, fnw, wi, wfo, fw)


def _stack_heads(x, hm):
    if x.shape[0] % BF16_ROWS == 0:
        x = x.astype(BF16)
        return jnp.concatenate([x * hm[h].astype(BF16) for h in range(N_HEAD)], axis=0)
    return jnp.concatenate([x * hm[h] for h in range(N_HEAD)], axis=0)


def _state_in(s, mask):
    return jnp.concatenate([s] * N_HEAD, axis=1) * mask


def _state_out(st):
    return (st[:, 0:HEAD_V] + st[:, HEAD_V:2 * HEAD_V]) + (st[:, 2 * HEAD_V:3 * HEAD_V] + st[:, 3 * HEAD_V:])


def _gla_tile(res, q, k, v, g3, tri, ones_t, ind, hm_k, hm_v, st_ref, mask, c, nseg, seglen):
    tl = q.shape[0]
    nblk = seglen // c
    b = _dotsr(tri, g3)
    yield
    rowi = lax.broadcasted_iota(jnp.int32, (c, q.shape[1]), 0)
    group = 4 * c
    accs, qes = [], []
    for g0 in range(0, tl, group):
        pieces, vrots = [], []
        for r0 in range(g0, g0 + group, c):
            qb, kb, vb = q[r0:r0 + c], k[r0:r0 + c], v[r0:r0 + c]
            bb = b[r0:r0 + c]
            b2 = bb * LOG2E
            qes.append(qb * jnp.exp2(b2))
            pieces.append(qb * kb)
            vr = [vb]
            for d in range(1, c):
                e = jnp.exp2(b2 - pltpu.roll(b2, d, 0))
                pieces.append(jnp.where(rowi >= d, qb * pltpu.roll(kb, d, 0) * e, 0.0))
                vr.append(pltpu.roll(vb, d, 0))
            vrots.append(vr)
        a = _dot1(jnp.concatenate(pieces, axis=0), ind)
        for i, vr in enumerate(vrots):
            acc = a[i * c * c:i * c * c + c] * vr[0]
            for d in range(1, c):
                acc = acc + a[(i * c + d) * c:(i * c + d + 1) * c] * vr[d]
            accs.append(acc)
        yield
    acc = _cat(accs)
    qe = _cat(qes)
    if nblk > 1:
        scores = []
        for seg in range(nseg):
            base = seg * seglen
            for i in range(1, nblk):
                lo = base + i * c
                r = b[lo - 1:lo, :]
                qi = q[lo:lo + c] * jnp.exp(b[lo:lo + c] - r)
                ki = _by_rows(lambda s: k[s] * jnp.exp(r - b[s]), lo, 2 * BF16_ROWS, base)
                scores.append(_dot1(_stack_heads(qi, hm_k), ki, _NT))
                if i % 3 == 0:
                    yield
        parts = []
        n = 0
        for seg in range(nseg):
            base = seg * seglen
            parts.append(jnp.zeros((c, N_HEAD * HEAD_V), F32))
            for i in range(1, nblk):
                r4 = _dot1(scores[n], v[base:base + i * c])
                n += 1
                oi = r4[0:c] * hm_v[0]
                for h in range(1, N_HEAD):
                    oi = oi + r4[h * c:(h + 1) * c] * hm_v[h]
                parts.append(oi)
                if i % 3 == 0:
                    yield
        acc = acc + jnp.concatenate(parts, axis=0)
    outs = []
    for seg in range(nseg):
        lo, hi = seg * seglen, (seg + 1) * seglen
        blast = b[hi - 1:hi, :]
        dcol = jnp.exp(_dots([p[lo:hi] for p in g3], ones_t[lo:hi], _TN))
        st = st_ref[seg]
        outs.append(acc[lo:hi] + _dot1(qe[lo:hi], st))
        ke = _by_rows(lambda s: k[s] * jnp.exp(blast - b[s]), hi, 2 * BF16_ROWS, lo)
        st_ref[seg] = st * dcol + _dot1(ke, v[lo:hi], _TN) * mask
        yield
    res.append(_cat(outs))
    yield


def _ret_tile(res, q, k, v, dm, lg, rtdec, hm, st_ref, mask, nseg, seglen):
    tau = (lax.broadcasted_iota(jnp.int32, q.shape, 0) & (seglen - 1)).astype(F32)
    a = _dot1(q, _stack_heads(k, hm), _NT) * dm
    yield
    intra = _dot1(a, _stack_heads(v, hm))
    qe = q * jnp.exp((tau + 1.0) * lg)
    ke = k * jnp.exp((float(seglen) - 1.0 - tau) * lg)
    yield
    outs = []
    for seg in range(nseg):
        lo, hi = seg * seglen, (seg + 1) * seglen
        st = st_ref[seg]
        outs.append(intra[lo:hi] + _dot1(qe[lo:hi], st))
        st_ref[seg] = st * rtdec + _dot1(ke[lo:hi], v[lo:hi], _TN) * mask
    res.append(_cat(outs))
    yield


def _head_rms_gate(o, ind64, nw, gate):
    ms = _headsum(o * o, ind64) * (1.0 / HEAD_V)
    return o * lax.rsqrt(ms + NORM_EPS) * nw * _silu(gate)


def _rwkv_blocks(kt, ah, kh, rt, ap, kp, v, dec, hmasks, m256, m_scr, c_scr, c, tick):
    nblk = kt.shape[0] // c
    blocks = range(nblk)
    r4 = lax.broadcasted_iota(jnp.int32, (c, N_HEAD * c), 0)
    s4 = lax.broadcasted_iota(jnp.int32, (c, N_HEAD * c), 1) & (c - 1)
    strict = s4 < r4
    incl = s4 <= r4
    rr = lax.broadcasted_iota(jnp.int32, (N_HEAD * c, N_HEAD * c), 0)
    cc = lax.broadcasted_iota(jnp.int32, (N_HEAD * c, N_HEAD * c), 1)
    sh = int(math.log2(c))
    bd4 = ((rr >> sh) == (cc >> sh)).astype(F32)
    eye4 = (rr == cc).astype(F32)
    eye256 = (lax.broadcasted_iota(jnp.int32, (256, 256), 0) == lax.broadcasted_iota(jnp.int32, (256, 256), 1))

    def blk(x, i):
        return x[i * c:(i + 1) * c]

    def stack(x):
        return _stack_heads(x, hmasks)

    def same(size):
        s = int(math.log2(size))
        return (rr >> s) == (cc >> s)

    kr = [jnp.concatenate([blk(kt, i), blk(rt, i)], axis=0).astype(BF16) for i in blocks]
    la = [_dot1(kr[i], stack(blk(ah, i)), _NT) for i in blocks]
    lk = [_dot1(kr[i], stack(blk(kh, i)), _NT) for i in blocks]
    tick()
    l_a = [jnp.where(strict, x[:c], 0.0) for x in la]
    m_a = [jnp.where(incl, x[c:], 0.0) for x in la]
    l_k = [jnp.where(strict, x[:c], 0.0) for x in lk]
    m_k = [jnp.where(incl, x[c:], 0.0) for x in lk]
    lbd = [jnp.concatenate([x] * N_HEAD, axis=0) * bd4 for x in l_a]
    tm = [eye4 - jnp.where(same(2), x, 0.0) for x in lbd]
    size = 2
    while size < c:
        cross = same(2 * size) & jnp.logical_not(same(size))
        xm = [_dot1(tm[i], jnp.where(cross, lbd[i], 0.0)) for i in blocks]
        tick()
        tm = [tm[i] - _dot1(xm[i], tm[i]) for i in blocks]
        tick()
        size *= 2
    ts = []
    for x in tm:
        t = x[0:c]
        for h in range(1, N_HEAD):
            t = t + x[h * c:(h + 1) * c]
        ts.append(t)
    vs = [stack(blk(v, i)).astype(BF16) for i in blocks]
    lkv = [_dot1(l_k[i], vs[i]) for i in blocks]
    mkv = [_dot1(m_k[i], vs[i]) for i in blocks]
    tick()
    kbar = [_dot1(ts[i], stack(blk(kt, i))) for i in blocks]
    u0 = [-_dot1(ts[i], stack(lkv[i])) for i in blocks]
    tick()
    for i in blocks:
        m = jnp.where(eye256, dec[i * c:i * c + 1], 0.0) - _dot1(blk(ap, i), kbar[i], _TN) * m256
        m_scr[i] = m.astype(BF16)
    tick()
    for i in blocks:
        c_scr[i] = _dot1(jnp.concatenate([blk(ap, i), blk(kp, i)], axis=0),
                         jnp.concatenate([u0[i], blk(v, i)], axis=0), _TN) * m256
    return kbar, u0, m_a, mkv


N_MIXER_SHARED_INPUTS = 19


def _mixer_kernel(*refs, c, c_rw, nseg, seglen, nj, has_state, n_alias):
    refs = list(refs)
    x_ref = refs.pop(0)
    if has_state:
        shg_ref, sgl_ref, srw_ref, srt_ref, shift_ref = refs[:5]
        refs = refs[5:]
    (cos_ref, sin_ref, p256_ref, glba_ref, mu_ref, wa2_ref, w2_ref, a2_ref, g2_ref,
     tri_ref, trib_ref, onesb_ref, ind64_ref, indgl_ref, m256_ref, mgl_ref, dm_ref, rtdec_ref,
     ones_ref) = refs[:N_MIXER_SHARED_INPUTS]
    (o_ref, nhg_ref, ngl_ref, nrw_ref, nrt_ref,
     st_hg, st_gl, st_rw, st_rt, carry, m_scr, c_scr) = refs[N_MIXER_SHARED_INPUTS + n_alias:]
    tl = nseg * seglen
    nblk_seg = seglen // c_rw
    j = pl.program_id(1)

    def prm(i):
        return p256_ref[i:i + 1, :]

    tri = tri_ref[...]
    trib = trib_ref[...]
    onesb = onesb_ref[...]
    ones_t = ones_ref[...]
    ind64 = ind64_ref[...]
    indgl = indgl_ref[...]
    m256 = m256_ref[...]
    mgl = mgl_ref[...]
    lane = lax.broadcasted_iota(jnp.int32, (1, 256), 1)
    hmasks = [((lane >> 6) == h).astype(F32) for h in range(N_HEAD)]
    lane_k = lax.broadcasted_iota(jnp.int32, (1, GLA_KW), 1)
    hmasks_glk = [((lane_k >> 5) == h).astype(F32) for h in range(N_HEAD)]

    @pl.when(j == 0)
    def _():
        for seg in range(nseg):
            if has_state:
                st_hg[seg] = _state_in(shg_ref[seg].reshape(256, HEAD_V), m256)
                st_gl[seg] = _state_in(sgl_ref[seg].reshape(GLA_KW, HEAD_V), mgl)
                st_rw[seg] = _state_in(srw_ref[seg].reshape(256, HEAD_V), m256)
                st_rt[seg] = _state_in(srt_ref[seg].reshape(256, HEAD_V), m256)
                carry[seg] = shift_ref[seg]
            else:
                st_hg[seg] = jnp.zeros((256, 256), F32)
                st_gl[seg] = jnp.zeros((GLA_KW, 256), F32)
                st_rw[seg] = jnp.zeros((256, 256), F32)
                st_rt[seg] = jnp.zeros((256, 256), F32)
                carry[seg] = jnp.zeros((1, RW_COLS), F32)

    def cols(c0, w):
        return x_ref.at[:, c0:c0 + w]

    def hgrn_work(res):
        lb = prm(P_LB)
        xq, xf = cols(C_HG_Q, 256), cols(C_HG_F, 256)

        def prep(s):
            forget = lb + (1.0 - lb) * _sigmoid(xf[s])
            return (_silu(xq[s]), 1.0 - forget) + _split3(jnp.log(jnp.maximum(forget, MIN_FORGET)))

        q, k, g1, g2_, g3_ = _by_rows(prep, tl, BF16_ROWS)
        yield
        yield from _gla_tile(res, q, k, cols(C_HG_I, 256), (g1, g2_, g3_), tri, ones_t, ind64,
                             hmasks, hmasks, st_hg, m256, c, nseg, seglen)
        o_ref[:, 0:256] = _head_rms_gate(res.pop(), ind64, prm(P_HG_NW), x_ref[:, C_HG_G:C_HG_G + 256])
        yield

    def gla_work(res):
        za = _dot1(x_ref[:, C_GL_A:C_GL_A + LANES], wa2_ref[...]) + glba_ref[0:1, :]

        def prep(s):
            z = za[s]
            return _split3((jnp.minimum(z, 0.0) - jnp.log1p(jnp.exp(-jnp.abs(z)))) / GLA_TAU)

        g3 = _by_rows(prep, tl, BF16_ROWS)
        yield
        yield from _gla_tile(res, x_ref[:, C_GL_Q:C_GL_Q + GLA_KW] * GLA_DK ** -0.5,
                             x_ref[:, C_GL_K:C_GL_K + GLA_KW], cols(C_GL_V, 256), g3,
                             tri, ones_t, indgl, hmasks_glk, hmasks, st_gl, mgl, c, nseg, seglen)
        o_ref[:, 256:512] = _head_rms_gate(res.pop(), ind64, prm(P_GL_NW), x_ref[:, C_GL_G:C_GL_G + 256])
        yield

    def ret_work(res):
        upper = (lane & (HEAD_V // 2)) != 0
        xq, xk = cols(C_RT_Q, 256), cols(C_RT_K, 256)

        def rot(x, s):
            sw = jnp.where(upper, pltpu.roll(x, HEAD_V // 2, 1), pltpu.roll(x, 256 - HEAD_V // 2, 1))
            return x * cos_ref[s, :] + sw * sin_ref[s, :]

        q_r, k_r = _by_rows(lambda s: (rot(xq[s], s), rot(xk[s], s) * HEAD_V ** -0.5), tl, SUBLANES)
        yield
        yield from _ret_tile(res, q_r, k_r, x_ref[:, C_RT_V:C_RT_V + 256], dm_ref[...], prm(P_RT_LG),
                             rtdec_ref[...], hmasks, st_rt, m256, nseg, seglen)
        ort = res.pop()
        ms = _headsum(ort * ort, ind64) * (1.0 / HEAD_V)
        o_ref[:, 768:1024] = ort * lax.rsqrt(ms + NORM_EPS) * _silu(x_ref[:, C_RT_G:C_RT_G + 256])
        yield

    work = [ret_work([]), hgrn_work([]), gla_work([])]

    def advance(n):
        for _ in range(n):
            while work:
                try:
                    next(work[0])
                    break
                except StopIteration:
                    work.pop(0)

    xrw = cols(C_RW, RW_COLS)
    first = lax.broadcasted_iota(jnp.int32, (SUBLANES, RW_COLS), 0) == 0
    mu = mu_ref[...]

    def mix_rows(s):
        rw = xrw[s]
        if s.start % seglen == 0:
            prev0 = carry[s.start // seglen]
        else:
            prev0 = xrw[s.start - 1:s.start]
        prev = jnp.where(first, jnp.broadcast_to(prev0, (SUBLANES, RW_COLS)), pltpu.roll(rw, 1, 0))
        mix = rw + (prev - rw) * mu
        k0 = pltpu.roll(mix[:, 256:640], 320, 1)[:, 0:256]
        v_ = pltpu.roll(mix[:, 512:896], 320, 1)[:, 0:256]
        kk0 = k0 * prm(P_KK)
        return (mix[:, 0:256], k0, v_, jnp.tanh(mix[:, 256:384]), mix[:, 768:896],
                _sigmoid(mix[:, 896:1024]), kk0, kk0 * kk0)

    r, k0, v, t_w, s_a, s_g, kk0, kk0sq = _by_rows(mix_rows, tl, SUBLANES)
    for seg in range(nseg):
        hi = (seg + 1) * seglen
        carry[seg] = xrw[hi - 1:hi]
    lw_pre = _dot1(t_w, w2_ref[...])
    a_pre = _dot1(s_a, a2_ref[...])
    g = _dot1(s_g, g2_ref[...])
    n2 = _headsum(kk0sq, ind64)

    def gates(s):
        lw = -math.exp(-0.5) * _sigmoid(prm(P_W0) + lw_pre[s])
        a = _sigmoid(prm(P_A0) + a_pre[s])
        kk = kk0[s] / jnp.maximum(jnp.sqrt(n2[s]), 1e-12)
        k = k0[s] * (1.0 + (a - 1.0) * prm(P_KA))
        return (lw, a * kk, kk, k, r[s] * k * prm(P_RK)) + _split3(lw)

    lw, alpha, kk, k, rkp, lw1, lw2, lw3 = _by_rows(gates, tl, BF16_ROWS)
    bw = _dotsr(trib, (lw1, lw2, lw3))
    bc = _dotsr(onesb, (lw1, lw2, lw3))

    def scaled(s):
        bws, bcs = bw[s], bc[s]
        e_neg = jnp.exp(-bws)
        e_rem = jnp.exp(bcs - bws)
        return (kk[s] * jnp.exp(bws - lw[s]), alpha[s] * e_neg, k[s] * e_neg, r[s] * jnp.exp(bws),
                alpha[s] * e_rem, k[s] * e_rem, jnp.exp(bcs))

    rw_kt, rw_ah, rw_kh, rw_rt, rw_ap, rw_kp, rw_dec = _by_rows(scaled, tl, SUBLANES)
    bonus = _headsum(rkp, ind64) * v

    kbar, u0, m_a, mkv = _rwkv_blocks(rw_kt, rw_ah, rw_kh, rw_rt, rw_ap, rw_kp, v, rw_dec,
                                      hmasks, m256, m_scr, c_scr, c_rw, lambda: advance(1))

    per_step = 6
    sts = [st_rw[seg] for seg in range(nseg)]
    outs = [[None] * nblk_seg for _ in range(nseg)]
    for i in range(nblk_seg):
        for seg in range(nseg):
            blk = seg * nblk_seg + i
            sth, stl = _split(sts[seg])
            kr = jnp.concatenate([kbar[blk], rw_rt[blk * c_rw:(blk + 1) * c_rw]], axis=0).astype(BF16)
            mb = m_scr[blk]
            res = _dg(jnp.concatenate([mb, kr], axis=0), sth, _NN)
            xr = res[256:]
            u = u0[blk] - xr[:c_rw]
            outs[seg][i] = xr[c_rw:] + _dot1(m_a[blk], _stack_heads(u, hmasks)) + mkv[blk]
            sts[seg] = (res[:256] + _dg(mb, stl, _NN)) + c_scr[blk]
        advance(per_step)
    while work:
        advance(1)
    for seg in range(nseg):
        st_rw[seg] = sts[seg]
    orw = _cat([o for seg_outs in outs for o in seg_outs])
    mean = _headsum(orw, ind64) * (1.0 / HEAD_V)
    xc = orw - mean
    var = _headsum(xc * xc, ind64) * (1.0 / HEAD_V)
    o_ref[:, 512:768] = (xc * lax.rsqrt(var + RW_LN_EPS) * prm(P_LNW) + prm(P_LNB) + bonus) * g

    @pl.when(j == nj - 1)
    def _():
        for seg in range(nseg):
            nhg_ref[seg] = _state_out(st_hg[seg]).reshape(N_HEAD, HEAD_V, HEAD_V)
            ngl_ref[seg] = _state_out(st_gl[seg]).reshape(N_HEAD, GLA_DK, HEAD_V)
            nrw_ref[seg] = _state_out(st_rw[seg]).reshape(N_HEAD, HEAD_V, HEAD_V)
            nrt_ref[seg] = _state_out(st_rt[seg]).reshape(N_HEAD, HEAD_V, HEAD_V)


def _mixer_constants(nseg, seglen, c_rw, log_gamma):
    tl = nseg * seglen
    t = np.arange(tl)
    same = (t[:, None] // c_rw) == (t[None, :] // c_rw)
    same_seg = (t[:, None] // seglen) == (t[None, :] // seglen)
    causal = (t[None, :] <= t[:, None]) & same_seg
    tri = causal.astype(np.float32)
    trib = (same & causal).astype(np.float32)
    onesb = same.astype(np.float32)
    kv = np.arange(256)
    ind64 = ((kv[:, None] // HEAD_V) == (kv[None, :] // HEAD_V)).astype(np.float32)
    kg = np.arange(GLA_KW)
    indgl = ((kg[:, None] // GLA_DK) == (kv[None, :] // HEAD_V)).astype(np.float32)
    diff = jnp.asarray((t[:, None] - t[None, :]).astype(np.float32))
    dm = jnp.concatenate([jnp.where(jnp.asarray(causal), jnp.exp(diff * log_gamma[h]), 0.0)
                          for h in range(N_HEAD)], axis=1)
    rtdec = jnp.broadcast_to(jnp.exp(float(seglen) * jnp.repeat(log_gamma, HEAD_V))[:, None], (256, 256))
    ones_t = np.ones((tl, 256), np.float32)
    return (jnp.asarray(tri, BF16), jnp.asarray(trib, BF16), jnp.asarray(onesb, BF16),
            jnp.asarray(ind64, BF16), jnp.asarray(indgl, BF16), jnp.asarray(ind64, F32),
            jnp.asarray(indgl, F32), dm, rtdec, jnp.asarray(ones_t, BF16))


def _mixer(proj, states, shift, cosf, sins, lp, c, c_rw, nseg, seglen, read_state):
    ng, rows, _ = proj.shape
    tl = nseg * seglen
    nj = rows // tl
    layer = lp['layer']
    consts = _mixer_constants(nseg, seglen, c_rw, lp['log_gamma'])

    def seq_spec(w):
        return pl.BlockSpec((None, tl, w), lambda g, j: (g, j, 0))

    def const_spec(a):
        nd = a.ndim
        return pl.BlockSpec(a.shape, lambda g, j: (0,) * nd)

    pos_spec = pl.BlockSpec((tl, 256), lambda g, j: (j, 0))
    params = (lp['p256'], lp['glba'], lp['mu'], lp['wa2'], lp['w2'], lp['a2'], lp['g2'])
    state_specs = [pl.BlockSpec((None, nseg) + s.shape[2:], lambda g, j: (layer, g, 0, 0, 0)) for s in states]
    in_specs = [seq_spec(N_MIX)]
    args = [proj]
    if read_state:
        in_specs += state_specs + [pl.BlockSpec((None, nseg, 1, RW_COLS), lambda g, j: (layer, g, 0, 0))]
        args += list(states) + [shift]
        aliases = {1 + i: 1 + i for i in range(len(states))}
    in_specs += [pos_spec, pos_spec] + [const_spec(a) for a in params] + [const_spec(a) for a in consts]
    args += [cosf, sins, *params, *consts]
    if not read_state:
        aliases = {len(args) + i: 1 + i for i in range(len(states))}
        in_specs += [pl.BlockSpec(memory_space=pl.ANY)] * len(states)
        args += list(states)
    out_specs = [seq_spec(N_BRANCH * BRANCH_W)] + state_specs
    out_shape = ([jax.ShapeDtypeStruct((ng, rows, N_BRANCH * BRANCH_W), F32)]
                 + [jax.ShapeDtypeStruct(s.shape, F32) for s in states])
    nblk = tl // c_rw
    scratch = [pltpu.VMEM((nseg, 256, 256), F32), pltpu.VMEM((nseg, GLA_KW, 256), F32),
               pltpu.VMEM((nseg, 256, 256), F32), pltpu.VMEM((nseg, 256, 256), F32),
               pltpu.VMEM((nseg, 1, RW_COLS), F32),
               pltpu.VMEM((nblk, 256, 256), BF16), pltpu.VMEM((nblk, 256, 256), F32)]
    return pl.pallas_call(
        functools.partial(_mixer_kernel, c=c, c_rw=c_rw, nseg=nseg, seglen=seglen, nj=nj,
                          has_state=read_state, n_alias=0 if read_state else len(states)),
        grid=(ng, nj),
        in_specs=in_specs,
        out_specs=out_specs,
        out_shape=out_shape,
        scratch_shapes=scratch,
        input_output_aliases=aliases,
        compiler_params=pltpu.CompilerParams(dimension_semantics=("arbitrary", "arbitrary"),
                                             vmem_limit_bytes=VMEM_LIMIT_BYTES),
        name="mixer",
    )(*args)


def _tile_heads(v):
    return jnp.tile(v, N_HEAD)


def _layer_params(l, lower_bounds, hg_norm_w, gla_wa2, gla_ba, gla_norm_w, rw_mu, rw_w0, rw_w2,
                  rw_a0, rw_a2, rw_g2, rw_kk, rw_ka, rw_rk, rw_ln_w, rw_ln_b):
    log_gamma = jnp.log1p(-jnp.exp2(-5.0 - jnp.arange(N_HEAD, dtype=F32)))
    rows = [lower_bounds[l], _tile_heads(hg_norm_w[l]), _tile_heads(gla_norm_w[l]), rw_w0[l], rw_a0[l],
            rw_kk[l], rw_ka[l], rw_rk[l], rw_ln_w[l], rw_ln_b[l], jnp.repeat(log_gamma, HEAD_V)]
    p256 = jnp.concatenate([jnp.stack(rows), jnp.zeros((16 - len(rows), 256), F32)], axis=0)
    zeros64 = jnp.zeros((64, 256), F32)
    return {
        'log_gamma': log_gamma,
        'p256': p256,
        'glba': jnp.concatenate([gla_ba[l][None, :], jnp.zeros((7, GLA_KW), F32)], axis=0),
        'mu': rw_mu[l][None, :],
        'wa2': jnp.concatenate([gla_wa2[l], jnp.zeros((LANES - GLA_RANK, GLA_KW), F32)], axis=0).astype(BF16),
        'w2': jnp.concatenate([rw_w2[l], zeros64], axis=0).astype(BF16),
        'a2': jnp.concatenate([zeros64, rw_a2[l]], axis=0).astype(BF16),
        'g2': rw_g2[l].astype(BF16),
    }


def _rotary_tables(pos):
    half = HEAD_V // 2
    inv = ROPE_BASE ** (-jnp.arange(half, dtype=F32) / half)
    ang = pos[:, None] * inv[None, :]
    cos, sin = jnp.cos(ang), jnp.sin(ang)
    cosf = jnp.tile(jnp.concatenate([cos, cos], axis=1), (1, N_HEAD))
    sins = jnp.tile(jnp.concatenate([-sin, sin], axis=1), (1, N_HEAD))
    return cosf, sins


def _group_layer(x, states, shift, tables, lp, attn_nw, ffn_nw, final_nw, final, c, c_rw, nseg, seglen,
                 read_state):
    nb, length, _ = x.shape
    xf = x.reshape(nb * length, D_MODEL)
    proj = _inproj(xf, attn_nw, lp['wmix'], lp['layer'])
    o, *outs = _mixer(proj.reshape(nb // nseg, nseg * length, N_MIX), states, shift,
                      tables[0], tables[1], lp, c, c_rw, nseg, seglen, read_state)
    x2 = _post(xf, o.reshape(nb * length, N_BRANCH * BRANCH_W), attn_nw, lp['wg'], lp['wb'], lp['wo'],
               ffn_nw, lp['wi'], lp['wfo'], final_nw, final, lp['layer'])
    new_shift = proj.reshape(nb, length, N_MIX)[:, length - 1, C_RW:C_RW + RW_COLS]
    return x2.reshape(nb, length, D_MODEL), outs, new_shift


def _states_out(outs, shifts):
    n_hg, n_gl, n_rw, n_rt = outs
    return (n_hg, n_gl, n_rw, jnp.stack(shifts), n_rt)


def kernel(x_prompt, x_sample, state_hgrn, state_gla, state_rwkv, state_rwkv_shift, state_ret,
           attn_norm_w, w_in, hg_lb_logits, hg_norm_w, gla_wa2, gla_ba, gla_norm_w,
           rw_mu, rw_w0, rw_w2, rw_a0, rw_a2, rw_g2, rw_kk, rw_ka, rw_rk, rw_ln_w, rw_ln_b,
           w_branch, w_out, ffn_norm_w, w_ffn_in, w_ffn_out, final_norm_w):
    depth = w_in.shape[0]
    lb_p = jax.nn.softmax(hg_lb_logits.astype(F32), axis=0)
    lower_bounds = jnp.cumsum(lb_p, axis=0) - lb_p[0:1]

    bp, lp_len, _ = x_prompt.shape
    bs, ls_len, _ = x_sample.shape
    tab_p = _rotary_tables(jnp.arange(lp_len, dtype=F32))
    tab_s = _rotary_tables(float(PAST_LEN) + jnp.arange(ls_len, dtype=F32))
    tab_s = tuple(jnp.tile(t, (SAMPLE_SEQS_PER_TILE, 1)) for t in tab_s)
    final_nw = final_norm_w[None, :]
    wmix, wg = _prep_in_weights(w_in)
    dense = {'wmix': wmix, 'wg': wg, 'wb': w_branch.astype(BF16), 'wo': w_out.astype(BF16),
             'wi': w_ffn_in.astype(BF16), 'wfo': w_ffn_out.astype(BF16)}

    s_states = [state_hgrn, state_gla, state_rwkv, state_ret]
    p_states = [jnp.zeros((depth, bp) + s.shape[2:], F32) for s in s_states]
    s_shift = state_rwkv_shift.reshape(depth, bs, 1, RW_COLS)
    xp, xs = x_prompt, x_sample
    p_shifts, s_shifts = [], []
    for l in range(depth):
        lp = _layer_params(l, lower_bounds, hg_norm_w, gla_wa2, gla_ba, gla_norm_w, rw_mu, rw_w0,
                           rw_w2, rw_a0, rw_a2, rw_g2, rw_kk, rw_ka, rw_rk, rw_ln_w, rw_ln_b)
        lp.update(dense)
        lp['layer'] = l
        final = l == depth - 1
        anw, fnw = attn_norm_w[l][None, :], ffn_norm_w[l][None, :]
        xp, p_states, sh = _group_layer(xp, p_states, None, tab_p, lp, anw, fnw, final_nw, final, c=8,
                                        c_rw=32, nseg=1, seglen=PROMPT_TILE, read_state=False)
        p_shifts.append(sh)
        xs, s_states, sh = _group_layer(xs, s_states, s_shift, tab_s, lp, anw, fnw, final_nw, final,
                                        c=ls_len, c_rw=ls_len, nseg=SAMPLE_SEQS_PER_TILE, seglen=ls_len,
                                        read_state=True)
        s_shifts.append(sh)

    return (xp, xs) + _states_out(p_states, p_shifts) + _states_out(s_states, s_shifts)
```

```python
import functools
import math

import jax
import jax.numpy as jnp
import numpy as np
from jax import lax
from jax.experimental import pallas as pl
from jax.experimental.pallas import tpu as pltpu

F32 = jnp.float32
BF16 = jnp.bfloat16

D_MODEL = 1024
N_HEAD = 4
HEAD_V = 64
BRANCH_W = N_HEAD * HEAD_V
N_BRANCH = 4
GLA_DK = 32
GLA_KW = N_HEAD * GLA_DK
GLA_RANK = 16
GLA_TAU = 16.0
RW_COLS = 1024
NORM_EPS = 1e-6
RW_LN_EPS = 64e-5
MIN_FORGET = 1e-30
ROPE_BASE = 10000.0
PAST_LEN = 16384
LOG2E = 1.4426950408889634
D_FF = 2816
N_MIX = 3968
PROMPT_TILE = 256
SAMPLE_SEQS_PER_TILE = 8
N_GATE = N_BRANCH * D_MODEL

VMEM_LIMIT_BYTES = 56 * 1024 * 1024
LANES = 128
SUBLANES = 8
BF16_ROWS = 16

C_HG_Q, C_HG_F, C_HG_I, C_HG_G = 0, 256, 512, 768
C_GL_Q, C_GL_K, C_GL_V, C_GL_A, C_GL_G = 1024, 1152, 1280, 1536, 1664
C_RW = 1920
C_RT_Q, C_RT_K, C_RT_V, C_RT_G = 2944, 3200, 3456, 3712

P_LB, P_HG_NW, P_GL_NW, P_W0, P_A0, P_KK, P_KA, P_RK, P_LNW, P_LNB, P_RT_LG = range(11)

_NN = (((1,), (0,)), ((), ()))
_NT = (((1,), (1,)), ((), ()))
_TN = (((0,), (0,)), ((), ()))


def _dg(a, b, dn):
    return lax.dot_general(a, b, dn, preferred_element_type=F32)


def _split(x):
    hi = x.astype(BF16)
    lo = (x - hi.astype(F32)).astype(BF16)
    return hi, lo


def _split3(x):
    x1 = x.astype(BF16)
    r = x - x1.astype(F32)
    x2 = r.astype(BF16)
    x3 = (r - x2.astype(F32)).astype(BF16)
    return x1, x2, x3


def _dot1(a, b, dn=_NN):
    return _dg(a.astype(BF16), b.astype(BF16), dn)


def _dots(pieces, b_exact, dn=_NN):
    out = _dg(pieces[0], b_exact, dn)
    for p in pieces[1:]:
        out = out + _dg(p, b_exact, dn)
    return out


def _dotsr(a_exact, pieces, dn=_NN):
    out = _dg(a_exact, pieces[0], dn)
    for p in pieces[1:]:
        out = out + _dg(a_exact, p, dn)
    return out


def _headsum(x, ind):
    return _dg(x.astype(BF16), ind, _NN)


def _sigmoid(x):
    return jax.nn.sigmoid(x)


def _silu(x):
    return x * jax.nn.sigmoid(x)


def _rms(x, w):
    return x * lax.rsqrt(jnp.mean(x * x, axis=-1, keepdims=True) + NORM_EPS) * w


def _cat(parts, axis=0):
    return parts[0] if len(parts) == 1 else jnp.concatenate(parts, axis=axis)


def _by_rows(fn, rows, chunk, start=0):
    outs = [fn(slice(i, min(i + chunk, rows))) for i in range(start, rows, chunk)]
    if isinstance(outs[0], tuple):
        return tuple(_cat([o[j] for o in outs]) for j in range(len(outs[0])))
    return _cat(outs)


def _inproj_kernel(x_ref, nw_ref, w_ref, o_ref):
    h = _rms(x_ref[...], nw_ref[...]).astype(BF16)
    pad0, pad1 = C_GL_A + GLA_RANK, C_GL_A + LANES
    o_ref[:, 0:pad1] = _dg(h, w_ref[0:pad1, :], _NT)
    o_ref[:, pad1:N_MIX] = _dg(h, w_ref[pad0:GATE_COL0, :], _NT)


def _merge_body(x, o_ref, nw_ref, wg_ref, wb_ref, wo_ref):
    h = _rms(x, nw_ref[...]).astype(BF16)
    merged = None
    for b in range(N_BRANCH):
        gl = _dg(h, wg_ref[b * D_MODEL:(b + 1) * D_MODEL, :], _NT)
        ob = o_ref[:, b * BRANCH_W:(b + 1) * BRANCH_W].astype(BF16)
        up = jnp.dot(ob, wb_ref[b], preferred_element_type=F32)
        t = _sigmoid(gl) * up
        merged = t if merged is None else merged + t
    return x + jnp.dot(merged.astype(BF16), wo_ref[...], preferred_element_type=F32)


FFN_CHUNK = 1408


def _ffn_body(x, nw_ref, wi_ref, wo_ref):
    h = _rms(x, nw_ref[...]).astype(BF16)
    acc = None
    for j in range(D_FF // FFN_CHUNK):
        lo = j * FFN_CHUNK
        g = jnp.dot(h, wi_ref[:, lo:lo + FFN_CHUNK], preferred_element_type=F32)
        u = jnp.dot(h, wi_ref[:, D_FF + lo:D_FF + lo + FFN_CHUNK], preferred_element_type=F32)
        a = (_silu(g) * u).astype(BF16)
        t = jnp.dot(a, wo_ref[lo:lo + FFN_CHUNK, :], preferred_element_type=F32)
        acc = t if acc is None else acc + t
    return x + acc


def _post_kernel(x_ref, o_ref, anw_ref, wg_ref, wb_ref, wo_ref, fnw_ref, wi_ref, wfo_ref, fw_ref, out_ref,
                 *, final):
    x1 = _merge_body(x_ref[...], o_ref, anw_ref, wg_ref, wb_ref, wo_ref)
    x2 = _ffn_body(x1, fnw_ref, wi_ref, wfo_ref)
    if final:
        x2 = _rms(x2, fw_ref[...])
    out_ref[...] = x2


def _dense_params():
    return pltpu.CompilerParams(dimension_semantics=("arbitrary",), vmem_limit_bytes=VMEM_LIMIT_BYTES)


def _const_spec(shape):
    nd = len(shape)
    return pl.BlockSpec(shape, lambda i: (0,) * nd)


def _token_tile(t):
    return 512 if t % 512 == 0 else t


GATE_COL0 = 3856
INPROJ_TILE = 1024


def _prep_in_weights(w_in):
    wt = jnp.swapaxes(w_in, 1, 2).astype(BF16)
    return wt, wt[:, GATE_COL0:]


def _inproj(x, nw, wmix, l):
    t = x.shape[0]
    tm = INPROJ_TILE if t % INPROJ_TILE == 0 and t > INPROJ_TILE else _token_tile(t)
    return pl.pallas_call(
        _inproj_kernel,
        grid=(t // tm,),
        in_specs=[pl.BlockSpec((tm, D_MODEL), lambda i: (i, 0)),
                  _const_spec((1, D_MODEL)), pl.BlockSpec((None, GATE_COL0, D_MODEL), lambda i: (l, 0, 0))],
        out_specs=pl.BlockSpec((tm, N_MIX), lambda i: (i, 0)),
        out_shape=jax.ShapeDtypeStruct((t, N_MIX), F32),
        compiler_params=_dense_params(),
        name="inproj",
    )(x, nw, wmix)


def _post(x, o, anw, wg, wb, wo, fnw, wi, wfo, fw, final, l):
    t = x.shape[0]
    tm = _token_tile(t)
    return pl.pallas_call(
        functools.partial(_post_kernel, final=final),
        grid=(t // tm,),
        in_specs=[pl.BlockSpec((tm, D_MODEL), lambda i: (i, 0)),
                  pl.BlockSpec((tm, N_BRANCH * BRANCH_W), lambda i: (i, 0)),
                  _const_spec((1, D_MODEL)), pl.BlockSpec((None, N_GATE, D_MODEL), lambda i: (l, 0, 0)),
                  pl.BlockSpec((None, N_BRANCH, BRANCH_W, D_MODEL), lambda i: (l, 0, 0, 0)),
                  pl.BlockSpec((None, D_MODEL, D_MODEL), lambda i: (l, 0, 0)),
                  _const_spec((1, D_MODEL)), pl.BlockSpec((None, D_MODEL, 2 * D_FF), lambda i: (l, 0, 0)),
                  pl.BlockSpec((None, D_FF, D_MODEL), lambda i: (l, 0, 0)), _const_spec((1, D_MODEL))],
        out_specs=pl.BlockSpec((tm, D_MODEL), lambda i: (i, 0)),
        out_shape=jax.ShapeDtypeStruct((t, D_MODEL), F32),
        compiler_params=_dense_params(),
        name="post",
    )(x, o, anw, wg, wb, wo, fnw, wi, wfo, fw)


def _stack_heads(x, hm):
    if x.shape[0] % BF16_ROWS == 0:
        x = x.astype(BF16)
        return jnp.concatenate([x * hm[h].astype(BF16) for h in range(N_HEAD)], axis=0)
    return jnp.concatenate([x * hm[h] for h in range(N_HEAD)], axis=0)


def _state_in(s, mask):
    return jnp.concatenate([s] * N_HEAD, axis=1) * mask


def _state_out(st):
    return (st[:, 0:HEAD_V] + st[:, HEAD_V:2 * HEAD_V]) + (st[:, 2 * HEAD_V:3 * HEAD_V] + st[:, 3 * HEAD_V:])


def _gla_tile(res, q, k, v, g3, tri, ones_t, ind, hm_k, hm_v, st_ref, mask, c, nseg, seglen):
    tl = q.shape[0]
    nblk = seglen // c
    b = _dotsr(tri, g3)
    yield
    rowi = lax.broadcasted_iota(jnp.int32, (c, q.shape[1]), 0)
    group = 4 * c
    accs, qes = [], []
    for g0 in range(0, tl, group):
        pieces, vrots = [], []
        for r0 in range(g0, g0 + group, c):
            qb, kb, vb = q[r0:r0 + c], k[r0:r0 + c], v[r0:r0 + c]
            bb = b[r0:r0 + c]
            b2 = bb * LOG2E
            qes.append(qb * jnp.exp2(b2))
            pieces.append(qb * kb)
            vr = [vb]
            for d in range(1, c):
                e = jnp.exp2(b2 - pltpu.roll(b2, d, 0))
                pieces.append(jnp.where(rowi >= d, qb * pltpu.roll(kb, d, 0) * e, 0.0))
                vr.append(pltpu.roll(vb, d, 0))
            vrots.append(vr)
        a = _dot1(jnp.concatenate(pieces, axis=0), ind)
        for i, vr in enumerate(vrots):
            acc = a[i * c * c:i * c * c + c] * vr[0]
            for d in range(1, c):
                acc = acc + a[(i * c + d) * c:(i * c + d + 1) * c] * vr[d]
            accs.append(acc)
        yield
    acc = _cat(accs)
    qe = _cat(qes)
    if nblk > 1:
        scores = []
        for seg in range(nseg):
            base = seg * seglen
            for i in range(1, nblk):
                lo = base + i * c
                r = b[lo - 1:lo, :]
                qi = q[lo:lo + c] * jnp.exp(b[lo:lo + c] - r)
                ki = _by_rows(lambda s: k[s] * jnp.exp(r - b[s]), lo, 2 * BF16_ROWS, base)
                scores.append(_dot1(_stack_heads(qi, hm_k), ki, _NT))
                if i % 3 == 0:
                    yield
        parts = []
        n = 0
        for seg in range(nseg):
            base = seg * seglen
            parts.append(jnp.zeros((c, N_HEAD * HEAD_V), F32))
            for i in range(1, nblk):
                r4 = _dot1(scores[n], v[base:base + i * c])
                n += 1
                oi = r4[0:c] * hm_v[0]
                for h in range(1, N_HEAD):
                    oi = oi + r4[h * c:(h + 1) * c] * hm_v[h]
                parts.append(oi)
                if i % 3 == 0:
                    yield
        acc = acc + jnp.concatenate(parts, axis=0)
    outs = []
    for seg in range(nseg):
        lo, hi = seg * seglen, (seg + 1) * seglen
        blast = b[hi - 1:hi, :]
        dcol = jnp.exp(_dots([p[lo:hi] for p in g3], ones_t[lo:hi], _TN))
        st = st_ref[seg]
        outs.append(acc[lo:hi] + _dot1(qe[lo:hi], st))
        ke = _by_rows(lambda s: k[s] * jnp.exp(blast - b[s]), hi, 2 * BF16_ROWS, lo)
        st_ref[seg] = st * dcol + _dot1(ke, v[lo:hi], _TN) * mask
        yield
    res.append(_cat(outs))
    yield


def _ret_tile(res, q, k, v, dm, lg, rtdec, hm, st_ref, mask, nseg, seglen):
    tau = (lax.broadcasted_iota(jnp.int32, q.shape, 0) & (seglen - 1)).astype(F32)
    a = _dot1(q, _stack_heads(k, hm), _NT) * dm
    yield
    intra = _dot1(a, _stack_heads(v, hm))
    qe = q * jnp.exp((tau + 1.0) * lg)
    ke = k * jnp.exp((float(seglen) - 1.0 - tau) * lg)
    yield
    outs = []
    for seg in range(nseg):
        lo, hi = seg * seglen, (seg + 1) * seglen
        st = st_ref[seg]
        outs.append(intra[lo:hi] + _dot1(qe[lo:hi], st))
        st_ref[seg] = st * rtdec + _dot1(ke[lo:hi], v[lo:hi], _TN) * mask
    res.append(_cat(outs))
    yield


def _head_rms_gate(o, ind64, nw, gate):
    ms = _headsum(o * o, ind64) * (1.0 / HEAD_V)
    return o * lax.rsqrt(ms + NORM_EPS) * nw * _silu(gate)


def _rwkv_blocks(kt, ah, kh, rt, ap, kp, v, dec, hmasks, m256, m_scr, c_scr, c, tick):
    nblk = kt.shape[0] // c
    blocks = range(nblk)
    r4 = lax.broadcasted_iota(jnp.int32, (c, N_HEAD * c), 0)
    s4 = lax.broadcasted_iota(jnp.int32, (c, N_HEAD * c), 1) & (c - 1)
    strict = s4 < r4
    incl = s4 <= r4
    rr = lax.broadcasted_iota(jnp.int32, (N_HEAD * c, N_HEAD * c), 0)
    cc = lax.broadcasted_iota(jnp.int32, (N_HEAD * c, N_HEAD * c), 1)
    sh = int(math.log2(c))
    bd4 = ((rr >> sh) == (cc >> sh)).astype(F32)
    eye4 = (rr == cc).astype(F32)
    eye256 = (lax.broadcasted_iota(jnp.int32, (256, 256), 0) == lax.broadcasted_iota(jnp.int32, (256, 256), 1))

    def blk(x, i):
        return x[i * c:(i + 1) * c]

    def stack(x):
        return _stack_heads(x, hmasks)

    def same(size):
        s = int(math.log2(size))
        return (rr >> s) == (cc >> s)

    kr = [jnp.concatenate([blk(kt, i), blk(rt, i)], axis=0).astype(BF16) for i in blocks]
    la = [_dot1(kr[i], stack(blk(ah, i)), _NT) for i in blocks]
    lk = [_dot1(kr[i], stack(blk(kh, i)), _NT) for i in blocks]
    tick()
    l_a = [jnp.where(strict, x[:c], 0.0) for x in la]
    m_a = [jnp.where(incl, x[c:], 0.0) for x in la]
    l_k = [jnp.where(strict, x[:c], 0.0) for x in lk]
    m_k = [jnp.where(incl, x[c:], 0.0) for x in lk]
    lbd = [jnp.concatenate([x] * N_HEAD, axis=0) * bd4 for x in l_a]
    tm = [eye4 - jnp.where(same(2), x, 0.0) for x in lbd]
    size = 2
    while size < c:
        cross = same(2 * size) & jnp.logical_not(same(size))
        xm = [_dot1(tm[i], jnp.where(cross, lbd[i], 0.0)) for i in blocks]
        tick()
        tm = [tm[i] - _dot1(xm[i], tm[i]) for i in blocks]
        tick()
        size *= 2
    ts = []
    for x in tm:
        t = x[0:c]
        for h in range(1, N_HEAD):
            t = t + x[h * c:(h + 1) * c]
        ts.append(t)
    vs = [stack(blk(v, i)).astype(BF16) for i in blocks]
    lkv = [_dot1(l_k[i], vs[i]) for i in blocks]
    mkv = [_dot1(m_k[i], vs[i]) for i in blocks]
    tick()
    kbar = [_dot1(ts[i], stack(blk(kt, i))) for i in blocks]
    u0 = [-_dot1(ts[i], stack(lkv[i])) for i in blocks]
    tick()
    for i in blocks:
        m = jnp.where(eye256, dec[i * c:i * c + 1], 0.0) - _dot1(blk(ap, i), kbar[i], _TN) * m256
        m_scr[i] = m.astype(BF16)
    tick()
    for i in blocks:
        c_scr[i] = _dot1(jnp.concatenate([blk(ap, i), blk(kp, i)], axis=0),
                         jnp.concatenate([u0[i], blk(v, i)], axis=0), _TN) * m256
    return kbar, u0, m_a, mkv


N_MIXER_SHARED_INPUTS = 19


def _mixer_kernel(*refs, c, c_rw, nseg, seglen, nj, has_state, n_alias):
    refs = list(refs)
    x_ref = refs.pop(0)
    if has_state:
        shg_ref, sgl_ref, srw_ref, srt_ref, shift_ref = refs[:5]
        refs = refs[5:]
    (cos_ref, sin_ref, p256_ref, glba_ref, mu_ref, wa2_ref, w2_ref, a2_ref, g2_ref,
     tri_ref, trib_ref, onesb_ref, ind64_ref, indgl_ref, m256_ref, mgl_ref, dm_ref, rtdec_ref,
     ones_ref) = refs[:N_MIXER_SHARED_INPUTS]
    (o_ref, nhg_ref, ngl_ref, nrw_ref, nrt_ref,
     st_hg, st_gl, st_rw, st_rt, carry, m_scr, c_scr) = refs[N_MIXER_SHARED_INPUTS + n_alias:]
    tl = nseg * seglen
    nblk_seg = seglen // c_rw
    j = pl.program_id(1)

    def prm(i):
        return p256_ref[i:i + 1, :]

    tri = tri_ref[...]
    trib = trib_ref[...]
    onesb = onesb_ref[...]
    ones_t = ones_ref[...]
    ind64 = ind64_ref[...]
    indgl = indgl_ref[...]
    m256 = m256_ref[...]
    mgl = mgl_ref[...]
    lane = lax.broadcasted_iota(jnp.int32, (1, 256), 1)
    hmasks = [((lane >> 6) == h).astype(F32) for h in range(N_HEAD)]
    lane_k = lax.broadcasted_iota(jnp.int32, (1, GLA_KW), 1)
    hmasks_glk = [((lane_k >> 5) == h).astype(F32) for h in range(N_HEAD)]

    @pl.when(j == 0)
    def _():
        for seg in range(nseg):
            if has_state:
                st_hg[seg] = _state_in(shg_ref[seg].reshape(256, HEAD_V), m256)
                st_gl[seg] = _state_in(sgl_ref[seg].reshape(GLA_KW, HEAD_V), mgl)
                st_rw[seg] = _state_in(srw_ref[seg].reshape(256, HEAD_V), m256)
                st_rt[seg] = _state_in(srt_ref[seg].reshape(256, HEAD_V), m256)
                carry[seg] = shift_ref[seg]
            else:
                st_hg[seg] = jnp.zeros((256, 256), F32)
                st_gl[seg] = jnp.zeros((GLA_KW, 256), F32)
                st_rw[seg] = jnp.zeros((256, 256), F32)
                st_rt[seg] = jnp.zeros((256, 256), F32)
                carry[seg] = jnp.zeros((1, RW_COLS), F32)

    def cols(c0, w):
        return x_ref.at[:, c0:c0 + w]

    def hgrn_work(res):
        lb = prm(P_LB)
        xq, xf = cols(C_HG_Q, 256), cols(C_HG_F, 256)

        def prep(s):
            forget = lb + (1.0 - lb) * _sigmoid(xf[s])
            return (_silu(xq[s]), 1.0 - forget) + _split3(jnp.log(jnp.maximum(forget, MIN_FORGET)))

        q, k, g1, g2_, g3_ = _by_rows(prep, tl, BF16_ROWS)
        yield
        yield from _gla_tile(res, q, k, cols(C_HG_I, 256), (g1, g2_, g3_), tri, ones_t, ind64,
                             hmasks, hmasks, st_hg, m256, c, nseg, seglen)
        o_ref[:, 0:256] = _head_rms_gate(res.pop(), ind64, prm(P_HG_NW), x_ref[:, C_HG_G:C_HG_G + 256])
        yield

    def gla_work(res):
        za = _dot1(x_ref[:, C_GL_A:C_GL_A + LANES], wa2_ref[...]) + glba_ref[0:1, :]

        def prep(s):
            z = za[s]
            return _split3((jnp.minimum(z, 0.0) - jnp.log1p(jnp.exp(-jnp.abs(z)))) / GLA_TAU)

        g3 = _by_rows(prep, tl, BF16_ROWS)
        yield
        yield from _gla_tile(res, x_ref[:, C_GL_Q:C_GL_Q + GLA_KW] * GLA_DK ** -0.5,
                             x_ref[:, C_GL_K:C_GL_K + GLA_KW], cols(C_GL_V, 256), g3,
                             tri, ones_t, indgl, hmasks_glk, hmasks, st_gl, mgl, c, nseg, seglen)
        o_ref[:, 256:512] = _head_rms_gate(res.pop(), ind64, prm(P_GL_NW), x_ref[:, C_GL_G:C_GL_G + 256])
        yield

    def ret_work(res):
        upper = (lane & (HEAD_V // 2)) != 0
        xq, xk = cols(C_RT_Q, 256), cols(C_RT_K, 256)

        def rot(x, s):
            sw = jnp.where(upper, pltpu.roll(x, HEAD_V // 2, 1), pltpu.roll(x, 256 - HEAD_V // 2, 1))
            return x * cos_ref[s, :] + sw * sin_ref[s, :]

        q_r, k_r = _by_rows(lambda s: (rot(xq[s], s), rot(xk[s], s) * HEAD_V ** -0.5), tl, SUBLANES)
        yield
        yield from _ret_tile(res, q_r, k_r, x_ref[:, C_RT_V:C_RT_V + 256], dm_ref[...], prm(P_RT_LG),
                             rtdec_ref[...], hmasks, st_rt, m256, nseg, seglen)
        ort = res.pop()
        ms = _headsum(ort * ort, ind64) * (1.0 / HEAD_V)
        o_ref[:, 768:1024] = ort * lax.rsqrt(ms + NORM_EPS) * _silu(x_ref[:, C_RT_G:C_RT_G + 256])
        yield

    work = [ret_work([]), hgrn_work([]), gla_work([])]

    def advance(n):
        for _ in range(n):
            while work:
                try:
                    next(work[0])
                    break
                except StopIteration:
                    work.pop(0)

    xrw = cols(C_RW, RW_COLS)
    first = lax.broadcasted_iota(jnp.int32, (SUBLANES, RW_COLS), 0) == 0
    mu = mu_ref[...]

    def mix_rows(s):
        rw = xrw[s]
        if s.start % seglen == 0:
            prev0 = carry[s.start // seglen]
        else:
            prev0 = xrw[s.start - 1:s.start]
        prev = jnp.where(first, jnp.broadcast_to(prev0, (SUBLANES, RW_COLS)), pltpu.roll(rw, 1, 0))
        mix = rw + (prev - rw) * mu
        k0 = pltpu.roll(mix[:, 256:640], 320, 1)[:, 0:256]
        v_ = pltpu.roll(mix[:, 512:896], 320, 1)[:, 0:256]
        kk0 = k0 * prm(P_KK)
        return (mix[:, 0:256], k0, v_, jnp.tanh(mix[:, 256:384]), mix[:, 768:896],
                _sigmoid(mix[:, 896:1024]), kk0, kk0 * kk0)

    r, k0, v, t_w, s_a, s_g, kk0, kk0sq = _by_rows(mix_rows, tl, SUBLANES)
    for seg in range(nseg):
        hi = (seg + 1) * seglen
        carry[seg] = xrw[hi - 1:hi]
    lw_pre = _dot1(t_w, w2_ref[...])
    a_pre = _dot1(s_a, a2_ref[...])
    g = _dot1(s_g, g2_ref[...])
    n2 = _headsum(kk0sq, ind64)

    def gates(s):
        lw = -math.exp(-0.5) * _sigmoid(prm(P_W0) + lw_pre[s])
        a = _sigmoid(prm(P_A0) + a_pre[s])
        kk = kk0[s] / jnp.maximum(jnp.sqrt(n2[s]), 1e-12)
        k = k0[s] * (1.0 + (a - 1.0) * prm(P_KA))
        return (lw, a * kk, kk, k, r[s] * k * prm(P_RK)) + _split3(lw)

    lw, alpha, kk, k, rkp, lw1, lw2, lw3 = _by_rows(gates, tl, BF16_ROWS)
    bw = _dotsr(trib, (lw1, lw2, lw3))
    bc = _dotsr(onesb, (lw1, lw2, lw3))

    def scaled(s):
        bws, bcs = bw[s], bc[s]
        e_neg = jnp.exp(-bws)
        e_rem = jnp.exp(bcs - bws)
        return (kk[s] * jnp.exp(bws - lw[s]), alpha[s] * e_neg, k[s] * e_neg, r[s] * jnp.exp(bws),
                alpha[s] * e_rem, k[s] * e_rem, jnp.exp(bcs))

    rw_kt, rw_ah, rw_kh, rw_rt, rw_ap, rw_kp, rw_dec = _by_rows(scaled, tl, SUBLANES)
    bonus = _headsum(rkp, ind64) * v

    kbar, u0, m_a, mkv = _rwkv_blocks(rw_kt, rw_ah, rw_kh, rw_rt, rw_ap, rw_kp, v, rw_dec,
                                      hmasks, m256, m_scr, c_scr, c_rw, lambda: advance(1))

    per_step = 9
    sts = [st_rw[seg] for seg in range(nseg)]
    outs = [[None] * nblk_seg for _ in range(nseg)]
    for i in range(nblk_seg):
        for seg in range(nseg):
            blk = seg * nblk_seg + i
            sth, stl = _split(sts[seg])
            kr = jnp.concatenate([kbar[blk], rw_rt[blk * c_rw:(blk + 1) * c_rw]], axis=0).astype(BF16)
            mb = m_scr[blk]
            res = _dg(jnp.concatenate([mb, kr], axis=0), sth, _NN)
            xr = res[256:]
            u = u0[blk] - xr[:c_rw]
            outs[seg][i] = xr[c_rw:] + _dot1(m_a[blk], _stack_heads(u, hmasks)) + mkv[blk]
            sts[seg] = (res[:256] + _dg(mb, stl, _NN)) + c_scr[blk]
        advance(per_step)
    while work:
        advance(1)
    for seg in range(nseg):
        st_rw[seg] = sts[seg]
    orw = _cat([o for seg_outs in outs for o in seg_outs])
    mean = _headsum(orw, ind64) * (1.0 / HEAD_V)
    xc = orw - mean
    var = _headsum(xc * xc, ind64) * (1.0 / HEAD_V)
    o_ref[:, 512:768] = (xc * lax.rsqrt(var + RW_LN_EPS) * prm(P_LNW) + prm(P_LNB) + bonus) * g

    @pl.when(j == nj - 1)
    def _():
        for seg in range(nseg):
            nhg_ref[seg] = _state_out(st_hg[seg]).reshape(N_HEAD, HEAD_V, HEAD_V)
            ngl_ref[seg] = _state_out(st_gl[seg]).reshape(N_HEAD, GLA_DK, HEAD_V)
            nrw_ref[seg] = _state_out(st_rw[seg]).reshape(N_HEAD, HEAD_V, HEAD_V)
            nrt_ref[seg] = _state_out(st_rt[seg]).reshape(N_HEAD, HEAD_V, HEAD_V)


def _mixer_constants(nseg, seglen, c_rw, log_gamma):
    tl = nseg * seglen
    t = np.arange(tl)
    same = (t[:, None] // c_rw) == (t[None, :] // c_rw)
    same_seg = (t[:, None] // seglen) == (t[None, :] // seglen)
    causal = (t[None, :] <= t[:, None]) & same_seg
    tri = causal.astype(np.float32)
    trib = (same & causal).astype(np.float32)
    onesb = same.astype(np.float32)
    kv = np.arange(256)
    ind64 = ((kv[:, None] // HEAD_V) == (kv[None, :] // HEAD_V)).astype(np.float32)
    kg = np.arange(GLA_KW)
    indgl = ((kg[:, None] // GLA_DK) == (kv[None, :] // HEAD_V)).astype(np.float32)
    diff = jnp.asarray((t[:, None] - t[None, :]).astype(np.float32))
    dm = jnp.concatenate([jnp.where(jnp.asarray(causal), jnp.exp(diff * log_gamma[h]), 0.0)
                          for h in range(N_HEAD)], axis=1)
    rtdec = jnp.broadcast_to(jnp.exp(float(seglen) * jnp.repeat(log_gamma, HEAD_V))[:, None], (256, 256))
    ones_t = np.ones((tl, 256), np.float32)
    return (jnp.asarray(tri, BF16), jnp.asarray(trib, BF16), jnp.asarray(onesb, BF16),
            jnp.asarray(ind64, BF16), jnp.asarray(indgl, BF16), jnp.asarray(ind64, F32),
            jnp.asarray(indgl, F32), dm, rtdec, jnp.asarray(ones_t, BF16))


def _mixer(proj, states, shift, cosf, sins, lp, c, c_rw, nseg, seglen, read_state):
    ng, rows, _ = proj.shape
    tl = nseg * seglen
    nj = rows // tl
    layer = lp['layer']
    consts = _mixer_constants(nseg, seglen, c_rw, lp['log_gamma'])

    def seq_spec(w):
        return pl.BlockSpec((None, tl, w), lambda g, j: (g, j, 0))

    def const_spec(a):
        nd = a.ndim
        return pl.BlockSpec(a.shape, lambda g, j: (0,) * nd)

    pos_spec = pl.BlockSpec((tl, 256), lambda g, j: (j, 0))
    params = (lp['p256'], lp['glba'], lp['mu'], lp['wa2'], lp['w2'], lp['a2'], lp['g2'])
    state_specs = [pl.BlockSpec((None, nseg) + s.shape[2:], lambda g, j: (layer, g, 0, 0, 0)) for s in states]
    in_specs = [seq_spec(N_MIX)]
    args = [proj]
    if read_state:
        in_specs += state_specs + [pl.BlockSpec((None, nseg, 1, RW_COLS), lambda g, j: (layer, g, 0, 0))]
        args += list(states) + [shift]
        aliases = {1 + i: 1 + i for i in range(len(states))}
    in_specs += [pos_spec, pos_spec] + [const_spec(a) for a in params] + [const_spec(a) for a in consts]
    args += [cosf, sins, *params, *consts]
    if not read_state:
        aliases = {len(args) + i: 1 + i for i in range(len(states))}
        in_specs += [pl.BlockSpec(memory_space=pl.ANY)] * len(states)
        args += list(states)
    out_specs = [seq_spec(N_BRANCH * BRANCH_W)] + state_specs
    out_shape = ([jax.ShapeDtypeStruct((ng, rows, N_BRANCH * BRANCH_W), F32)]
                 + [jax.ShapeDtypeStruct(s.shape, F32) for s in states])
    nblk = tl // c_rw
    scratch = [pltpu.VMEM((nseg, 256, 256), F32), pltpu.VMEM((nseg, GLA_KW, 256), F32),
               pltpu.VMEM((nseg, 256, 256), F32), pltpu.VMEM((nseg, 256, 256), F32),
               pltpu.VMEM((nseg, 1, RW_COLS), F32),
               pltpu.VMEM((nblk, 256, 256), BF16), pltpu.VMEM((nblk, 256, 256), F32)]
    return pl.pallas_call(
        functools.partial(_mixer_kernel, c=c, c_rw=c_rw, nseg=nseg, seglen=seglen, nj=nj,
                          has_state=read_state, n_alias=0 if read_state else len(states)),
        grid=(ng, nj),
        in_specs=in_specs,
        out_specs=out_specs,
        out_shape=out_shape,
        scratch_shapes=scratch,
        input_output_aliases=aliases,
        compiler_params=pltpu.CompilerParams(dimension_semantics=("arbitrary", "arbitrary"),
                                             vmem_limit_bytes=VMEM_LIMIT_BYTES),
        name="mixer",
    )(*args)


def _tile_heads(v):
    return jnp.tile(v, N_HEAD)


def _layer_params(l, lower_bounds, hg_norm_w, gla_wa2, gla_ba, gla_norm_w, rw_mu, rw_w0, rw_w2,
                  rw_a0, rw_a2, rw_g2, rw_kk, rw_ka, rw_rk, rw_ln_w, rw_ln_b):
    log_gamma = jnp.log1p(-jnp.exp2(-5.0 - jnp.arange(N_HEAD, dtype=F32)))
    rows = [lower_bounds[l], _tile_heads(hg_norm_w[l]), _tile_heads(gla_norm_w[l]), rw_w0[l], rw_a0[l],
            rw_kk[l], rw_ka[l], rw_rk[l], rw_ln_w[l], rw_ln_b[l], jnp.repeat(log_gamma, HEAD_V)]
    p256 = jnp.concatenate([jnp.stack(rows), jnp.zeros((16 - len(rows), 256), F32)], axis=0)
    zeros64 = jnp.zeros((64, 256), F32)
    return {
        'log_gamma': log_gamma,
        'p256': p256,
        'glba': jnp.concatenate([gla_ba[l][None, :], jnp.zeros((7, GLA_KW), F32)], axis=0),
        'mu': rw_mu[l][None, :],
        'wa2': jnp.concatenate([gla_wa2[l], jnp.zeros((LANES - GLA_RANK, GLA_KW), F32)], axis=0).astype(BF16),
        'w2': jnp.concatenate([rw_w2[l], zeros64], axis=0).astype(BF16),
        'a2': jnp.concatenate([zeros64, rw_a2[l]], axis=0).astype(BF16),
        'g2': rw_g2[l].astype(BF16),
    }


def _rotary_tables(pos):
    half = HEAD_V // 2
    inv = ROPE_BASE ** (-jnp.arange(half, dtype=F32) / half)
    ang = pos[:, None] * inv[None, :]
    cos, sin = jnp.cos(ang), jnp.sin(ang)
    cosf = jnp.tile(jnp.concatenate([cos, cos], axis=1), (1, N_HEAD))
    sins = jnp.tile(jnp.concatenate([-sin, sin], axis=1), (1, N_HEAD))
    return cosf, sins


def _group_layer(x, states, shift, tables, lp, attn_nw, ffn_nw, final_nw, final, c, c_rw, nseg, seglen,
                 read_state):
    nb, length, _ = x.shape
    xf = x.reshape(nb * length, D_MODEL)
    proj = _inproj(xf, attn_nw, lp['wmix'], lp['layer'])
    o, *outs = _mixer(proj.reshape(nb // nseg, nseg * length, N_MIX), states, shift,
                      tables[0], tables[1], lp, c, c_rw, nseg, seglen, read_state)
    x2 = _post(xf, o.reshape(nb * length, N_BRANCH * BRANCH_W), attn_nw, lp['wg'], lp['wb'], lp['wo'],
               ffn_nw, lp['wi'], lp['wfo'], final_nw, final, lp['layer'])
    new_shift = proj.reshape(nb, length, N_MIX)[:, length - 1, C_RW:C_RW + RW_COLS]
    return x2.reshape(nb, length, D_MODEL), outs, new_shift


def _states_out(outs, shifts):
    n_hg, n_gl, n_rw, n_rt = outs
    return (n_hg, n_gl, n_rw, jnp.stack(shifts), n_rt)


def kernel(x_prompt, x_sample, state_hgrn, state_gla, state_rwkv, state_rwkv_shift, state_ret,
           attn_norm_w, w_in, hg_lb_logits, hg_norm_w, gla_wa2, gla_ba, gla_norm_w,
           rw_mu, rw_w0, rw_w2, rw_a0, rw_a2, rw_g2, rw_kk, rw_ka, rw_rk, rw_ln_w, rw_ln_b,
           w_branch, w_out, ffn_norm_w, w_ffn_in, w_ffn_out, final_norm_w):
    depth = w_in.shape[0]
    lb_p = jax.nn.softmax(hg_lb_logits.astype(F32), axis=0)
    lower_bounds = jnp.cumsum(lb_p, axis=0) - lb_p[0:1]

    bp, lp_len, _ = x_prompt.shape
    bs, ls_len, _ = x_sample.shape
    tab_p = _rotary_tables(jnp.arange(lp_len, dtype=F32))
    tab_s = _rotary_tables(float(PAST_LEN) + jnp.arange(ls_len, dtype=F32))
    tab_s = tuple(jnp.tile(t, (SAMPLE_SEQS_PER_TILE, 1)) for t in tab_s)
    final_nw = final_norm_w[None, :]
    wmix, wg = _prep_in_weights(w_in)
    dense = {'wmix': wmix, 'wg': wg, 'wb': w_branch.astype(BF16), 'wo': w_out.astype(BF16),
             'wi': w_ffn_in.astype(BF16), 'wfo': w_ffn_out.astype(BF16)}

    s_states = [state_hgrn, state_gla, state_rwkv, state_ret]
    p_states = [jnp.zeros((depth, bp) + s.shape[2:], F32) for s in s_states]
    s_shift = state_rwkv_shift.reshape(depth, bs, 1, RW_COLS)
    xp, xs = x_prompt, x_sample
    p_shifts, s_shifts = [], []
    for l in range(depth):
        lp = _layer_params(l, lower_bounds, hg_norm_w, gla_wa2, gla_ba, gla_norm_w, rw_mu, rw_w0,
                           rw_w2, rw_a0, rw_a2, rw_g2, rw_kk, rw_ka, rw_rk, rw_ln_w, rw_ln_b)
        lp.update(dense)
        lp['layer'] = l
        final = l == depth - 1
        anw, fnw = attn_norm_w[l][None, :], ffn_norm_w[l][None, :]
        xp, p_states, sh = _group_layer(xp, p_states, None, tab_p, lp, anw, fnw, final_nw, final, c=8,
                                        c_rw=32, nseg=1, seglen=PROMPT_TILE, read_state=False)
        p_shifts.append(sh)
        xs, s_states, sh = _group_layer(xs, s_states, s_shift, tab_s, lp, anw, fnw, final_nw, final,
                                        c=ls_len, c_rw=ls_len, nseg=SAMPLE_SEQS_PER_TILE, seglen=ls_len,
                                        read_state=True)
        s_shifts.append(sh)

    return (xp, xs) + _states_out(p_states, p_shifts) + _states_out(s_states, s_shifts)
```

```python
import functools
import math

import jax
import jax.numpy as jnp
import numpy as np
from jax import lax
from jax.experimental import pallas as pl
from jax.experimental.pallas import tpu as pltpu

F32 = jnp.float32
BF16 = jnp.bfloat16

D_MODEL = 1024
N_HEAD = 4
HEAD_V = 64
BRANCH_W = N_HEAD * HEAD_V
N_BRANCH = 4
GLA_DK = 32
GLA_KW = N_HEAD * GLA_DK
GLA_RANK = 16
GLA_TAU = 16.0
RW_COLS = 1024
NORM_EPS = 1e-6
RW_LN_EPS = 64e-5
MIN_FORGET = 1e-30
ROPE_BASE = 10000.0
PAST_LEN = 16384
LOG2E = 1.4426950408889634
D_FF = 2816
N_MIX = 3968
PROMPT_TILE = 256
SAMPLE_SEQS_PER_TILE = 8
N_GATE = N_BRANCH * D_MODEL

VMEM_LIMIT_BYTES = 56 * 1024 * 1024
LANES = 128
SUBLANES = 8
BF16_ROWS = 16

C_HG_Q, C_HG_F, C_HG_I, C_HG_G = 0, 256, 512, 768
C_GL_Q, C_GL_K, C_GL_V, C_GL_A, C_GL_G = 1024, 1152, 1280, 1536, 1664
C_RW = 1920
C_RT_Q, C_RT_K, C_RT_V, C_RT_G = 2944, 3200, 3456, 3712

P_LB, P_HG_NW, P_GL_NW, P_W0, P_A0, P_KK, P_KA, P_RK, P_LNW, P_LNB, P_RT_LG = range(11)

_NN = (((1,), (0,)), ((), ()))
_NT = (((1,), (1,)), ((), ()))
_TN = (((0,), (0,)), ((), ()))


def _dg(a, b, dn):
    return lax.dot_general(a, b, dn, preferred_element_type=F32)


def _split(x):
    hi = x.astype(BF16)
    lo = (x - hi.astype(F32)).astype(BF16)
    return hi, lo


def _split3(x):
    x1 = x.astype(BF16)
    r = x - x1.astype(F32)
    x2 = r.astype(BF16)
    x3 = (r - x2.astype(F32)).astype(BF16)
    return x1, x2, x3


def _dot1(a, b, dn=_NN):
    return _dg(a.astype(BF16), b.astype(BF16), dn)


def _dots(pieces, b_exact, dn=_NN):
    out = _dg(pieces[0], b_exact, dn)
    for p in pieces[1:]:
        out = out + _dg(p, b_exact, dn)
    return out


def _dotsr(a_exact, pieces, dn=_NN):
    out = _dg(a_exact, pieces[0], dn)
    for p in pieces[1:]:
        out = out + _dg(a_exact, p, dn)
    return out


def _headsum(x, ind):
    return _dg(x.astype(BF16), ind, _NN)


def _sigmoid(x):
    return jax.nn.sigmoid(x)


def _silu(x):
    return x * jax.nn.sigmoid(x)


def _rms(x, w):
    return x * lax.rsqrt(jnp.mean(x * x, axis=-1, keepdims=True) + NORM_EPS) * w


def _cat(parts, axis=0):
    return parts[0] if len(parts) == 1 else jnp.concatenate(parts, axis=axis)


def _by_rows(fn, rows, chunk, start=0):
    outs = [fn(slice(i, min(i + chunk, rows))) for i in range(start, rows, chunk)]
    if isinstance(outs[0], tuple):
        return tuple(_cat([o[j] for o in outs]) for j in range(len(outs[0])))
    return _cat(outs)


def _inproj_kernel(x_ref, nw_ref, w_ref, o_ref):
    h = _rms(x_ref[...], nw_ref[...]).astype(BF16)
    pad0, pad1 = C_GL_A + GLA_RANK, C_GL_A + LANES
    o_ref[:, 0:pad1] = _dg(h, w_ref[0:pad1, :], _NT)
    o_ref[:, pad1:N_MIX] = _dg(h, w_ref[pad0:GATE_COL0, :], _NT)


def _merge_body(x, o_ref, nw_ref, wg_ref, wb_ref, wo_ref):
    h = _rms(x, nw_ref[...]).astype(BF16)
    obs = [o_ref[:, b * BRANCH_W:(b + 1) * BRANCH_W].astype(BF16) for b in range(N_BRANCH)]
    y = None
    for lo in range(0, D_MODEL, DENSE_CHUNK):
        merged = None
        for b in range(N_BRANCH):
            g0 = b * D_MODEL + lo
            gl = _dg(h, wg_ref[g0:g0 + DENSE_CHUNK, :], _NT)
            up = jnp.dot(obs[b], wb_ref[b, :, lo:lo + DENSE_CHUNK], preferred_element_type=F32)
            t = _sigmoid(gl) * up
            merged = t if merged is None else merged + t
        part = jnp.dot(merged.astype(BF16), wo_ref[lo:lo + DENSE_CHUNK, :], preferred_element_type=F32)
        y = part if y is None else y + part
    return x + y


DENSE_CHUNK = 256
FFN_CHUNK = DENSE_CHUNK


def _ffn_body(x, nw_ref, wi_ref, wo_ref):
    h = _rms(x, nw_ref[...]).astype(BF16)
    acc = None
    for j in range(D_FF // FFN_CHUNK):
        lo = j * FFN_CHUNK
        g = jnp.dot(h, wi_ref[:, lo:lo + FFN_CHUNK], preferred_element_type=F32)
        u = jnp.dot(h, wi_ref[:, D_FF + lo:D_FF + lo + FFN_CHUNK], preferred_element_type=F32)
        a = (_silu(g) * u).astype(BF16)
        t = jnp.dot(a, wo_ref[lo:lo + FFN_CHUNK, :], preferred_element_type=F32)
        acc = t if acc is None else acc + t
    return x + acc


def _post_kernel(x_ref, o_ref, anw_ref, wg_ref, wb_ref, wo_ref, fnw_ref, wi_ref, wfo_ref, fw_ref, out_ref,
                 *, final):
    x1 = _merge_body(x_ref[...], o_ref, anw_ref, wg_ref, wb_ref, wo_ref)
    x2 = _ffn_body(x1, fnw_ref, wi_ref, wfo_ref)
    if final:
        x2 = _rms(x2, fw_ref[...])
    out_ref[...] = x2


def _dense_params():
    return pltpu.CompilerParams(dimension_semantics=("arbitrary",), vmem_limit_bytes=VMEM_LIMIT_BYTES)


def _const_spec(shape):
    nd = len(shape)
    return pl.BlockSpec(shape, lambda i: (0,) * nd)


def _token_tile(t):
    return 512 if t % 512 == 0 else t


GATE_COL0 = 3856
INPROJ_TILE = 1024


def _prep_in_weights(w_in):
    wt = jnp.swapaxes(w_in, 1, 2).astype(BF16)
    return wt, wt[:, GATE_COL0:]


def _inproj(x, nw, wmix, l):
    t = x.shape[0]
    tm = INPROJ_TILE if t % INPROJ_TILE == 0 and t > INPROJ_TILE else _token_tile(t)
    return pl.pallas_call(
        _inproj_kernel,
        grid=(t // tm,),
        in_specs=[pl.BlockSpec((tm, D_MODEL), lambda i: (i, 0)),
                  _const_spec((1, D_MODEL)), pl.BlockSpec((None, GATE_COL0, D_MODEL), lambda i: (l, 0, 0))],
        out_specs=pl.BlockSpec((tm, N_MIX), lambda i: (i, 0)),
        out_shape=jax.ShapeDtypeStruct((t, N_MIX), F32),
        compiler_params=_dense_params(),
        name="inproj",
    )(x, nw, wmix)


def _post(x, o, anw, wg, wb, wo, fnw, wi, wfo, fw, final, l):
    t = x.shape[0]
    tm = _token_tile(t)
    return pl.pallas_call(
        functools.partial(_post_kernel, final=final),
        grid=(t // tm,),
        in_specs=[pl.BlockSpec((tm, D_MODEL), lambda i: (i, 0)),
                  pl.BlockSpec((tm, N_BRANCH * BRANCH_W), lambda i: (i, 0)),
                  _const_spec((1, D_MODEL)), pl.BlockSpec((None, N_GATE, D_MODEL), lambda i: (l, 0, 0)),
                  pl.BlockSpec((None, N_BRANCH, BRANCH_W, D_MODEL), lambda i: (l, 0, 0, 0)),
                  pl.BlockSpec((None, D_MODEL, D_MODEL), lambda i: (l, 0, 0)),
                  _const_spec((1, D_MODEL)), pl.BlockSpec((None, D_MODEL, 2 * D_FF), lambda i: (l, 0, 0)),
                  pl.BlockSpec((None, D_FF, D_MODEL), lambda i: (l, 0, 0)), _const_spec((1, D_MODEL))],
        out_specs=pl.BlockSpec((tm, D_MODEL), lambda i: (i, 0)),
        out_shape=jax.ShapeDtypeStruct((t, D_MODEL), F32),
        compiler_params=_dense_params(),
        name="post",
    )(x, o, anw, wg, wb, wo, fnw, wi, wfo, fw)


def _stack_heads(x, hm):
    if x.shape[0] % BF16_ROWS == 0:
        x = x.astype(BF16)
        return jnp.concatenate([x * hm[h].astype(BF16) for h in range(N_HEAD)], axis=0)
    return jnp.concatenate([x * hm[h] for h in range(N_HEAD)], axis=0)


def _state_in(s, mask):
    return jnp.concatenate([s] * N_HEAD, axis=1) * mask


def _state_out(st):
    return (st[:, 0:HEAD_V] + st[:, HEAD_V:2 * HEAD_V]) + (st[:, 2 * HEAD_V:3 * HEAD_V] + st[:, 3 * HEAD_V:])


def _gla_tile(res, q, k, v, g3, tri, ones_t, ind, hm_k, hm_v, st_ref, mask, c, nseg, seglen):
    tl = q.shape[0]
    nblk = seglen // c
    b = _dotsr(tri, g3)
    yield
    rowi = lax.broadcasted_iota(jnp.int32, (c, q.shape[1]), 0)
    group = 4 * c
    accs, qes = [], []
    for g0 in range(0, tl, group):
        pieces, vrots = [], []
        for r0 in range(g0, g0 + group, c):
            qb, kb, vb = q[r0:r0 + c], k[r0:r0 + c], v[r0:r0 + c]
            bb = b[r0:r0 + c]
            b2 = bb * LOG2E
            qes.append(qb * jnp.exp2(b2))
            pieces.append(qb * kb)
            vr = [vb]
            for d in range(1, c):
                e = jnp.exp2(b2 - pltpu.roll(b2, d, 0))
                pieces.append(jnp.where(rowi >= d, qb * pltpu.roll(kb, d, 0) * e, 0.0))
                vr.append(pltpu.roll(vb, d, 0))
            vrots.append(vr)
        a = _dot1(jnp.concatenate(pieces, axis=0), ind)
        for i, vr in enumerate(vrots):
            acc = a[i * c * c:i * c * c + c] * vr[0]
            for d in range(1, c):
                acc = acc + a[(i * c + d) * c:(i * c + d + 1) * c] * vr[d]
            accs.append(acc)
        yield
    acc = _cat(accs)
    qe = _cat(qes)
    if nblk > 1:
        scores = []
        for seg in range(nseg):
            base = seg * seglen
            for i in range(1, nblk):
                lo = base + i * c
                r = b[lo - 1:lo, :]
                qi = q[lo:lo + c] * jnp.exp(b[lo:lo + c] - r)
                ki = _by_rows(lambda s: k[s] * jnp.exp(r - b[s]), lo, 2 * BF16_ROWS, base)
                scores.append(_dot1(_stack_heads(qi, hm_k), ki, _NT))
                if i % 3 == 0:
                    yield
        parts = []
        n = 0
        for seg in range(nseg):
            base = seg * seglen
            parts.append(jnp.zeros((c, N_HEAD * HEAD_V), F32))
            for i in range(1, nblk):
                r4 = _dot1(scores[n], v[base:base + i * c])
                n += 1
                oi = r4[0:c] * hm_v[0]
                for h in range(1, N_HEAD):
                    oi = oi + r4[h * c:(h + 1) * c] * hm_v[h]
                parts.append(oi)
                if i % 3 == 0:
                    yield
        acc = acc + jnp.concatenate(parts, axis=0)
    outs = []
    for seg in range(nseg):
        lo, hi = seg * seglen, (seg + 1) * seglen
        blast = b[hi - 1:hi, :]
        dcol = jnp.exp(_dots([p[lo:hi] for p in g3], ones_t[lo:hi], _TN))
        st = st_ref[seg]
        outs.append(acc[lo:hi] + _dot1(qe[lo:hi], st))
        ke = _by_rows(lambda s: k[s] * jnp.exp(blast - b[s]), hi, 2 * BF16_ROWS, lo)
        st_ref[seg] = st * dcol + _dot1(ke, v[lo:hi], _TN) * mask
        yield
    res.append(_cat(outs))
    yield


def _ret_tile(res, q, k, v, dm, lg, rtdec, hm, st_ref, mask, nseg, seglen):
    tau = (lax.broadcasted_iota(jnp.int32, q.shape, 0) & (seglen - 1)).astype(F32)
    a = _dot1(q, _stack_heads(k, hm), _NT) * dm
    yield
    intra = _dot1(a, _stack_heads(v, hm))
    qe = q * jnp.exp((tau + 1.0) * lg)
    ke = k * jnp.exp((float(seglen) - 1.0 - tau) * lg)
    yield
    outs = []
    for seg in range(nseg):
        lo, hi = seg * seglen, (seg + 1) * seglen
        st = st_ref[seg]
        outs.append(intra[lo:hi] + _dot1(qe[lo:hi], st))
        st_ref[seg] = st * rtdec + _dot1(ke[lo:hi], v[lo:hi], _TN) * mask
    res.append(_cat(outs))
    yield


def _head_rms_gate(o, ind64, nw, gate):
    ms = _headsum(o * o, ind64) * (1.0 / HEAD_V)
    return o * lax.rsqrt(ms + NORM_EPS) * nw * _silu(gate)


def _rwkv_blocks(kt, ah, kh, rt, ap, kp, v, dec, hmasks, m256, m_scr, c_scr, c, tick):
    nblk = kt.shape[0] // c
    blocks = range(nblk)
    r4 = lax.broadcasted_iota(jnp.int32, (c, N_HEAD * c), 0)
    s4 = lax.broadcasted_iota(jnp.int32, (c, N_HEAD * c), 1) & (c - 1)
    strict = s4 < r4
    incl = s4 <= r4
    rr = lax.broadcasted_iota(jnp.int32, (N_HEAD * c, N_HEAD * c), 0)
    cc = lax.broadcasted_iota(jnp.int32, (N_HEAD * c, N_HEAD * c), 1)
    sh = int(math.log2(c))
    bd4 = ((rr >> sh) == (cc >> sh)).astype(F32)
    eye4 = (rr == cc).astype(F32)
    eye256 = (lax.broadcasted_iota(jnp.int32, (256, 256), 0) == lax.broadcasted_iota(jnp.int32, (256, 256), 1))

    def blk(x, i):
        return x[i * c:(i + 1) * c]

    def stack(x):
        return _stack_heads(x, hmasks)

    def same(size):
        s = int(math.log2(size))
        return (rr >> s) == (cc >> s)

    kr = [jnp.concatenate([blk(kt, i), blk(rt, i)], axis=0).astype(BF16) for i in blocks]
    la = [_dot1(kr[i], stack(blk(ah, i)), _NT) for i in blocks]
    lk = [_dot1(kr[i], stack(blk(kh, i)), _NT) for i in blocks]
    tick()
    l_a = [jnp.where(strict, x[:c], 0.0) for x in la]
    m_a = [jnp.where(incl, x[c:], 0.0) for x in la]
    l_k = [jnp.where(strict, x[:c], 0.0) for x in lk]
    m_k = [jnp.where(incl, x[c:], 0.0) for x in lk]
    lbd = [jnp.concatenate([x] * N_HEAD, axis=0) * bd4 for x in l_a]
    tm = [eye4 - jnp.where(same(2), x, 0.0) for x in lbd]
    size = 2
    while size < c:
        cross = same(2 * size) & jnp.logical_not(same(size))
        xm = [_dot1(tm[i], jnp.where(cross, lbd[i], 0.0)) for i in blocks]
        tick()
        tm = [tm[i] - _dot1(xm[i], tm[i]) for i in blocks]
        tick()
        size *= 2
    ts = []
    for x in tm:
        t = x[0:c]
        for h in range(1, N_HEAD):
            t = t + x[h * c:(h + 1) * c]
        ts.append(t)
    vs = [stack(blk(v, i)).astype(BF16) for i in blocks]
    lkv = [_dot1(l_k[i], vs[i]) for i in blocks]
    mkv = [_dot1(m_k[i], vs[i]) for i in blocks]
    tick()
    kbar = [_dot1(ts[i], stack(blk(kt, i))) for i in blocks]
    u0 = [-_dot1(ts[i], stack(lkv[i])) for i in blocks]
    tick()
    for i in blocks:
        m = jnp.where(eye256, dec[i * c:i * c + 1], 0.0) - _dot1(blk(ap, i), kbar[i], _TN) * m256
        m_scr[i] = m.astype(BF16)
    tick()
    for i in blocks:
        c_scr[i] = _dot1(jnp.concatenate([blk(ap, i), blk(kp, i)], axis=0),
                         jnp.concatenate([u0[i], blk(v, i)], axis=0), _TN) * m256
    return kbar, u0, m_a, mkv


N_MIXER_SHARED_INPUTS = 19


def _mixer_kernel(*refs, c, c_rw, nseg, seglen, nj, has_state, n_alias):
    refs = list(refs)
    x_ref = refs.pop(0)
    if has_state:
        shg_ref, sgl_ref, srw_ref, srt_ref, shift_ref = refs[:5]
        refs = refs[5:]
    (cos_ref, sin_ref, p256_ref, glba_ref, mu_ref, wa2_ref, w2_ref, a2_ref, g2_ref,
     tri_ref, trib_ref, onesb_ref, ind64_ref, indgl_ref, m256_ref, mgl_ref, dm_ref, rtdec_ref,
     ones_ref) = refs[:N_MIXER_SHARED_INPUTS]
    (o_ref, nhg_ref, ngl_ref, nrw_ref, nrt_ref,
     st_hg, st_gl, st_rw, st_rt, carry, m_scr, c_scr) = refs[N_MIXER_SHARED_INPUTS + n_alias:]
    tl = nseg * seglen
    nblk_seg = seglen // c_rw
    j = pl.program_id(1)

    def prm(i):
        return p256_ref[i:i + 1, :]

    tri = tri_ref[...]
    trib = trib_ref[...]
    onesb = onesb_ref[...]
    ones_t = ones_ref[...]
    ind64 = ind64_ref[...]
    indgl = indgl_ref[...]
    m256 = m256_ref[...]
    mgl = mgl_ref[...]
    lane = lax.broadcasted_iota(jnp.int32, (1, 256), 1)
    hmasks = [((lane >> 6) == h).astype(F32) for h in range(N_HEAD)]
    lane_k = lax.broadcasted_iota(jnp.int32, (1, GLA_KW), 1)
    hmasks_glk = [((lane_k >> 5) == h).astype(F32) for h in range(N_HEAD)]

    @pl.when(j == 0)
    def _():
        for seg in range(nseg):
            if has_state:
                st_hg[seg] = _state_in(shg_ref[seg].reshape(256, HEAD_V), m256)
                st_gl[seg] = _state_in(sgl_ref[seg].reshape(GLA_KW, HEAD_V), mgl)
                st_rw[seg] = _state_in(srw_ref[seg].reshape(256, HEAD_V), m256)
                st_rt[seg] = _state_in(srt_ref[seg].reshape(256, HEAD_V), m256)
                carry[seg] = shift_ref[seg]
            else:
                st_hg[seg] = jnp.zeros((256, 256), F32)
                st_gl[seg] = jnp.zeros((GLA_KW, 256), F32)
                st_rw[seg] = jnp.zeros((256, 256), F32)
                st_rt[seg] = jnp.zeros((256, 256), F32)
                carry[seg] = jnp.zeros((1, RW_COLS), F32)

    def cols(c0, w):
        return x_ref.at[:, c0:c0 + w]

    def hgrn_work(res):
        lb = prm(P_LB)
        xq, xf = cols(C_HG_Q, 256), cols(C_HG_F, 256)

        def prep(s):
            forget = lb + (1.0 - lb) * _sigmoid(xf[s])
            return (_silu(xq[s]), 1.0 - forget) + _split3(jnp.log(jnp.maximum(forget, MIN_FORGET)))

        q, k, g1, g2_, g3_ = _by_rows(prep, tl, BF16_ROWS)
        yield
        yield from _gla_tile(res, q, k, cols(C_HG_I, 256), (g1, g2_, g3_), tri, ones_t, ind64,
                             hmasks, hmasks, st_hg, m256, c, nseg, seglen)
        o_ref[:, 0:256] = _head_rms_gate(res.pop(), ind64, prm(P_HG_NW), x_ref[:, C_HG_G:C_HG_G + 256])
        yield

    def gla_work(res):
        za = _dot1(x_ref[:, C_GL_A:C_GL_A + LANES], wa2_ref[...]) + glba_ref[0:1, :]

        def prep(s):
            z = za[s]
            return _split3((jnp.minimum(z, 0.0) - jnp.log1p(jnp.exp(-jnp.abs(z)))) / GLA_TAU)

        g3 = _by_rows(prep, tl, BF16_ROWS)
        yield
        yield from _gla_tile(res, x_ref[:, C_GL_Q:C_GL_Q + GLA_KW] * GLA_DK ** -0.5,
                             x_ref[:, C_GL_K:C_GL_K + GLA_KW], cols(C_GL_V, 256), g3,
                             tri, ones_t, indgl, hmasks_glk, hmasks, st_gl, mgl, c, nseg, seglen)
        o_ref[:, 256:512] = _head_rms_gate(res.pop(), ind64, prm(P_GL_NW), x_ref[:, C_GL_G:C_GL_G + 256])
        yield

    def ret_work(res):
        upper = (lane & (HEAD_V // 2)) != 0
        xq, xk = cols(C_RT_Q, 256), cols(C_RT_K, 256)

        def rot(x, s):
            sw = jnp.where(upper, pltpu.roll(x, HEAD_V // 2, 1), pltpu.roll(x, 256 - HEAD_V // 2, 1))
            return x * cos_ref[s, :] + sw * sin_ref[s, :]

        q_r, k_r = _by_rows(lambda s: (rot(xq[s], s), rot(xk[s], s) * HEAD_V ** -0.5), tl, SUBLANES)
        yield
        yield from _ret_tile(res, q_r, k_r, x_ref[:, C_RT_V:C_RT_V + 256], dm_ref[...], prm(P_RT_LG),
                             rtdec_ref[...], hmasks, st_rt, m256, nseg, seglen)
        ort = res.pop()
        ms = _headsum(ort * ort, ind64) * (1.0 / HEAD_V)
        o_ref[:, 768:1024] = ort * lax.rsqrt(ms + NORM_EPS) * _silu(x_ref[:, C_RT_G:C_RT_G + 256])
        yield

    work = [ret_work([]), hgrn_work([]), gla_work([])]

    def advance(n):
        for _ in range(n):
            while work:
                try:
                    next(work[0])
                    break
                except StopIteration:
                    work.pop(0)

    xrw = cols(C_RW, RW_COLS)
    first = lax.broadcasted_iota(jnp.int32, (SUBLANES, RW_COLS), 0) == 0
    mu = mu_ref[...]

    def mix_rows(s):
        rw = xrw[s]
        if s.start % seglen == 0:
            prev0 = carry[s.start // seglen]
        else:
            prev0 = xrw[s.start - 1:s.start]
        prev = jnp.where(first, jnp.broadcast_to(prev0, (SUBLANES, RW_COLS)), pltpu.roll(rw, 1, 0))
        mix = rw + (prev - rw) * mu
        k0 = pltpu.roll(mix[:, 256:640], 320, 1)[:, 0:256]
        v_ = pltpu.roll(mix[:, 512:896], 320, 1)[:, 0:256]
        kk0 = k0 * prm(P_KK)
        return (mix[:, 0:256], k0, v_, jnp.tanh(mix[:, 256:384]), mix[:, 768:896],
                _sigmoid(mix[:, 896:1024]), kk0, kk0 * kk0)

    r, k0, v, t_w, s_a, s_g, kk0, kk0sq = _by_rows(mix_rows, tl, SUBLANES)
    for seg in range(nseg):
        hi = (seg + 1) * seglen
        carry[seg] = xrw[hi - 1:hi]
    lw_pre = _dot1(t_w, w2_ref[...])
    a_pre = _dot1(s_a, a2_ref[...])
    g = _dot1(s_g, g2_ref[...])
    n2 = _headsum(kk0sq, ind64)

    def gates(s):
        lw = -math.exp(-0.5) * _sigmoid(prm(P_W0) + lw_pre[s])
        a = _sigmoid(prm(P_A0) + a_pre[s])
        kk = kk0[s] / jnp.maximum(jnp.sqrt(n2[s]), 1e-12)
        k = k0[s] * (1.0 + (a - 1.0) * prm(P_KA))
        return (lw, a * kk, kk, k, r[s] * k * prm(P_RK)) + _split3(lw)

    lw, alpha, kk, k, rkp, lw1, lw2, lw3 = _by_rows(gates, tl, BF16_ROWS)
    bw = _dotsr(trib, (lw1, lw2, lw3))
    bc = _dotsr(onesb, (lw1, lw2, lw3))

    def scaled(s):
        bws, bcs = bw[s], bc[s]
        e_neg = jnp.exp(-bws)
        e_rem = jnp.exp(bcs - bws)
        return (kk[s] * jnp.exp(bws - lw[s]), alpha[s] * e_neg, k[s] * e_neg, r[s] * jnp.exp(bws),
                alpha[s] * e_rem, k[s] * e_rem, jnp.exp(bcs))

    rw_kt, rw_ah, rw_kh, rw_rt, rw_ap, rw_kp, rw_dec = _by_rows(scaled, tl, SUBLANES)
    bonus = _headsum(rkp, ind64) * v

    kbar, u0, m_a, mkv = _rwkv_blocks(rw_kt, rw_ah, rw_kh, rw_rt, rw_ap, rw_kp, v, rw_dec,
                                      hmasks, m256, m_scr, c_scr, c_rw, lambda: advance(1))

    per_step = 9
    sts = [st_rw[seg] for seg in range(nseg)]
    outs = [[None] * nblk_seg for _ in range(nseg)]
    for i in range(nblk_seg):
        for seg in range(nseg):
            blk = seg * nblk_seg + i
            sth, stl = _split(sts[seg])
            kr = jnp.concatenate([kbar[blk], rw_rt[blk * c_rw:(blk + 1) * c_rw]], axis=0).astype(BF16)
            mb = m_scr[blk]
            res = _dg(jnp.concatenate([mb, kr], axis=0), sth, _NN)
            xr = res[256:]
            u = u0[blk] - xr[:c_rw]
            outs[seg][i] = xr[c_rw:] + _dot1(m_a[blk], _stack_heads(u, hmasks)) + mkv[blk]
            sts[seg] = (res[:256] + _dg(mb, stl, _NN)) + c_scr[blk]
        advance(per_step)
    while work:
        advance(1)
    for seg in range(nseg):
        st_rw[seg] = sts[seg]
    orw = _cat([o for seg_outs in outs for o in seg_outs])
    mean = _headsum(orw, ind64) * (1.0 / HEAD_V)
    xc = orw - mean
    var = _headsum(xc * xc, ind64) * (1.0 / HEAD_V)
    o_ref[:, 512:768] = (xc * lax.rsqrt(var + RW_LN_EPS) * prm(P_LNW) + prm(P_LNB) + bonus) * g

    @pl.when(j == nj - 1)
    def _():
        for seg in range(nseg):
            nhg_ref[seg] = _state_out(st_hg[seg]).reshape(N_HEAD, HEAD_V, HEAD_V)
            ngl_ref[seg] = _state_out(st_gl[seg]).reshape(N_HEAD, GLA_DK, HEAD_V)
            nrw_ref[seg] = _state_out(st_rw[seg]).reshape(N_HEAD, HEAD_V, HEAD_V)
            nrt_ref[seg] = _state_out(st_rt[seg]).reshape(N_HEAD, HEAD_V, HEAD_V)


def _mixer_constants(nseg, seglen, c_rw, log_gamma):
    tl = nseg * seglen
    t = np.arange(tl)
    same = (t[:, None] // c_rw) == (t[None, :] // c_rw)
    same_seg = (t[:, None] // seglen) == (t[None, :] // seglen)
    causal = (t[None, :] <= t[:, None]) & same_seg
    tri = causal.astype(np.float32)
    trib = (same & causal).astype(np.float32)
    onesb = same.astype(np.float32)
    kv = np.arange(256)
    ind64 = ((kv[:, None] // HEAD_V) == (kv[None, :] // HEAD_V)).astype(np.float32)
    kg = np.arange(GLA_KW)
    indgl = ((kg[:, None] // GLA_DK) == (kv[None, :] // HEAD_V)).astype(np.float32)
    diff = jnp.asarray((t[:, None] - t[None, :]).astype(np.float32))
    dm = jnp.concatenate([jnp.where(jnp.asarray(causal), jnp.exp(diff * log_gamma[h]), 0.0)
                          for h in range(N_HEAD)], axis=1)
    rtdec = jnp.broadcast_to(jnp.exp(float(seglen) * jnp.repeat(log_gamma, HEAD_V))[:, None], (256, 256))
    ones_t = np.ones((tl, 256), np.float32)
    return (jnp.asarray(tri, BF16), jnp.asarray(trib, BF16), jnp.asarray(onesb, BF16),
            jnp.asarray(ind64, BF16), jnp.asarray(indgl, BF16), jnp.asarray(ind64, F32),
            jnp.asarray(indgl, F32), dm, rtdec, jnp.asarray(ones_t, BF16))


def _mixer(proj, states, shift, cosf, sins, lp, c, c_rw, nseg, seglen, read_state):
    ng, rows, _ = proj.shape
    tl = nseg * seglen
    nj = rows // tl
    layer = lp['layer']
    consts = _mixer_constants(nseg, seglen, c_rw, lp['log_gamma'])

    def seq_spec(w):
        return pl.BlockSpec((None, tl, w), lambda g, j: (g, j, 0))

    def const_spec(a):
        nd = a.ndim
        return pl.BlockSpec(a.shape, lambda g, j: (0,) * nd)

    pos_spec = pl.BlockSpec((tl, 256), lambda g, j: (j, 0))
    params = (lp['p256'], lp['glba'], lp['mu'], lp['wa2'], lp['w2'], lp['a2'], lp['g2'])
    state_specs = [pl.BlockSpec((None, nseg) + s.shape[2:], lambda g, j: (layer, g, 0, 0, 0)) for s in states]
    in_specs = [seq_spec(N_MIX)]
    args = [proj]
    if read_state:
        in_specs += state_specs + [pl.BlockSpec((None, nseg, 1, RW_COLS), lambda g, j: (layer, g, 0, 0))]
        args += list(states) + [shift]
        aliases = {1 + i: 1 + i for i in range(len(states))}
    in_specs += [pos_spec, pos_spec] + [const_spec(a) for a in params] + [const_spec(a) for a in consts]
    args += [cosf, sins, *params, *consts]
    if not read_state:
        aliases = {len(args) + i: 1 + i for i in range(len(states))}
        in_specs += [pl.BlockSpec(memory_space=pl.ANY)] * len(states)
        args += list(states)
    out_specs = [seq_spec(N_BRANCH * BRANCH_W)] + state_specs
    out_shape = ([jax.ShapeDtypeStruct((ng, rows, N_BRANCH * BRANCH_W), F32)]
                 + [jax.ShapeDtypeStruct(s.shape, F32) for s in states])
    nblk = tl // c_rw
    scratch = [pltpu.VMEM((nseg, 256, 256), F32), pltpu.VMEM((nseg, GLA_KW, 256), F32),
               pltpu.VMEM((nseg, 256, 256), F32), pltpu.VMEM((nseg, 256, 256), F32),
               pltpu.VMEM((nseg, 1, RW_COLS), F32),
               pltpu.VMEM((nblk, 256, 256), BF16), pltpu.VMEM((nblk, 256, 256), F32)]
    return pl.pallas_call(
        functools.partial(_mixer_kernel, c=c, c_rw=c_rw, nseg=nseg, seglen=seglen, nj=nj,
                          has_state=read_state, n_alias=0 if read_state else len(states)),
        grid=(ng, nj),
        in_specs=in_specs,
        out_specs=out_specs,
        out_shape=out_shape,
        scratch_shapes=scratch,
        input_output_aliases=aliases,
        compiler_params=pltpu.CompilerParams(dimension_semantics=("arbitrary", "arbitrary"),
                                             vmem_limit_bytes=VMEM_LIMIT_BYTES),
        name="mixer",
    )(*args)


def _tile_heads(v):
    return jnp.tile(v, N_HEAD)


def _layer_params(l, lower_bounds, hg_norm_w, gla_wa2, gla_ba, gla_norm_w, rw_mu, rw_w0, rw_w2,
                  rw_a0, rw_a2, rw_g2, rw_kk, rw_ka, rw_rk, rw_ln_w, rw_ln_b):
    log_gamma = jnp.log1p(-jnp.exp2(-5.0 - jnp.arange(N_HEAD, dtype=F32)))
    rows = [lower_bounds[l], _tile_heads(hg_norm_w[l]), _tile_heads(gla_norm_w[l]), rw_w0[l], rw_a0[l],
            rw_kk[l], rw_ka[l], rw_rk[l], rw_ln_w[l], rw_ln_b[l], jnp.repeat(log_gamma, HEAD_V)]
    p256 = jnp.concatenate([jnp.stack(rows), jnp.zeros((16 - len(rows), 256), F32)], axis=0)
    zeros64 = jnp.zeros((64, 256), F32)
    return {
        'log_gamma': log_gamma,
        'p256': p256,
        'glba': jnp.concatenate([gla_ba[l][None, :], jnp.zeros((7, GLA_KW), F32)], axis=0),
        'mu': rw_mu[l][None, :],
        'wa2': jnp.concatenate([gla_wa2[l], jnp.zeros((LANES - GLA_RANK, GLA_KW), F32)], axis=0).astype(BF16),
        'w2': jnp.concatenate([rw_w2[l], zeros64], axis=0).astype(BF16),
        'a2': jnp.concatenate([zeros64, rw_a2[l]], axis=0).astype(BF16),
        'g2': rw_g2[l].astype(BF16),
    }


def _rotary_tables(pos):
    half = HEAD_V // 2
    inv = ROPE_BASE ** (-jnp.arange(half, dtype=F32) / half)
    ang = pos[:, None] * inv[None, :]
    cos, sin = jnp.cos(ang), jnp.sin(ang)
    cosf = jnp.tile(jnp.concatenate([cos, cos], axis=1), (1, N_HEAD))
    sins = jnp.tile(jnp.concatenate([-sin, sin], axis=1), (1, N_HEAD))
    return cosf, sins


def _group_layer(x, states, shift, tables, lp, attn_nw, ffn_nw, final_nw, final, c, c_rw, nseg, seglen,
                 read_state):
    nb, length, _ = x.shape
    xf = x.reshape(nb * length, D_MODEL)
    proj = _inproj(xf, attn_nw, lp['wmix'], lp['layer'])
    o, *outs = _mixer(proj.reshape(nb // nseg, nseg * length, N_MIX), states, shift,
                      tables[0], tables[1], lp, c, c_rw, nseg, seglen, read_state)
    x2 = _post(xf, o.reshape(nb * length, N_BRANCH * BRANCH_W), attn_nw, lp['wg'], lp['wb'], lp['wo'],
               ffn_nw, lp['wi'], lp['wfo'], final_nw, final, lp['layer'])
    new_shift = proj.reshape(nb, length, N_MIX)[:, length - 1, C_RW:C_RW + RW_COLS]
    return x2.reshape(nb, length, D_MODEL), outs, new_shift


def _states_out(outs, shifts):
    n_hg, n_gl, n_rw, n_rt = outs
    return (n_hg, n_gl, n_rw, jnp.stack(shifts), n_rt)


def kernel(x_prompt, x_sample, state_hgrn, state_gla, state_rwkv, state_rwkv_shift, state_ret,
           attn_norm_w, w_in, hg_lb_logits, hg_norm_w, gla_wa2, gla_ba, gla_norm_w,
           rw_mu, rw_w0, rw_w2, rw_a0, rw_a2, rw_g2, rw_kk, rw_ka, rw_rk, rw_ln_w, rw_ln_b,
           w_branch, w_out, ffn_norm_w, w_ffn_in, w_ffn_out, final_norm_w):
    depth = w_in.shape[0]
    lb_p = jax.nn.softmax(hg_lb_logits.astype(F32), axis=0)
    lower_bounds = jnp.cumsum(lb_p, axis=0) - lb_p[0:1]

    bp, lp_len, _ = x_prompt.shape
    bs, ls_len, _ = x_sample.shape
    tab_p = _rotary_tables(jnp.arange(lp_len, dtype=F32))
    tab_s = _rotary_tables(float(PAST_LEN) + jnp.arange(ls_len, dtype=F32))
    tab_s = tuple(jnp.tile(t, (SAMPLE_SEQS_PER_TILE, 1)) for t in tab_s)
    final_nw = final_norm_w[None, :]
    wmix, wg = _prep_in_weights(w_in)
    dense = {'wmix': wmix, 'wg': wg, 'wb': w_branch.astype(BF16), 'wo': w_out.astype(BF16),
             'wi': w_ffn_in.astype(BF16), 'wfo': w_ffn_out.astype(BF16)}

    s_states = [state_hgrn, state_gla, state_rwkv, state_ret]
    p_states = [jnp.zeros((depth, bp) + s.shape[2:], F32) for s in s_states]
    s_shift = state_rwkv_shift.reshape(depth, bs, 1, RW_COLS)
    xp, xs = x_prompt, x_sample
    p_shifts, s_shifts = [], []
    for l in range(depth):
        lp = _layer_params(l, lower_bounds, hg_norm_w, gla_wa2, gla_ba, gla_norm_w, rw_mu, rw_w0,
                           rw_w2, rw_a0, rw_a2, rw_g2, rw_kk, rw_ka, rw_rk, rw_ln_w, rw_ln_b)
        lp.update(dense)
        lp['layer'] = l
        final = l == depth - 1
        anw, fnw = attn_norm_w[l][None, :], ffn_norm_w[l][None, :]
        xp, p_states, sh = _group_layer(xp, p_states, None, tab_p, lp, anw, fnw, final_nw, final, c=8,
                                        c_rw=32, nseg=1, seglen=PROMPT_TILE, read_state=False)
        p_shifts.append(sh)
        xs, s_states, sh = _group_layer(xs, s_states, s_shift, tab_s, lp, anw, fnw, final_nw, final,
                                        c=ls_len, c_rw=ls_len, nseg=SAMPLE_SEQS_PER_TILE, seglen=ls_len,
                                        read_state=True)
        s_shifts.append(sh)

    return (xp, xs) + _states_out(p_states, p_shifts) + _states_out(s_states, s_shifts)
```

```python
import functools
import math

import jax
import jax.numpy as jnp
import numpy as np
from jax import lax
from jax.experimental import pallas as pl
from jax.experimental.pallas import tpu as pltpu

F32 = jnp.float32
BF16 = jnp.bfloat16

D_MODEL = 1024
N_HEAD = 4
HEAD_V = 64
BRANCH_W = N_HEAD * HEAD_V
N_BRANCH = 4
GLA_DK = 32
GLA_KW = N_HEAD * GLA_DK
GLA_RANK = 16
GLA_TAU = 16.0
RW_COLS = 1024
NORM_EPS = 1e-6
RW_LN_EPS = 64e-5
MIN_FORGET = 1e-30
ROPE_BASE = 10000.0
PAST_LEN = 16384
LOG2E = 1.4426950408889634
D_FF = 2816
N_MIX = 3968
PROMPT_TILE = 256
SAMPLE_SEQS_PER_TILE = 8
N_GATE = N_BRANCH * D_MODEL

VMEM_LIMIT_BYTES = 56 * 1024 * 1024
LANES = 128
SUBLANES = 8
BF16_ROWS = 16

C_HG_Q, C_HG_F, C_HG_I, C_HG_G = 0, 256, 512, 768
C_GL_Q, C_GL_K, C_GL_V, C_GL_A, C_GL_G = 1024, 1152, 1280, 1536, 1664
C_RW = 1920
C_RT_Q, C_RT_K, C_RT_V, C_RT_G = 2944, 3200, 3456, 3712

P_LB, P_HG_NW, P_GL_NW, P_W0, P_A0, P_KK, P_KA, P_RK, P_LNW, P_LNB, P_RT_LG = range(11)

_NN = (((1,), (0,)), ((), ()))
_NT = (((1,), (1,)), ((), ()))
_TN = (((0,), (0,)), ((), ()))


def _dg(a, b, dn):
    return lax.dot_general(a, b, dn, preferred_element_type=F32)


def _split(x):
    hi = x.astype(BF16)
    lo = (x - hi.astype(F32)).astype(BF16)
    return hi, lo


def _split3(x):
    x1 = x.astype(BF16)
    r = x - x1.astype(F32)
    x2 = r.astype(BF16)
    x3 = (r - x2.astype(F32)).astype(BF16)
    return x1, x2, x3


def _dot1(a, b, dn=_NN):
    return _dg(a.astype(BF16), b.astype(BF16), dn)


def _dots(pieces, b_exact, dn=_NN):
    out = _dg(pieces[0], b_exact, dn)
    for p in pieces[1:]:
        out = out + _dg(p, b_exact, dn)
    return out


def _dotsr(a_exact, pieces, dn=_NN):
    out = _dg(a_exact, pieces[0], dn)
    for p in pieces[1:]:
        out = out + _dg(a_exact, p, dn)
    return out


def _headsum(x, ind):
    return _dg(x.astype(BF16), ind, _NN)


def _sigmoid(x):
    return jax.nn.sigmoid(x)


def _silu(x):
    return x * jax.nn.sigmoid(x)


def _rms(x, w):
    return x * lax.rsqrt(jnp.mean(x * x, axis=-1, keepdims=True) + NORM_EPS) * w


def _cat(parts, axis=0):
    return parts[0] if len(parts) == 1 else jnp.concatenate(parts, axis=axis)


def _by_rows(fn, rows, chunk, start=0):
    outs = [fn(slice(i, min(i + chunk, rows))) for i in range(start, rows, chunk)]
    if isinstance(outs[0], tuple):
        return tuple(_cat([o[j] for o in outs]) for j in range(len(outs[0])))
    return _cat(outs)


def _inproj_kernel(x_ref, nw_ref, w_ref, o_ref):
    h = _rms(x_ref[...], nw_ref[...]).astype(BF16)
    pad0, pad1 = C_GL_A + GLA_RANK, C_GL_A + LANES
    o_ref[:, 0:pad1] = _dg(h, w_ref[0:pad1, :], _NT)
    o_ref[:, pad1:N_MIX] = _dg(h, w_ref[pad0:GATE_COL0, :], _NT)


def _merge_body(x, o_ref, nw_ref, wg_ref, wb_ref, wo_ref):
    h = _rms(x, nw_ref[...]).astype(BF16)
    obs = [o_ref[:, b * BRANCH_W:(b + 1) * BRANCH_W] for b in range(N_BRANCH)]
    y = None
    for lo in range(0, D_MODEL, DENSE_CHUNK):
        merged = None
        for b in range(N_BRANCH):
            g0 = b * D_MODEL + lo
            gl = _dg(h, wg_ref[g0:g0 + DENSE_CHUNK, :], _NT)
            up = jnp.dot(obs[b], wb_ref[b, :, lo:lo + DENSE_CHUNK], preferred_element_type=F32)
            t = _sigmoid(gl) * up
            merged = t if merged is None else merged + t
        part = jnp.dot(merged.astype(BF16), wo_ref[lo:lo + DENSE_CHUNK, :], preferred_element_type=F32)
        y = part if y is None else y + part
    return x + y


DENSE_CHUNK = 256
FFN_CHUNK = DENSE_CHUNK


def _ffn_body(x, nw_ref, wi_ref, wo_ref):
    h = _rms(x, nw_ref[...]).astype(BF16)
    acc = None
    for j in range(D_FF // FFN_CHUNK):
        lo = j * FFN_CHUNK
        g = jnp.dot(h, wi_ref[:, lo:lo + FFN_CHUNK], preferred_element_type=F32)
        u = jnp.dot(h, wi_ref[:, D_FF + lo:D_FF + lo + FFN_CHUNK], preferred_element_type=F32)
        a = (_silu(g) * u).astype(BF16)
        t = jnp.dot(a, wo_ref[lo:lo + FFN_CHUNK, :], preferred_element_type=F32)
        acc = t if acc is None else acc + t
    return x + acc


def _post_kernel(x_ref, o_ref, anw_ref, wg_ref, wb_ref, wo_ref, fnw_ref, wi_ref, wfo_ref, fw_ref, out_ref,
                 *, final):
    x1 = _merge_body(x_ref[...], o_ref, anw_ref, wg_ref, wb_ref, wo_ref)
    x2 = _ffn_body(x1, fnw_ref, wi_ref, wfo_ref)
    if final:
        x2 = _rms(x2, fw_ref[...])
    out_ref[...] = x2


def _dense_params():
    return pltpu.CompilerParams(dimension_semantics=("arbitrary",), vmem_limit_bytes=VMEM_LIMIT_BYTES)


def _const_spec(shape):
    nd = len(shape)
    return pl.BlockSpec(shape, lambda i: (0,) * nd)


def _token_tile(t):
    return 512 if t % 512 == 0 else t


GATE_COL0 = 3856
INPROJ_TILE = 1024


def _prep_in_weights(w_in):
    wt = jnp.swapaxes(w_in, 1, 2).astype(BF16)
    return wt, wt[:, GATE_COL0:]


def _inproj(x, nw, wmix, l):
    t = x.shape[0]
    tm = INPROJ_TILE if t % INPROJ_TILE == 0 and t > INPROJ_TILE else _token_tile(t)
    return pl.pallas_call(
        _inproj_kernel,
        grid=(t // tm,),
        in_specs=[pl.BlockSpec((tm, D_MODEL), lambda i: (i, 0)),
                  _const_spec((1, D_MODEL)), pl.BlockSpec((None, GATE_COL0, D_MODEL), lambda i: (l, 0, 0))],
        out_specs=pl.BlockSpec((tm, N_MIX), lambda i: (i, 0)),
        out_shape=jax.ShapeDtypeStruct((t, N_MIX), F32),
        compiler_params=_dense_params(),
        name="inproj",
    )(x, nw, wmix)


def _post(x, o, anw, wg, wb, wo, fnw, wi, wfo, fw, final, l):
    t = x.shape[0]
    tm = _token_tile(t)
    return pl.pallas_call(
        functools.partial(_post_kernel, final=final),
        grid=(t // tm,),
        in_specs=[pl.BlockSpec((tm, D_MODEL), lambda i: (i, 0)),
                  pl.BlockSpec((tm, N_BRANCH * BRANCH_W), lambda i: (i, 0)),
                  _const_spec((1, D_MODEL)), pl.BlockSpec((None, N_GATE, D_MODEL), lambda i: (l, 0, 0)),
                  pl.BlockSpec((None, N_BRANCH, BRANCH_W, D_MODEL), lambda i: (l, 0, 0, 0)),
                  pl.BlockSpec((None, D_MODEL, D_MODEL), lambda i: (l, 0, 0)),
                  _const_spec((1, D_MODEL)), pl.BlockSpec((None, D_MODEL, 2 * D_FF), lambda i: (l, 0, 0)),
                  pl.BlockSpec((None, D_FF, D_MODEL), lambda i: (l, 0, 0)), _const_spec((1, D_MODEL))],
        out_specs=pl.BlockSpec((tm, D_MODEL), lambda i: (i, 0)),
        out_shape=jax.ShapeDtypeStruct((t, D_MODEL), F32),
        compiler_params=_dense_params(),
        name="post",
    )(x, o, anw, wg, wb, wo, fnw, wi, wfo, fw)


def _stack_heads(x, hm):
    if x.shape[0] % BF16_ROWS == 0:
        x = x.astype(BF16)
        return jnp.concatenate([x * hm[h].astype(BF16) for h in range(N_HEAD)], axis=0)
    return jnp.concatenate([x * hm[h] for h in range(N_HEAD)], axis=0)


def _state_in(s, mask):
    return jnp.concatenate([s] * N_HEAD, axis=1) * mask


def _state_out(st):
    return (st[:, 0:HEAD_V] + st[:, HEAD_V:2 * HEAD_V]) + (st[:, 2 * HEAD_V:3 * HEAD_V] + st[:, 3 * HEAD_V:])


def _gla_tile(res, q, k, v, g3, tri, ones_t, ind, hm_k, hm_v, st_ref, mask, c, nseg, seglen):
    tl = q.shape[0]
    nblk = seglen // c
    b = _dotsr(tri, g3)
    yield
    rowi = lax.broadcasted_iota(jnp.int32, (c, q.shape[1]), 0)
    group = 4 * c
    accs, qes = [], []
    for g0 in range(0, tl, group):
        pieces, vrots = [], []
        for r0 in range(g0, g0 + group, c):
            qb, kb, vb = q[r0:r0 + c], k[r0:r0 + c], v[r0:r0 + c]
            bb = b[r0:r0 + c]
            b2 = bb * LOG2E
            qes.append(qb * jnp.exp2(b2))
            pieces.append(qb * kb)
            vr = [vb]
            for d in range(1, c):
                e = jnp.exp2(b2 - pltpu.roll(b2, d, 0))
                pieces.append(jnp.where(rowi >= d, qb * pltpu.roll(kb, d, 0) * e, 0.0))
                vr.append(pltpu.roll(vb, d, 0))
            vrots.append(vr)
        a = _dot1(jnp.concatenate(pieces, axis=0), ind)
        for i, vr in enumerate(vrots):
            acc = a[i * c * c:i * c * c + c] * vr[0]
            for d in range(1, c):
                acc = acc + a[(i * c + d) * c:(i * c + d + 1) * c] * vr[d]
            accs.append(acc)
        yield
    acc = _cat(accs)
    qe = _cat(qes)
    if nblk > 1:
        scores = []
        for seg in range(nseg):
            base = seg * seglen
            for i in range(1, nblk):
                lo = base + i * c
                r = b[lo - 1:lo, :]
                qi = q[lo:lo + c] * jnp.exp(b[lo:lo + c] - r)
                ki = _by_rows(lambda s: k[s] * jnp.exp(r - b[s]), lo, 2 * BF16_ROWS, base)
                scores.append(_dot1(_stack_heads(qi, hm_k), ki, _NT))
                if i % 3 == 0:
                    yield
        parts = []
        n = 0
        for seg in range(nseg):
            base = seg * seglen
            parts.append(jnp.zeros((c, N_HEAD * HEAD_V), F32))
            for i in range(1, nblk):
                r4 = _dot1(scores[n], v[base:base + i * c])
                n += 1
                oi = r4[0:c] * hm_v[0]
                for h in range(1, N_HEAD):
                    oi = oi + r4[h * c:(h + 1) * c] * hm_v[h]
                parts.append(oi)
                if i % 3 == 0:
                    yield
        acc = acc + jnp.concatenate(parts, axis=0)
    outs = []
    for seg in range(nseg):
        lo, hi = seg * seglen, (seg + 1) * seglen
        blast = b[hi - 1:hi, :]
        dcol = jnp.exp(_dots([p[lo:hi] for p in g3], ones_t[lo:hi], _TN))
        st = st_ref[seg]
        outs.append(acc[lo:hi] + _dot1(qe[lo:hi], st))
        ke = _by_rows(lambda s: k[s] * jnp.exp(blast - b[s]), hi, 2 * BF16_ROWS, lo)
        st_ref[seg] = st * dcol + _dot1(ke, v[lo:hi], _TN) * mask
        yield
    res.append(_cat(outs))
    yield


def _ret_tile(res, q, k, v, dm, lg, rtdec, hm, st_ref, mask, nseg, seglen):
    tau = (lax.broadcasted_iota(jnp.int32, q.shape, 0) & (seglen - 1)).astype(F32)
    a = _dot1(q, _stack_heads(k, hm), _NT) * dm
    yield
    intra = _dot1(a, _stack_heads(v, hm))
    qe = q * jnp.exp((tau + 1.0) * lg)
    ke = k * jnp.exp((float(seglen) - 1.0 - tau) * lg)
    yield
    outs = []
    for seg in range(nseg):
        lo, hi = seg * seglen, (seg + 1) * seglen
        st = st_ref[seg]
        outs.append(intra[lo:hi] + _dot1(qe[lo:hi], st))
        st_ref[seg] = st * rtdec + _dot1(ke[lo:hi], v[lo:hi], _TN) * mask
    res.append(_cat(outs))
    yield


def _head_rms_gate(o, ind64, nw, gate):
    ms = _headsum(o * o, ind64) * (1.0 / HEAD_V)
    return o * lax.rsqrt(ms + NORM_EPS) * nw * _silu(gate)


def _rwkv_blocks(kt, ah, kh, rt, ap, kp, v, dec, hmasks, m256, m_scr, c_scr, c, tick):
    nblk = kt.shape[0] // c
    blocks = range(nblk)
    r4 = lax.broadcasted_iota(jnp.int32, (c, N_HEAD * c), 0)
    s4 = lax.broadcasted_iota(jnp.int32, (c, N_HEAD * c), 1) & (c - 1)
    strict = s4 < r4
    incl = s4 <= r4
    rr = lax.broadcasted_iota(jnp.int32, (N_HEAD * c, N_HEAD * c), 0)
    cc = lax.broadcasted_iota(jnp.int32, (N_HEAD * c, N_HEAD * c), 1)
    sh = int(math.log2(c))
    bd4 = ((rr >> sh) == (cc >> sh)).astype(F32)
    eye4 = (rr == cc).astype(F32)
    eye256 = (lax.broadcasted_iota(jnp.int32, (256, 256), 0) == lax.broadcasted_iota(jnp.int32, (256, 256), 1))

    def blk(x, i):
        return x[i * c:(i + 1) * c]

    def stack(x):
        return _stack_heads(x, hmasks)

    def same(size):
        s = int(math.log2(size))
        return (rr >> s) == (cc >> s)

    kr = [jnp.concatenate([blk(kt, i), blk(rt, i)], axis=0).astype(BF16) for i in blocks]
    la = [_dot1(kr[i], stack(blk(ah, i)), _NT) for i in blocks]
    lk = [_dot1(kr[i], stack(blk(kh, i)), _NT) for i in blocks]
    tick()
    l_a = [jnp.where(strict, x[:c], 0.0) for x in la]
    m_a = [jnp.where(incl, x[c:], 0.0) for x in la]
    l_k = [jnp.where(strict, x[:c], 0.0) for x in lk]
    m_k = [jnp.where(incl, x[c:], 0.0) for x in lk]
    lbd = [jnp.concatenate([x] * N_HEAD, axis=0) * bd4 for x in l_a]
    tm = [eye4 - jnp.where(same(2), x, 0.0) for x in lbd]
    size = 2
    while size < c:
        cross = same(2 * size) & jnp.logical_not(same(size))
        xm = [_dot1(tm[i], jnp.where(cross, lbd[i], 0.0)) for i in blocks]
        tick()
        tm = [tm[i] - _dot1(xm[i], tm[i]) for i in blocks]
        tick()
        size *= 2
    ts = []
    for x in tm:
        t = x[0:c]
        for h in range(1, N_HEAD):
            t = t + x[h * c:(h + 1) * c]
        ts.append(t)
    vs = [stack(blk(v, i)).astype(BF16) for i in blocks]
    lkv = [_dot1(l_k[i], vs[i]) for i in blocks]
    mkv = [_dot1(m_k[i], vs[i]) for i in blocks]
    tick()
    kbar = [_dot1(ts[i], stack(blk(kt, i))) for i in blocks]
    u0 = [-_dot1(ts[i], stack(lkv[i])) for i in blocks]
    tick()
    for i in blocks:
        m = jnp.where(eye256, dec[i * c:i * c + 1], 0.0) - _dot1(blk(ap, i), kbar[i], _TN) * m256
        m_scr[i] = m.astype(BF16)
    tick()
    for i in blocks:
        c_scr[i] = _dot1(jnp.concatenate([blk(ap, i), blk(kp, i)], axis=0),
                         jnp.concatenate([u0[i], blk(v, i)], axis=0), _TN) * m256
    return kbar, u0, m_a, mkv


N_MIXER_SHARED_INPUTS = 19


def _mixer_kernel(*refs, c, c_rw, nseg, seglen, nj, has_state, n_alias):
    refs = list(refs)
    x_ref = refs.pop(0)
    if has_state:
        shg_ref, sgl_ref, srw_ref, srt_ref, shift_ref = refs[:5]
        refs = refs[5:]
    (cos_ref, sin_ref, p256_ref, glba_ref, mu_ref, wa2_ref, w2_ref, a2_ref, g2_ref,
     tri_ref, trib_ref, onesb_ref, ind64_ref, indgl_ref, m256_ref, mgl_ref, dm_ref, rtdec_ref,
     ones_ref) = refs[:N_MIXER_SHARED_INPUTS]
    (o_ref, nhg_ref, ngl_ref, nrw_ref, nrt_ref,
     st_hg, st_gl, st_rw, st_rt, carry, m_scr, c_scr) = refs[N_MIXER_SHARED_INPUTS + n_alias:]
    tl = nseg * seglen
    nblk_seg = seglen // c_rw
    j = pl.program_id(1)

    def prm(i):
        return p256_ref[i:i + 1, :]

    tri = tri_ref[...]
    trib = trib_ref[...]
    onesb = onesb_ref[...]
    ones_t = ones_ref[...]
    ind64 = ind64_ref[...]
    indgl = indgl_ref[...]
    m256 = m256_ref[...]
    mgl = mgl_ref[...]
    lane = lax.broadcasted_iota(jnp.int32, (1, 256), 1)
    hmasks = [((lane >> 6) == h).astype(F32) for h in range(N_HEAD)]
    lane_k = lax.broadcasted_iota(jnp.int32, (1, GLA_KW), 1)
    hmasks_glk = [((lane_k >> 5) == h).astype(F32) for h in range(N_HEAD)]

    @pl.when(j == 0)
    def _():
        for seg in range(nseg):
            if has_state:
                st_hg[seg] = _state_in(shg_ref[seg].reshape(256, HEAD_V), m256)
                st_gl[seg] = _state_in(sgl_ref[seg].reshape(GLA_KW, HEAD_V), mgl)
                st_rw[seg] = _state_in(srw_ref[seg].reshape(256, HEAD_V), m256)
                st_rt[seg] = _state_in(srt_ref[seg].reshape(256, HEAD_V), m256)
                carry[seg] = shift_ref[seg]
            else:
                st_hg[seg] = jnp.zeros((256, 256), F32)
                st_gl[seg] = jnp.zeros((GLA_KW, 256), F32)
                st_rw[seg] = jnp.zeros((256, 256), F32)
                st_rt[seg] = jnp.zeros((256, 256), F32)
                carry[seg] = jnp.zeros((1, RW_COLS), F32)

    def cols(c0, w):
        return x_ref.at[:, c0:c0 + w]

    def hgrn_work(res):
        lb = prm(P_LB)
        xq, xf = cols(C_HG_Q, 256), cols(C_HG_F, 256)

        def prep(s):
            forget = lb + (1.0 - lb) * _sigmoid(xf[s])
            return (_silu(xq[s]), 1.0 - forget) + _split3(jnp.log(jnp.maximum(forget, MIN_FORGET)))

        q, k, g1, g2_, g3_ = _by_rows(prep, tl, BF16_ROWS)
        yield
        yield from _gla_tile(res, q, k, cols(C_HG_I, 256), (g1, g2_, g3_), tri, ones_t, ind64,
                             hmasks, hmasks, st_hg, m256, c, nseg, seglen)
        o_ref[:, 0:256] = _head_rms_gate(res.pop(), ind64, prm(P_HG_NW),
                                         x_ref[:, C_HG_G:C_HG_G + 256]).astype(o_ref.dtype)
        yield

    def gla_work(res):
        za = _dot1(x_ref[:, C_GL_A:C_GL_A + LANES], wa2_ref[...]) + glba_ref[0:1, :]

        def prep(s):
            z = za[s]
            return _split3((jnp.minimum(z, 0.0) - jnp.log1p(jnp.exp(-jnp.abs(z)))) / GLA_TAU)

        g3 = _by_rows(prep, tl, BF16_ROWS)
        yield
        yield from _gla_tile(res, x_ref[:, C_GL_Q:C_GL_Q + GLA_KW] * GLA_DK ** -0.5,
                             x_ref[:, C_GL_K:C_GL_K + GLA_KW], cols(C_GL_V, 256), g3,
                             tri, ones_t, indgl, hmasks_glk, hmasks, st_gl, mgl, c, nseg, seglen)
        o_ref[:, 256:512] = _head_rms_gate(res.pop(), ind64, prm(P_GL_NW),
                                           x_ref[:, C_GL_G:C_GL_G + 256]).astype(o_ref.dtype)
        yield

    def ret_work(res):
        upper = (lane & (HEAD_V // 2)) != 0
        xq, xk = cols(C_RT_Q, 256), cols(C_RT_K, 256)

        def rot(x, s):
            sw = jnp.where(upper, pltpu.roll(x, HEAD_V // 2, 1), pltpu.roll(x, 256 - HEAD_V // 2, 1))
            return x * cos_ref[s, :] + sw * sin_ref[s, :]

        q_r, k_r = _by_rows(lambda s: (rot(xq[s], s), rot(xk[s], s) * HEAD_V ** -0.5), tl, SUBLANES)
        yield
        yield from _ret_tile(res, q_r, k_r, x_ref[:, C_RT_V:C_RT_V + 256], dm_ref[...], prm(P_RT_LG),
                             rtdec_ref[...], hmasks, st_rt, m256, nseg, seglen)
        ort = res.pop()
        ms = _headsum(ort * ort, ind64) * (1.0 / HEAD_V)
        o_ref[:, 768:1024] = (ort * lax.rsqrt(ms + NORM_EPS)
                              * _silu(x_ref[:, C_RT_G:C_RT_G + 256])).astype(o_ref.dtype)
        yield

    work = [ret_work([]), hgrn_work([]), gla_work([])]

    def advance(n):
        for _ in range(n):
            while work:
                try:
                    next(work[0])
                    break
                except StopIteration:
                    work.pop(0)

    xrw = cols(C_RW, RW_COLS)
    first = lax.broadcasted_iota(jnp.int32, (SUBLANES, RW_COLS), 0) == 0
    mu = mu_ref[...]

    def mix_rows(s):
        rw = xrw[s]
        if s.start % seglen == 0:
            prev0 = carry[s.start // seglen]
        else:
            prev0 = xrw[s.start - 1:s.start]
        prev = jnp.where(first, jnp.broadcast_to(prev0, (SUBLANES, RW_COLS)), pltpu.roll(rw, 1, 0))
        mix = rw + (prev - rw) * mu
        k0 = pltpu.roll(mix[:, 256:640], 320, 1)[:, 0:256]
        v_ = pltpu.roll(mix[:, 512:896], 320, 1)[:, 0:256]
        kk0 = k0 * prm(P_KK)
        return (mix[:, 0:256], k0, v_, jnp.tanh(mix[:, 256:384]), mix[:, 768:896],
                _sigmoid(mix[:, 896:1024]), kk0, kk0 * kk0)

    r, k0, v, t_w, s_a, s_g, kk0, kk0sq = _by_rows(mix_rows, tl, SUBLANES)
    for seg in range(nseg):
        hi = (seg + 1) * seglen
        carry[seg] = xrw[hi - 1:hi]
    lw_pre = _dot1(t_w, w2_ref[...])
    a_pre = _dot1(s_a, a2_ref[...])
    g = _dot1(s_g, g2_ref[...])
    n2 = _headsum(kk0sq, ind64)

    def gates(s):
        lw = -math.exp(-0.5) * _sigmoid(prm(P_W0) + lw_pre[s])
        a = _sigmoid(prm(P_A0) + a_pre[s])
        kk = kk0[s] / jnp.maximum(jnp.sqrt(n2[s]), 1e-12)
        k = k0[s] * (1.0 + (a - 1.0) * prm(P_KA))
        return (lw, a * kk, kk, k, r[s] * k * prm(P_RK)) + _split3(lw)

    lw, alpha, kk, k, rkp, lw1, lw2, lw3 = _by_rows(gates, tl, BF16_ROWS)
    bw = _dotsr(trib, (lw1, lw2, lw3))
    bc = _dotsr(onesb, (lw1, lw2, lw3))

    def scaled(s):
        bws, bcs = bw[s], bc[s]
        e_neg = jnp.exp(-bws)
        e_rem = jnp.exp(bcs - bws)
        return (kk[s] * jnp.exp(bws - lw[s]), alpha[s] * e_neg, k[s] * e_neg, r[s] * jnp.exp(bws),
                alpha[s] * e_rem, k[s] * e_rem, jnp.exp(bcs))

    rw_kt, rw_ah, rw_kh, rw_rt, rw_ap, rw_kp, rw_dec = _by_rows(scaled, tl, SUBLANES)
    bonus = _headsum(rkp, ind64) * v

    kbar, u0, m_a, mkv = _rwkv_blocks(rw_kt, rw_ah, rw_kh, rw_rt, rw_ap, rw_kp, v, rw_dec,
                                      hmasks, m256, m_scr, c_scr, c_rw, lambda: advance(1))

    per_step = 9
    sts = [st_rw[seg] for seg in range(nseg)]
    outs = [[None] * nblk_seg for _ in range(nseg)]
    for i in range(nblk_seg):
        for seg in range(nseg):
            blk = seg * nblk_seg + i
            sth, stl = _split(sts[seg])
            kr = jnp.concatenate([kbar[blk], rw_rt[blk * c_rw:(blk + 1) * c_rw]], axis=0).astype(BF16)
            mb = m_scr[blk]
            res = _dg(jnp.concatenate([mb, kr], axis=0), sth, _NN)
            xr = res[256:]
            u = u0[blk] - xr[:c_rw]
            outs[seg][i] = xr[c_rw:] + _dot1(m_a[blk], _stack_heads(u, hmasks)) + mkv[blk]
            sts[seg] = (res[:256] + _dg(mb, stl, _NN)) + c_scr[blk]
        advance(per_step)
    while work:
        advance(1)
    for seg in range(nseg):
        st_rw[seg] = sts[seg]
    orw = _cat([o for seg_outs in outs for o in seg_outs])
    mean = _headsum(orw, ind64) * (1.0 / HEAD_V)
    xc = orw - mean
    var = _headsum(xc * xc, ind64) * (1.0 / HEAD_V)
    o_ref[:, 512:768] = ((xc * lax.rsqrt(var + RW_LN_EPS) * prm(P_LNW) + prm(P_LNB) + bonus) * g).astype(o_ref.dtype)

    @pl.when(j == nj - 1)
    def _():
        for seg in range(nseg):
            nhg_ref[seg] = _state_out(st_hg[seg]).reshape(N_HEAD, HEAD_V, HEAD_V)
            ngl_ref[seg] = _state_out(st_gl[seg]).reshape(N_HEAD, GLA_DK, HEAD_V)
            nrw_ref[seg] = _state_out(st_rw[seg]).reshape(N_HEAD, HEAD_V, HEAD_V)
            nrt_ref[seg] = _state_out(st_rt[seg]).reshape(N_HEAD, HEAD_V, HEAD_V)


def _mixer_constants(nseg, seglen, c_rw, log_gamma):
    tl = nseg * seglen
    t = np.arange(tl)
    same = (t[:, None] // c_rw) == (t[None, :] // c_rw)
    same_seg = (t[:, None] // seglen) == (t[None, :] // seglen)
    causal = (t[None, :] <= t[:, None]) & same_seg
    tri = causal.astype(np.float32)
    trib = (same & causal).astype(np.float32)
    onesb = same.astype(np.float32)
    kv = np.arange(256)
    ind64 = ((kv[:, None] // HEAD_V) == (kv[None, :] // HEAD_V)).astype(np.float32)
    kg = np.arange(GLA_KW)
    indgl = ((kg[:, None] // GLA_DK) == (kv[None, :] // HEAD_V)).astype(np.float32)
    diff = jnp.asarray((t[:, None] - t[None, :]).astype(np.float32))
    dm = jnp.concatenate([jnp.where(jnp.asarray(causal), jnp.exp(diff * log_gamma[h]), 0.0)
                          for h in range(N_HEAD)], axis=1)
    rtdec = jnp.broadcast_to(jnp.exp(float(seglen) * jnp.repeat(log_gamma, HEAD_V))[:, None], (256, 256))
    ones_t = np.ones((tl, 256), np.float32)
    return (jnp.asarray(tri, BF16), jnp.asarray(trib, BF16), jnp.asarray(onesb, BF16),
            jnp.asarray(ind64, BF16), jnp.asarray(indgl, BF16), jnp.asarray(ind64, F32),
            jnp.asarray(indgl, F32), dm, rtdec, jnp.asarray(ones_t, BF16))


def _mixer(proj, states, shift, cosf, sins, lp, c, c_rw, nseg, seglen, read_state):
    ng, rows, _ = proj.shape
    tl = nseg * seglen
    nj = rows // tl
    layer = lp['layer']
    consts = _mixer_constants(nseg, seglen, c_rw, lp['log_gamma'])

    def seq_spec(w):
        return pl.BlockSpec((None, tl, w), lambda g, j: (g, j, 0))

    def const_spec(a):
        nd = a.ndim
        return pl.BlockSpec(a.shape, lambda g, j: (0,) * nd)

    pos_spec = pl.BlockSpec((tl, 256), lambda g, j: (j, 0))
    params = (lp['p256'], lp['glba'], lp['mu'], lp['wa2'], lp['w2'], lp['a2'], lp['g2'])
    state_specs = [pl.BlockSpec((None, nseg) + s.shape[2:], lambda g, j: (layer, g, 0, 0, 0)) for s in states]
    in_specs = [seq_spec(N_MIX)]
    args = [proj]
    if read_state:
        in_specs += state_specs + [pl.BlockSpec((None, nseg, 1, RW_COLS), lambda g, j: (layer, g, 0, 0))]
        args += list(states) + [shift]
        aliases = {1 + i: 1 + i for i in range(len(states))}
    in_specs += [pos_spec, pos_spec] + [const_spec(a) for a in params] + [const_spec(a) for a in consts]
    args += [cosf, sins, *params, *consts]
    if not read_state:
        aliases = {len(args) + i: 1 + i for i in range(len(states))}
        in_specs += [pl.BlockSpec(memory_space=pl.ANY)] * len(states)
        args += list(states)
    out_specs = [seq_spec(N_BRANCH * BRANCH_W)] + state_specs
    out_shape = ([jax.ShapeDtypeStruct((ng, rows, N_BRANCH * BRANCH_W), BF16)]
                 + [jax.ShapeDtypeStruct(s.shape, F32) for s in states])
    nblk = tl // c_rw
    scratch = [pltpu.VMEM((nseg, 256, 256), F32), pltpu.VMEM((nseg, GLA_KW, 256), F32),
               pltpu.VMEM((nseg, 256, 256), F32), pltpu.VMEM((nseg, 256, 256), F32),
               pltpu.VMEM((nseg, 1, RW_COLS), F32),
               pltpu.VMEM((nblk, 256, 256), BF16), pltpu.VMEM((nblk, 256, 256), F32)]
    return pl.pallas_call(
        functools.partial(_mixer_kernel, c=c, c_rw=c_rw, nseg=nseg, seglen=seglen, nj=nj,
                          has_state=read_state, n_alias=0 if read_state else len(states)),
        grid=(ng, nj),
        in_specs=in_specs,
        out_specs=out_specs,
        out_shape=out_shape,
        scratch_shapes=scratch,
        input_output_aliases=aliases,
        compiler_params=pltpu.CompilerParams(dimension_semantics=("arbitrary", "arbitrary"),
                                             vmem_limit_bytes=VMEM_LIMIT_BYTES),
        name="mixer",
    )(*args)


def _tile_heads(v):
    return jnp.tile(v, N_HEAD)


def _layer_params(l, lower_bounds, hg_norm_w, gla_wa2, gla_ba, gla_norm_w, rw_mu, rw_w0, rw_w2,
                  rw_a0, rw_a2, rw_g2, rw_kk, rw_ka, rw_rk, rw_ln_w, rw_ln_b):
    log_gamma = jnp.log1p(-jnp.exp2(-5.0 - jnp.arange(N_HEAD, dtype=F32)))
    rows = [lower_bounds[l], _tile_heads(hg_norm_w[l]), _tile_heads(gla_norm_w[l]), rw_w0[l], rw_a0[l],
            rw_kk[l], rw_ka[l], rw_rk[l], rw_ln_w[l], rw_ln_b[l], jnp.repeat(log_gamma, HEAD_V)]
    p256 = jnp.concatenate([jnp.stack(rows), jnp.zeros((16 - len(rows), 256), F32)], axis=0)
    zeros64 = jnp.zeros((64, 256), F32)
    return {
        'log_gamma': log_gamma,
        'p256': p256,
        'glba': jnp.concatenate([gla_ba[l][None, :], jnp.zeros((7, GLA_KW), F32)], axis=0),
        'mu': rw_mu[l][None, :],
        'wa2': jnp.concatenate([gla_wa2[l], jnp.zeros((LANES - GLA_RANK, GLA_KW), F32)], axis=0).astype(BF16),
        'w2': jnp.concatenate([rw_w2[l], zeros64], axis=0).astype(BF16),
        'a2': jnp.concatenate([zeros64, rw_a2[l]], axis=0).astype(BF16),
        'g2': rw_g2[l].astype(BF16),
    }


def _rotary_tables(pos):
    half = HEAD_V // 2
    inv = ROPE_BASE ** (-jnp.arange(half, dtype=F32) / half)
    ang = pos[:, None] * inv[None, :]
    cos, sin = jnp.cos(ang), jnp.sin(ang)
    cosf = jnp.tile(jnp.concatenate([cos, cos], axis=1), (1, N_HEAD))
    sins = jnp.tile(jnp.concatenate([-sin, sin], axis=1), (1, N_HEAD))
    return cosf, sins


def _group_layer(x, states, shift, tables, lp, attn_nw, ffn_nw, final_nw, final, c, c_rw, nseg, seglen,
                 read_state):
    nb, length, _ = x.shape
    xf = x.reshape(nb * length, D_MODEL)
    proj = _inproj(xf, attn_nw, lp['wmix'], lp['layer'])
    o, *outs = _mixer(proj.reshape(nb // nseg, nseg * length, N_MIX), states, shift,
                      tables[0], tables[1], lp, c, c_rw, nseg, seglen, read_state)
    x2 = _post(xf, o.reshape(nb * length, N_BRANCH * BRANCH_W), attn_nw, lp['wg'], lp['wb'], lp['wo'],
               ffn_nw, lp['wi'], lp['wfo'], final_nw, final, lp['layer'])
    new_shift = proj.reshape(nb, length, N_MIX)[:, length - 1, C_RW:C_RW + RW_COLS]
    return x2.reshape(nb, length, D_MODEL), outs, new_shift


def _states_out(outs, shifts):
    n_hg, n_gl, n_rw, n_rt = outs
    return (n_hg, n_gl, n_rw, jnp.stack(shifts), n_rt)


def kernel(x_prompt, x_sample, state_hgrn, state_gla, state_rwkv, state_rwkv_shift, state_ret,
           attn_norm_w, w_in, hg_lb_logits, hg_norm_w, gla_wa2, gla_ba, gla_norm_w,
           rw_mu, rw_w0, rw_w2, rw_a0, rw_a2, rw_g2, rw_kk, rw_ka, rw_rk, rw_ln_w, rw_ln_b,
           w_branch, w_out, ffn_norm_w, w_ffn_in, w_ffn_out, final_norm_w):
    depth = w_in.shape[0]
    lb_p = jax.nn.softmax(hg_lb_logits.astype(F32), axis=0)
    lower_bounds = jnp.cumsum(lb_p, axis=0) - lb_p[0:1]

    bp, lp_len, _ = x_prompt.shape
    bs, ls_len, _ = x_sample.shape
    tab_p = _rotary_tables(jnp.arange(lp_len, dtype=F32))
    tab_s = _rotary_tables(float(PAST_LEN) + jnp.arange(ls_len, dtype=F32))
    tab_s = tuple(jnp.tile(t, (SAMPLE_SEQS_PER_TILE, 1)) for t in tab_s)
    final_nw = final_norm_w[None, :]
    wmix, wg = _prep_in_weights(w_in)
    dense = {'wmix': wmix, 'wg': wg, 'wb': w_branch.astype(BF16), 'wo': w_out.astype(BF16),
             'wi': w_ffn_in.astype(BF16), 'wfo': w_ffn_out.astype(BF16)}

    s_states = [state_hgrn, state_gla, state_rwkv, state_ret]
    p_states = [jnp.zeros((depth, bp) + s.shape[2:], F32) for s in s_states]
    s_shift = state_rwkv_shift.reshape(depth, bs, 1, RW_COLS)
    xp, xs = x_prompt, x_sample
    p_shifts, s_shifts = [], []
    for l in range(depth):
        lp = _layer_params(l, lower_bounds, hg_norm_w, gla_wa2, gla_ba, gla_norm_w, rw_mu, rw_w0,
                           rw_w2, rw_a0, rw_a2, rw_g2, rw_kk, rw_ka, rw_rk, rw_ln_w, rw_ln_b)
        lp.update(dense)
        lp['layer'] = l
        final = l == depth - 1
        anw, fnw = attn_norm_w[l][None, :], ffn_norm_w[l][None, :]
        xp, p_states, sh = _group_layer(xp, p_states, None, tab_p, lp, anw, fnw, final_nw, final, c=8,
                                        c_rw=32, nseg=1, seglen=PROMPT_TILE, read_state=False)
        p_shifts.append(sh)
        xs, s_states, sh = _group_layer(xs, s_states, s_shift, tab_s, lp, anw, fnw, final_nw, final,
                                        c=ls_len, c_rw=ls_len, nseg=SAMPLE_SEQS_PER_TILE, seglen=ls_len,
                                        read_state=True)
        s_shifts.append(sh)

    return (xp, xs) + _states_out(p_states, p_shifts) + _states_out(s_states, s_shifts)
```

```python
import functools
import math

import jax
import jax.numpy as jnp
import numpy as np
from jax import lax
from jax.experimental import pallas as pl
from jax.experimental.pallas import tpu as pltpu

F32 = jnp.float32
BF16 = jnp.bfloat16

D_MODEL = 1024
N_HEAD = 4
HEAD_V = 64
BRANCH_W = N_HEAD * HEAD_V
N_BRANCH = 4
GLA_DK = 32
GLA_KW = N_HEAD * GLA_DK
GLA_RANK = 16
GLA_TAU = 16.0
RW_COLS = 1024
NORM_EPS = 1e-6
RW_LN_EPS = 64e-5
MIN_FORGET = 1e-30
ROPE_BASE = 10000.0
PAST_LEN = 16384
LOG2E = 1.4426950408889634
D_FF = 2816
N_MIX = 3968
PROMPT_TILE = 256
SAMPLE_SEQS_PER_TILE = 8
N_GATE = N_BRANCH * D_MODEL

VMEM_LIMIT_BYTES = 56 * 1024 * 1024
LANES = 128
SUBLANES = 8
BF16_ROWS = 16

C_HG_Q, C_HG_F, C_HG_I, C_HG_G = 0, 256, 512, 768
C_GL_Q, C_GL_K, C_GL_V, C_GL_A, C_GL_G = 1024, 1152, 1280, 1536, 1664
C_RW = 1920
C_RT_Q, C_RT_K, C_RT_V, C_RT_G = 2944, 3200, 3456, 3712

P_LB, P_HG_NW, P_GL_NW, P_W0, P_A0, P_KK, P_KA, P_RK, P_LNW, P_LNB, P_RT_LG = range(11)

_NN = (((1,), (0,)), ((), ()))
_NT = (((1,), (1,)), ((), ()))
_TN = (((0,), (0,)), ((), ()))


def _dg(a, b, dn):
    return lax.dot_general(a, b, dn, preferred_element_type=F32)


def _split(x):
    hi = x.astype(BF16)
    lo = (x - hi.astype(F32)).astype(BF16)
    return hi, lo


def _split3(x):
    x1 = x.astype(BF16)
    r = x - x1.astype(F32)
    x2 = r.astype(BF16)
    x3 = (r - x2.astype(F32)).astype(BF16)
    return x1, x2, x3


def _dot1(a, b, dn=_NN):
    return _dg(a.astype(BF16), b.astype(BF16), dn)


def _dots(pieces, b_exact, dn=_NN):
    out = _dg(pieces[0], b_exact, dn)
    for p in pieces[1:]:
        out = out + _dg(p, b_exact, dn)
    return out


def _dotsr(a_exact, pieces, dn=_NN):
    out = _dg(a_exact, pieces[0], dn)
    for p in pieces[1:]:
        out = out + _dg(a_exact, p, dn)
    return out


def _headsum(x, ind):
    return _dg(x.astype(BF16), ind, _NN)


def _sigmoid(x):
    return jax.nn.sigmoid(x)


def _silu(x):
    return x * jax.nn.sigmoid(x)


def _rms(x, w):
    return x * lax.rsqrt(jnp.mean(x * x, axis=-1, keepdims=True) + NORM_EPS) * w


def _cat(parts, axis=0):
    return parts[0] if len(parts) == 1 else jnp.concatenate(parts, axis=axis)


def _by_rows(fn, rows, chunk, start=0):
    outs = [fn(slice(i, min(i + chunk, rows))) for i in range(start, rows, chunk)]
    if isinstance(outs[0], tuple):
        return tuple(_cat([o[j] for o in outs]) for j in range(len(outs[0])))
    return _cat(outs)


def _inproj_kernel(x_ref, nw_ref, w_ref, o_ref):
    h = _rms(x_ref[...], nw_ref[...]).astype(BF16)
    pad0, pad1 = C_GL_A + GLA_RANK, C_GL_A + LANES
    o_ref[:, 0:pad1] = _dg(h, w_ref[0:pad1, :], _NT)
    o_ref[:, pad1:N_MIX] = _dg(h, w_ref[pad0:GATE_COL0, :], _NT)


def _merge_body(x, o_ref, nw_ref, wg_ref, wb_ref, wo_ref):
    h = _rms(x, nw_ref[...]).astype(BF16)
    obs = [o_ref[:, b * BRANCH_W:(b + 1) * BRANCH_W].astype(BF16) for b in range(N_BRANCH)]
    y = None
    for lo in range(0, D_MODEL, DENSE_CHUNK):
        merged = None
        for b in range(N_BRANCH):
            g0 = b * D_MODEL + lo
            gl = _dg(h, wg_ref[g0:g0 + DENSE_CHUNK, :], _NT)
            up = jnp.dot(obs[b], wb_ref[b, :, lo:lo + DENSE_CHUNK].astype(BF16), preferred_element_type=F32)
            t = _sigmoid(gl) * up
            merged = t if merged is None else merged + t
        part = jnp.dot(merged.astype(BF16), wo_ref[lo:lo + DENSE_CHUNK, :].astype(BF16),
                       preferred_element_type=F32)
        y = part if y is None else y + part
    return x + y


DENSE_CHUNK = 256
FFN_CHUNK = DENSE_CHUNK


def _ffn_body(x, nw_ref, wi_ref, wo_ref):
    h = _rms(x, nw_ref[...]).astype(BF16)
    acc = None
    for j in range(D_FF // FFN_CHUNK):
        lo = j * FFN_CHUNK
        g = jnp.dot(h, wi_ref[:, lo:lo + FFN_CHUNK], preferred_element_type=F32)
        u = jnp.dot(h, wi_ref[:, D_FF + lo:D_FF + lo + FFN_CHUNK], preferred_element_type=F32)
        a = (_silu(g) * u).astype(BF16)
        t = jnp.dot(a, wo_ref[lo:lo + FFN_CHUNK, :], preferred_element_type=F32)
        acc = t if acc is None else acc + t
    return x + acc


def _post_kernel(x_ref, o_ref, anw_ref, wg_ref, wb_ref, wo_ref, fnw_ref, wi_ref, wfo_ref, fw_ref, out_ref,
                 *, final):
    x1 = _merge_body(x_ref[...], o_ref, anw_ref, wg_ref, wb_ref, wo_ref)
    x2 = _ffn_body(x1, fnw_ref, wi_ref, wfo_ref)
    if final:
        x2 = _rms(x2, fw_ref[...])
    out_ref[...] = x2


def _dense_params():
    return pltpu.CompilerParams(dimension_semantics=("arbitrary",), vmem_limit_bytes=VMEM_LIMIT_BYTES)


def _const_spec(shape):
    nd = len(shape)
    return pl.BlockSpec(shape, lambda i: (0,) * nd)


def _token_tile(t):
    return 512 if t % 512 == 0 else t


GATE_COL0 = 3856
INPROJ_TILE = 1024


def _prep_in_weights(w_in):
    wt = jnp.swapaxes(w_in, 1, 2).astype(BF16)
    return wt, wt[:, GATE_COL0:]


def _inproj(x, nw, wmix, l):
    t = x.shape[0]
    tm = INPROJ_TILE if t % INPROJ_TILE == 0 and t > INPROJ_TILE else _token_tile(t)
    return pl.pallas_call(
        _inproj_kernel,
        grid=(t // tm,),
        in_specs=[pl.BlockSpec((tm, D_MODEL), lambda i: (i, 0)),
                  _const_spec((1, D_MODEL)), pl.BlockSpec((None, GATE_COL0, D_MODEL), lambda i: (l, 0, 0))],
        out_specs=pl.BlockSpec((tm, N_MIX), lambda i: (i, 0)),
        out_shape=jax.ShapeDtypeStruct((t, N_MIX), F32),
        compiler_params=_dense_params(),
        name="inproj",
    )(x, nw, wmix)


def _post(x, o, anw, wg, wb, wo, fnw, wi, wfo, fw, final, l):
    t = x.shape[0]
    tm = _token_tile(t)
    return pl.pallas_call(
        functools.partial(_post_kernel, final=final),
        grid=(t // tm,),
        in_specs=[pl.BlockSpec((tm, D_MODEL), lambda i: (i, 0)),
                  pl.BlockSpec((tm, N_BRANCH * BRANCH_W), lambda i: (i, 0)),
                  _const_spec((1, D_MODEL)), pl.BlockSpec((None, N_GATE, D_MODEL), lambda i: (l, 0, 0)),
                  pl.BlockSpec((None, N_BRANCH, BRANCH_W, D_MODEL), lambda i: (l, 0, 0, 0)),
                  pl.BlockSpec((None, D_MODEL, D_MODEL), lambda i: (l, 0, 0)),
                  _const_spec((1, D_MODEL)), pl.BlockSpec((None, D_MODEL, 2 * D_FF), lambda i: (l, 0, 0)),
                  pl.BlockSpec((None, D_FF, D_MODEL), lambda i: (l, 0, 0)), _const_spec((1, D_MODEL))],
        out_specs=pl.BlockSpec((tm, D_MODEL), lambda i: (i, 0)),
        out_shape=jax.ShapeDtypeStruct((t, D_MODEL), F32),
        compiler_params=_dense_params(),
        name="post",
    )(x, o, anw, wg, wb, wo, fnw, wi, wfo, fw)


def _stack_heads(x, hm):
    if x.shape[0] % BF16_ROWS == 0:
        x = x.astype(BF16)
        return jnp.concatenate([x * hm[h].astype(BF16) for h in range(N_HEAD)], axis=0)
    return jnp.concatenate([x * hm[h] for h in range(N_HEAD)], axis=0)


def _state_in(s, mask):
    return jnp.concatenate([s] * N_HEAD, axis=1) * mask


def _state_out(st):
    return (st[:, 0:HEAD_V] + st[:, HEAD_V:2 * HEAD_V]) + (st[:, 2 * HEAD_V:3 * HEAD_V] + st[:, 3 * HEAD_V:])


def _gla_tile(res, q, k, v, g3, tri, ones_t, ind, hm_k, hm_v, st_ref, mask, c, nseg, seglen):
    tl = q.shape[0]
    nblk = seglen // c
    b = _dotsr(tri, g3)
    yield
    rowi = lax.broadcasted_iota(jnp.int32, (c, q.shape[1]), 0)
    group = 4 * c
    accs, qes = [], []
    for g0 in range(0, tl, group):
        pieces, vrots = [], []
        for r0 in range(g0, g0 + group, c):
            qb, kb, vb = q[r0:r0 + c], k[r0:r0 + c], v[r0:r0 + c]
            bb = b[r0:r0 + c]
            b2 = bb * LOG2E
            qes.append(qb * jnp.exp2(b2))
            pieces.append(qb * kb)
            vr = [vb]
            for d in range(1, c):
                e = jnp.exp2(b2 - pltpu.roll(b2, d, 0))
                pieces.append(jnp.where(rowi >= d, qb * pltpu.roll(kb, d, 0) * e, 0.0))
                vr.append(pltpu.roll(vb, d, 0))
            vrots.append(vr)
        a = _dot1(jnp.concatenate(pieces, axis=0), ind)
        for i, vr in enumerate(vrots):
            acc = a[i * c * c:i * c * c + c] * vr[0]
            for d in range(1, c):
                acc = acc + a[(i * c + d) * c:(i * c + d + 1) * c] * vr[d]
            accs.append(acc)
        yield
    acc = _cat(accs)
    qe = _cat(qes)
    if nblk > 1:
        scores = []
        for seg in range(nseg):
            base = seg * seglen
            for i in range(1, nblk):
                lo = base + i * c
                r = b[lo - 1:lo, :]
                qi = q[lo:lo + c] * jnp.exp(b[lo:lo + c] - r)
                ki = _by_rows(lambda s: k[s] * jnp.exp(r - b[s]), lo, 2 * BF16_ROWS, base)
                scores.append(_dot1(_stack_heads(qi, hm_k), ki, _NT))
                if i % 3 == 0:
                    yield
        parts = []
        n = 0
        for seg in range(nseg):
            base = seg * seglen
            parts.append(jnp.zeros((c, N_HEAD * HEAD_V), F32))
            for i in range(1, nblk):
                r4 = _dot1(scores[n], v[base:base + i * c])
                n += 1
                oi = r4[0:c] * hm_v[0]
                for h in range(1, N_HEAD):
                    oi = oi + r4[h * c:(h + 1) * c] * hm_v[h]
                parts.append(oi)
                if i % 3 == 0:
                    yield
        acc = acc + jnp.concatenate(parts, axis=0)
    outs = []
    for seg in range(nseg):
        lo, hi = seg * seglen, (seg + 1) * seglen
        blast = b[hi - 1:hi, :]
        dcol = jnp.exp(_dots([p[lo:hi] for p in g3], ones_t[lo:hi], _TN))
        st = st_ref[seg]
        outs.append(acc[lo:hi] + _dot1(qe[lo:hi], st))
        ke = _by_rows(lambda s: k[s] * jnp.exp(blast - b[s]), hi, 2 * BF16_ROWS, lo)
        st_ref[seg] = st * dcol + _dot1(ke, v[lo:hi], _TN) * mask
        yield
    res.append(_cat(outs))
    yield


def _ret_tile(res, q, k, v, dm, lg, rtdec, hm, st_ref, mask, nseg, seglen):
    tau = (lax.broadcasted_iota(jnp.int32, q.shape, 0) & (seglen - 1)).astype(F32)
    a = _dot1(q, _stack_heads(k, hm), _NT) * dm
    yield
    intra = _dot1(a, _stack_heads(v, hm))
    qe = q * jnp.exp((tau + 1.0) * lg)
    ke = k * jnp.exp((float(seglen) - 1.0 - tau) * lg)
    yield
    outs = []
    for seg in range(nseg):
        lo, hi = seg * seglen, (seg + 1) * seglen
        st = st_ref[seg]
        outs.append(intra[lo:hi] + _dot1(qe[lo:hi], st))
        st_ref[seg] = st * rtdec + _dot1(ke[lo:hi], v[lo:hi], _TN) * mask
    res.append(_cat(outs))
    yield


def _head_rms_gate(o, ind64, nw, gate):
    ms = _headsum(o * o, ind64) * (1.0 / HEAD_V)
    return o * lax.rsqrt(ms + NORM_EPS) * nw * _silu(gate)


def _rwkv_blocks(kt, ah, kh, rt, ap, kp, v, dec, hmasks, m256, m_scr, c_scr, c, tick):
    nblk = kt.shape[0] // c
    blocks = range(nblk)
    r4 = lax.broadcasted_iota(jnp.int32, (c, N_HEAD * c), 0)
    s4 = lax.broadcasted_iota(jnp.int32, (c, N_HEAD * c), 1) & (c - 1)
    strict = s4 < r4
    incl = s4 <= r4
    rr = lax.broadcasted_iota(jnp.int32, (N_HEAD * c, N_HEAD * c), 0)
    cc = lax.broadcasted_iota(jnp.int32, (N_HEAD * c, N_HEAD * c), 1)
    sh = int(math.log2(c))
    bd4 = ((rr >> sh) == (cc >> sh)).astype(F32)
    eye4 = (rr == cc).astype(F32)
    eye256 = (lax.broadcasted_iota(jnp.int32, (256, 256), 0) == lax.broadcasted_iota(jnp.int32, (256, 256), 1))

    def blk(x, i):
        return x[i * c:(i + 1) * c]

    def stack(x):
        return _stack_heads(x, hmasks)

    def same(size):
        s = int(math.log2(size))
        return (rr >> s) == (cc >> s)

    kr = [jnp.concatenate([blk(kt, i), blk(rt, i)], axis=0).astype(BF16) for i in blocks]
    la = [_dot1(kr[i], stack(blk(ah, i)), _NT) for i in blocks]
    lk = [_dot1(kr[i], stack(blk(kh, i)), _NT) for i in blocks]
    tick()
    l_a = [jnp.where(strict, x[:c], 0.0) for x in la]
    m_a = [jnp.where(incl, x[c:], 0.0) for x in la]
    l_k = [jnp.where(strict, x[:c], 0.0) for x in lk]
    m_k = [jnp.where(incl, x[c:], 0.0) for x in lk]
    lbd = [jnp.concatenate([x] * N_HEAD, axis=0) * bd4 for x in l_a]
    tm = [eye4 - jnp.where(same(2), x, 0.0) for x in lbd]
    size = 2
    while size < c:
        cross = same(2 * size) & jnp.logical_not(same(size))
        xm = [_dot1(tm[i], jnp.where(cross, lbd[i], 0.0)) for i in blocks]
        tick()
        tm = [tm[i] - _dot1(xm[i], tm[i]) for i in blocks]
        tick()
        size *= 2
    ts = []
    for x in tm:
        t = x[0:c]
        for h in range(1, N_HEAD):
            t = t + x[h * c:(h + 1) * c]
        ts.append(t)
    vs = [stack(blk(v, i)).astype(BF16) for i in blocks]
    lkv = [_dot1(l_k[i], vs[i]) for i in blocks]
    mkv = [_dot1(m_k[i], vs[i]) for i in blocks]
    tick()
    kbar = [_dot1(ts[i], stack(blk(kt, i))) for i in blocks]
    u0 = [-_dot1(ts[i], stack(lkv[i])) for i in blocks]
    tick()
    for i in blocks:
        m = jnp.where(eye256, dec[i * c:i * c + 1], 0.0) - _dot1(blk(ap, i), kbar[i], _TN) * m256
        m_scr[i] = m.astype(BF16)
    tick()
    for i in blocks:
        c_scr[i] = _dot1(jnp.concatenate([blk(ap, i), blk(kp, i)], axis=0),
                         jnp.concatenate([u0[i], blk(v, i)], axis=0), _TN) * m256
    return kbar, u0, m_a, mkv


N_MIXER_SHARED_INPUTS = 19


def _mixer_kernel(*refs, c, c_rw, nseg, seglen, nj, has_state, n_alias):
    refs = list(refs)
    x_ref = refs.pop(0)
    if has_state:
        shg_ref, sgl_ref, srw_ref, srt_ref, shift_ref = refs[:5]
        refs = refs[5:]
    (cos_ref, sin_ref, p256_ref, glba_ref, mu_ref, wa2_ref, w2_ref, a2_ref, g2_ref,
     tri_ref, trib_ref, onesb_ref, ind64_ref, indgl_ref, m256_ref, mgl_ref, dm_ref, rtdec_ref,
     ones_ref) = refs[:N_MIXER_SHARED_INPUTS]
    (o_ref, nhg_ref, ngl_ref, nrw_ref, nrt_ref,
     st_hg, st_gl, st_rw, st_rt, carry, m_scr, c_scr) = refs[N_MIXER_SHARED_INPUTS + n_alias:]
    tl = nseg * seglen
    nblk_seg = seglen // c_rw
    j = pl.program_id(1)

    def prm(i):
        return p256_ref[i:i + 1, :]

    tri = tri_ref[...]
    trib = trib_ref[...]
    onesb = onesb_ref[...]
    ones_t = ones_ref[...]
    ind64 = ind64_ref[...]
    indgl = indgl_ref[...]
    m256 = m256_ref[...]
    mgl = mgl_ref[...]
    lane = lax.broadcasted_iota(jnp.int32, (1, 256), 1)
    hmasks = [((lane >> 6) == h).astype(F32) for h in range(N_HEAD)]
    lane_k = lax.broadcasted_iota(jnp.int32, (1, GLA_KW), 1)
    hmasks_glk = [((lane_k >> 5) == h).astype(F32) for h in range(N_HEAD)]

    @pl.when(j == 0)
    def _():
        for seg in range(nseg):
            if has_state:
                st_hg[seg] = _state_in(shg_ref[seg].reshape(256, HEAD_V), m256)
                st_gl[seg] = _state_in(sgl_ref[seg].reshape(GLA_KW, HEAD_V), mgl)
                st_rw[seg] = _state_in(srw_ref[seg].reshape(256, HEAD_V), m256)
                st_rt[seg] = _state_in(srt_ref[seg].reshape(256, HEAD_V), m256)
                carry[seg] = shift_ref[seg]
            else:
                st_hg[seg] = jnp.zeros((256, 256), F32)
                st_gl[seg] = jnp.zeros((GLA_KW, 256), F32)
                st_rw[seg] = jnp.zeros((256, 256), F32)
                st_rt[seg] = jnp.zeros((256, 256), F32)
                carry[seg] = jnp.zeros((1, RW_COLS), F32)

    def cols(c0, w):
        return x_ref.at[:, c0:c0 + w]

    def hgrn_work(res):
        lb = prm(P_LB)
        xq, xf = cols(C_HG_Q, 256), cols(C_HG_F, 256)

        def prep(s):
            forget = lb + (1.0 - lb) * _sigmoid(xf[s])
            return (_silu(xq[s]), 1.0 - forget) + _split3(jnp.log(jnp.maximum(forget, MIN_FORGET)))

        q, k, g1, g2_, g3_ = _by_rows(prep, tl, BF16_ROWS)
        yield
        yield from _gla_tile(res, q, k, cols(C_HG_I, 256), (g1, g2_, g3_), tri, ones_t, ind64,
                             hmasks, hmasks, st_hg, m256, c, nseg, seglen)
        o_ref[:, 0:256] = _head_rms_gate(res.pop(), ind64, prm(P_HG_NW), x_ref[:, C_HG_G:C_HG_G + 256])
        yield

    def gla_work(res):
        za = _dot1(x_ref[:, C_GL_A:C_GL_A + LANES], wa2_ref[...]) + glba_ref[0:1, :]

        def prep(s):
            z = za[s]
            return _split3((jnp.minimum(z, 0.0) - jnp.log1p(jnp.exp(-jnp.abs(z)))) / GLA_TAU)

        g3 = _by_rows(prep, tl, BF16_ROWS)
        yield
        yield from _gla_tile(res, x_ref[:, C_GL_Q:C_GL_Q + GLA_KW] * GLA_DK ** -0.5,
                             x_ref[:, C_GL_K:C_GL_K + GLA_KW], cols(C_GL_V, 256), g3,
                             tri, ones_t, indgl, hmasks_glk, hmasks, st_gl, mgl, c, nseg, seglen)
        o_ref[:, 256:512] = _head_rms_gate(res.pop(), ind64, prm(P_GL_NW), x_ref[:, C_GL_G:C_GL_G + 256])
        yield

    def ret_work(res):
        upper = (lane & (HEAD_V // 2)) != 0
        xq, xk = cols(C_RT_Q, 256), cols(C_RT_K, 256)

        def rot(x, s):
            sw = jnp.where(upper, pltpu.roll(x, HEAD_V // 2, 1), pltpu.roll(x, 256 - HEAD_V // 2, 1))
            return x * cos_ref[s, :] + sw * sin_ref[s, :]

        q_r, k_r = _by_rows(lambda s: (rot(xq[s], s), rot(xk[s], s) * HEAD_V ** -0.5), tl, SUBLANES)
        yield
        yield from _ret_tile(res, q_r, k_r, x_ref[:, C_RT_V:C_RT_V + 256], dm_ref[...], prm(P_RT_LG),
                             rtdec_ref[...], hmasks, st_rt, m256, nseg, seglen)
        ort = res.pop()
        ms = _headsum(ort * ort, ind64) * (1.0 / HEAD_V)
        o_ref[:, 768:1024] = ort * lax.rsqrt(ms + NORM_EPS) * _silu(x_ref[:, C_RT_G:C_RT_G + 256])
        yield

    work = [ret_work([]), hgrn_work([]), gla_work([])]

    def advance(n):
        for _ in range(n):
            while work:
                try:
                    next(work[0])
                    break
                except StopIteration:
                    work.pop(0)

    xrw = cols(C_RW, RW_COLS)
    first = lax.broadcasted_iota(jnp.int32, (SUBLANES, RW_COLS), 0) == 0
    mu = mu_ref[...]

    def mix_rows(s):
        rw = xrw[s]
        if s.start % seglen == 0:
            prev0 = carry[s.start // seglen]
        else:
            prev0 = xrw[s.start - 1:s.start]
        prev = jnp.where(first, jnp.broadcast_to(prev0, (SUBLANES, RW_COLS)), pltpu.roll(rw, 1, 0))
        mix = rw + (prev - rw) * mu
        k0 = pltpu.roll(mix[:, 256:640], 320, 1)[:, 0:256]
        v_ = pltpu.roll(mix[:, 512:896], 320, 1)[:, 0:256]
        kk0 = k0 * prm(P_KK)
        return (mix[:, 0:256], k0, v_, jnp.tanh(mix[:, 256:384]), mix[:, 768:896],
                _sigmoid(mix[:, 896:1024]), kk0, kk0 * kk0)

    r, k0, v, t_w, s_a, s_g, kk0, kk0sq = _by_rows(mix_rows, tl, SUBLANES)
    for seg in range(nseg):
        hi = (seg + 1) * seglen
        carry[seg] = xrw[hi - 1:hi]
    lw_pre = _dot1(t_w, w2_ref[...])
    a_pre = _dot1(s_a, a2_ref[...])
    g = _dot1(s_g, g2_ref[...])
    n2 = _headsum(kk0sq, ind64)

    def gates(s):
        lw = -math.exp(-0.5) * _sigmoid(prm(P_W0) + lw_pre[s])
        a = _sigmoid(prm(P_A0) + a_pre[s])
        kk = kk0[s] / jnp.maximum(jnp.sqrt(n2[s]), 1e-12)
        k = k0[s] * (1.0 + (a - 1.0) * prm(P_KA))
        return (lw, a * kk, kk, k, r[s] * k * prm(P_RK)) + _split3(lw)

    lw, alpha, kk, k, rkp, lw1, lw2, lw3 = _by_rows(gates, tl, BF16_ROWS)
    bw = _dotsr(trib, (lw1, lw2, lw3))
    bc = _dotsr(onesb, (lw1, lw2, lw3))

    def scaled(s):
        bws, bcs = bw[s], bc[s]
        e_neg = jnp.exp(-bws)
        e_rem = jnp.exp(bcs - bws)
        return (kk[s] * jnp.exp(bws - lw[s]), alpha[s] * e_neg, k[s] * e_neg, r[s] * jnp.exp(bws),
                alpha[s] * e_rem, k[s] * e_rem, jnp.exp(bcs))

    rw_kt, rw_ah, rw_kh, rw_rt, rw_ap, rw_kp, rw_dec = _by_rows(scaled, tl, SUBLANES)
    bonus = _headsum(rkp, ind64) * v

    kbar, u0, m_a, mkv = _rwkv_blocks(rw_kt, rw_ah, rw_kh, rw_rt, rw_ap, rw_kp, v, rw_dec,
                                      hmasks, m256, m_scr, c_scr, c_rw, lambda: advance(1))

    per_step = 9
    sts = [st_rw[seg] for seg in range(nseg)]
    outs = [[None] * nblk_seg for _ in range(nseg)]
    for i in range(nblk_seg):
        for seg in range(nseg):
            blk = seg * nblk_seg + i
            sth, stl = _split(sts[seg])
            kr = jnp.concatenate([kbar[blk], rw_rt[blk * c_rw:(blk + 1) * c_rw]], axis=0).astype(BF16)
            mb = m_scr[blk]
            res = _dg(jnp.concatenate([mb, kr], axis=0), sth, _NN)
            xr = res[256:]
            u = u0[blk] - xr[:c_rw]
            outs[seg][i] = xr[c_rw:] + _dot1(m_a[blk], _stack_heads(u, hmasks)) + mkv[blk]
            sts[seg] = (res[:256] + _dg(mb, stl, _NN)) + c_scr[blk]
        advance(per_step)
    while work:
        advance(1)
    for seg in range(nseg):
        st_rw[seg] = sts[seg]
    orw = _cat([o for seg_outs in outs for o in seg_outs])
    mean = _headsum(orw, ind64) * (1.0 / HEAD_V)
    xc = orw - mean
    var = _headsum(xc * xc, ind64) * (1.0 / HEAD_V)
    o_ref[:, 512:768] = (xc * lax.rsqrt(var + RW_LN_EPS) * prm(P_LNW) + prm(P_LNB) + bonus) * g

    @pl.when(j == nj - 1)
    def _():
        for seg in range(nseg):
            nhg_ref[seg] = _state_out(st_hg[seg]).reshape(N_HEAD, HEAD_V, HEAD_V)
            ngl_ref[seg] = _state_out(st_gl[seg]).reshape(N_HEAD, GLA_DK, HEAD_V)
            nrw_ref[seg] = _state_out(st_rw[seg]).reshape(N_HEAD, HEAD_V, HEAD_V)
            nrt_ref[seg] = _state_out(st_rt[seg]).reshape(N_HEAD, HEAD_V, HEAD_V)


def _mixer_constants(nseg, seglen, c_rw, log_gamma):
    tl = nseg * seglen
    t = np.arange(tl)
    same = (t[:, None] // c_rw) == (t[None, :] // c_rw)
    same_seg = (t[:, None] // seglen) == (t[None, :] // seglen)
    causal = (t[None, :] <= t[:, None]) & same_seg
    tri = causal.astype(np.float32)
    trib = (same & causal).astype(np.float32)
    onesb = same.astype(np.float32)
    kv = np.arange(256)
    ind64 = ((kv[:, None] // HEAD_V) == (kv[None, :] // HEAD_V)).astype(np.float32)
    kg = np.arange(GLA_KW)
    indgl = ((kg[:, None] // GLA_DK) == (kv[None, :] // HEAD_V)).astype(np.float32)
    diff = jnp.asarray((t[:, None] - t[None, :]).astype(np.float32))
    dm = jnp.concatenate([jnp.where(jnp.asarray(causal), jnp.exp(diff * log_gamma[h]), 0.0)
                          for h in range(N_HEAD)], axis=1)
    rtdec = jnp.broadcast_to(jnp.exp(float(seglen) * jnp.repeat(log_gamma, HEAD_V))[:, None], (256, 256))
    ones_t = np.ones((tl, 256), np.float32)
    return (jnp.asarray(tri, BF16), jnp.asarray(trib, BF16), jnp.asarray(onesb, BF16),
            jnp.asarray(ind64, BF16), jnp.asarray(indgl, BF16), jnp.asarray(ind64, F32),
            jnp.asarray(indgl, F32), dm, rtdec, jnp.asarray(ones_t, BF16))


def _mixer(proj, states, shift, cosf, sins, lp, c, c_rw, nseg, seglen, read_state):
    ng, rows, _ = proj.shape
    tl = nseg * seglen
    nj = rows // tl
    layer = lp['layer']
    consts = _mixer_constants(nseg, seglen, c_rw, lp['log_gamma'])

    def seq_spec(w):
        return pl.BlockSpec((None, tl, w), lambda g, j: (g, j, 0))

    def const_spec(a):
        nd = a.ndim
        return pl.BlockSpec(a.shape, lambda g, j: (0,) * nd)

    pos_spec = pl.BlockSpec((tl, 256), lambda g, j: (j, 0))
    params = (lp['p256'], lp['glba'], lp['mu'], lp['wa2'], lp['w2'], lp['a2'], lp['g2'])
    state_specs = [pl.BlockSpec((None, nseg) + s.shape[2:], lambda g, j: (layer, g, 0, 0, 0)) for s in states]
    in_specs = [seq_spec(N_MIX)]
    args = [proj]
    if read_state:
        in_specs += state_specs + [pl.BlockSpec((None, nseg, 1, RW_COLS), lambda g, j: (layer, g, 0, 0))]
        args += list(states) + [shift]
        aliases = {1 + i: 1 + i for i in range(len(states))}
    in_specs += [pos_spec, pos_spec] + [const_spec(a) for a in params] + [const_spec(a) for a in consts]
    args += [cosf, sins, *params, *consts]
    if not read_state:
        aliases = {len(args) + i: 1 + i for i in range(len(states))}
        in_specs += [pl.BlockSpec(memory_space=pl.ANY)] * len(states)
        args += list(states)
    out_specs = [seq_spec(N_BRANCH * BRANCH_W)] + state_specs
    out_shape = ([jax.ShapeDtypeStruct((ng, rows, N_BRANCH * BRANCH_W), F32)]
                 + [jax.ShapeDtypeStruct(s.shape, F32) for s in states])
    nblk = tl // c_rw
    scratch = [pltpu.VMEM((nseg, 256, 256), F32), pltpu.VMEM((nseg, GLA_KW, 256), F32),
               pltpu.VMEM((nseg, 256, 256), F32), pltpu.VMEM((nseg, 256, 256), F32),
               pltpu.VMEM((nseg, 1, RW_COLS), F32),
               pltpu.VMEM((nblk, 256, 256), BF16), pltpu.VMEM((nblk, 256, 256), F32)]
    return pl.pallas_call(
        functools.partial(_mixer_kernel, c=c, c_rw=c_rw, nseg=nseg, seglen=seglen, nj=nj,
                          has_state=read_state, n_alias=0 if read_state else len(states)),
        grid=(ng, nj),
        in_specs=in_specs,
        out_specs=out_specs,
        out_shape=out_shape,
        scratch_shapes=scratch,
        input_output_aliases=aliases,
        compiler_params=pltpu.CompilerParams(dimension_semantics=("arbitrary", "arbitrary"),
                                             vmem_limit_bytes=VMEM_LIMIT_BYTES),
        name="mixer",
    )(*args)


def _tile_heads(v):
    return jnp.tile(v, N_HEAD)


def _layer_params(l, lower_bounds, hg_norm_w, gla_wa2, gla_ba, gla_norm_w, rw_mu, rw_w0, rw_w2,
                  rw_a0, rw_a2, rw_g2, rw_kk, rw_ka, rw_rk, rw_ln_w, rw_ln_b):
    log_gamma = jnp.log1p(-jnp.exp2(-5.0 - jnp.arange(N_HEAD, dtype=F32)))
    rows = [lower_bounds[l], _tile_heads(hg_norm_w[l]), _tile_heads(gla_norm_w[l]), rw_w0[l], rw_a0[l],
            rw_kk[l], rw_ka[l], rw_rk[l], rw_ln_w[l], rw_ln_b[l], jnp.repeat(log_gamma, HEAD_V)]
    p256 = jnp.concatenate([jnp.stack(rows), jnp.zeros((16 - len(rows), 256), F32)], axis=0)
    zeros64 = jnp.zeros((64, 256), F32)
    return {
        'log_gamma': log_gamma,
        'p256': p256,
        'glba': jnp.concatenate([gla_ba[l][None, :], jnp.zeros((7, GLA_KW), F32)], axis=0),
        'mu': rw_mu[l][None, :],
        'wa2': jnp.concatenate([gla_wa2[l], jnp.zeros((LANES - GLA_RANK, GLA_KW), F32)], axis=0).astype(BF16),
        'w2': jnp.concatenate([rw_w2[l], zeros64], axis=0).astype(BF16),
        'a2': jnp.concatenate([zeros64, rw_a2[l]], axis=0).astype(BF16),
        'g2': rw_g2[l].astype(BF16),
    }


def _rotary_tables(pos):
    half = HEAD_V // 2
    inv = ROPE_BASE ** (-jnp.arange(half, dtype=F32) / half)
    ang = pos[:, None] * inv[None, :]
    cos, sin = jnp.cos(ang), jnp.sin(ang)
    cosf = jnp.tile(jnp.concatenate([cos, cos], axis=1), (1, N_HEAD))
    sins = jnp.tile(jnp.concatenate([-sin, sin], axis=1), (1, N_HEAD))
    return cosf, sins


def _group_layer(x, states, shift, tables, lp, attn_nw, ffn_nw, final_nw, final, c, c_rw, nseg, seglen,
                 read_state):
    nb, length, _ = x.shape
    xf = x.reshape(nb * length, D_MODEL)
    proj = _inproj(xf, attn_nw, lp['wmix'], lp['layer'])
    o, *outs = _mixer(proj.reshape(nb // nseg, nseg * length, N_MIX), states, shift,
                      tables[0], tables[1], lp, c, c_rw, nseg, seglen, read_state)
    x2 = _post(xf, o.reshape(nb * length, N_BRANCH * BRANCH_W), attn_nw, lp['wg'], lp['wb'], lp['wo'],
               ffn_nw, lp['wi'], lp['wfo'], final_nw, final, lp['layer'])
    new_shift = proj.reshape(nb, length, N_MIX)[:, length - 1, C_RW:C_RW + RW_COLS]
    return x2.reshape(nb, length, D_MODEL), outs, new_shift


def _states_out(outs, shifts):
    n_hg, n_gl, n_rw, n_rt = outs
    return (n_hg, n_gl, n_rw, jnp.stack(shifts), n_rt)


def kernel(x_prompt, x_sample, state_hgrn, state_gla, state_rwkv, state_rwkv_shift, state_ret,
           attn_norm_w, w_in, hg_lb_logits, hg_norm_w, gla_wa2, gla_ba, gla_norm_w,
           rw_mu, rw_w0, rw_w2, rw_a0, rw_a2, rw_g2, rw_kk, rw_ka, rw_rk, rw_ln_w, rw_ln_b,
           w_branch, w_out, ffn_norm_w, w_ffn_in, w_ffn_out, final_norm_w):
    depth = w_in.shape[0]
    lb_p = jax.nn.softmax(hg_lb_logits.astype(F32), axis=0)
    lower_bounds = jnp.cumsum(lb_p, axis=0) - lb_p[0:1]

    bp, lp_len, _ = x_prompt.shape
    bs, ls_len, _ = x_sample.shape
    tab_p = _rotary_tables(jnp.arange(lp_len, dtype=F32))
    tab_s = _rotary_tables(float(PAST_LEN) + jnp.arange(ls_len, dtype=F32))
    tab_s = tuple(jnp.tile(t, (SAMPLE_SEQS_PER_TILE, 1)) for t in tab_s)
    final_nw = final_norm_w[None, :]
    wmix, wg = _prep_in_weights(w_in)
    dense = {'wmix': wmix, 'wg': wg, 'wb': w_branch, 'wo': w_out,
             'wi': w_ffn_in.astype(BF16), 'wfo': w_ffn_out.astype(BF16)}

    s_states = [state_hgrn, state_gla, state_rwkv, state_ret]
    p_states = [jnp.zeros((depth, bp) + s.shape[2:], F32) for s in s_states]
    s_shift = state_rwkv_shift.reshape(depth, bs, 1, RW_COLS)
    xp, xs = x_prompt, x_sample
    p_shifts, s_shifts = [], []
    for l in range(depth):
        lp = _layer_params(l, lower_bounds, hg_norm_w, gla_wa2, gla_ba, gla_norm_w, rw_mu, rw_w0,
                           rw_w2, rw_a0, rw_a2, rw_g2, rw_kk, rw_ka, rw_rk, rw_ln_w, rw_ln_b)
        lp.update(dense)
        lp['layer'] = l
        final = l == depth - 1
        anw, fnw = attn_norm_w[l][None, :], ffn_norm_w[l][None, :]
        xp, p_states, sh = _group_layer(xp, p_states, None, tab_p, lp, anw, fnw, final_nw, final, c=8,
                                        c_rw=32, nseg=1, seglen=PROMPT_TILE, read_state=False)
        p_shifts.append(sh)
        xs, s_states, sh = _group_layer(xs, s_states, s_shift, tab_s, lp, anw, fnw, final_nw, final,
                                        c=ls_len, c_rw=ls_len, nseg=SAMPLE_SEQS_PER_TILE, seglen=ls_len,
                                        read_state=True)
        s_shifts.append(sh)

    return (xp, xs) + _states_out(p_states, p_shifts) + _states_out(s_states, s_shifts)
```
